```python
import math
import jax, jax.numpy as jnp
from jax import lax
import numpy as np

D_MODEL = 2048
BATCH = 4
SEQ = 4096
DEPTH = 2

MEM_LEN = 256
CHUNK = 128
WINDOW = 128
G_GROUPS = 4
G_WIDTH = 1024
G_GROUP_DIM = G_WIDTH // G_GROUPS
SWA_HEADS = 16
SWA_KV_HEADS = 4
SWA_HEAD_DIM = 64
SWA_REP = SWA_HEADS // SWA_KV_HEADS
SWA_WIDTH = SWA_HEADS * SWA_HEAD_DIM
SWA_KV_WIDTH = SWA_KV_HEADS * SWA_HEAD_DIM
MEM_HEADS = 4
MEM_HEAD_DIM = 256
MEM_WIDTH = MEM_HEADS * MEM_HEAD_DIM
N_BRANCH = 3
BRANCH_WIDTH = 1024
D_FF = 5504
EPS = 1e-6
NEG = -1e30
IN_WIDTH = 2 * G_WIDTH + SWA_WIDTH + 2 * SWA_KV_WIDTH + MEM_WIDTH + N_BRANCH * D_MODEL

kernel_name = "hybrid_gated_gmlp_swa_memattn_macaron"


def rmsnorm(x, g):
    xf = x.astype(jnp.float32)
    y = xf * lax.rsqrt(jnp.mean(xf * xf, axis=-1, keepdims=True) + EPS)
    return (y * g.astype(jnp.float32)).astype(x.dtype)


def layernorm(x, g, b):
    xf = x.astype(jnp.float32)
    mu = jnp.mean(xf, axis=-1, keepdims=True)
    xc = xf - mu
    var = jnp.mean(xc * xc, axis=-1, keepdims=True)
    y = xc * lax.rsqrt(var + EPS) * g.astype(jnp.float32) + b.astype(jnp.float32)
    return y.astype(x.dtype)


def swiglu(x, w_in, w_out):
    a, b = jnp.split(x @ w_in, 2, axis=-1)
    return (jax.nn.silu(a) * b) @ w_out


def gmlp_branch(u, v, w_s, b_s, ln_g, ln_b):
    B, S, _ = u.shape
    n_chunks = S // CHUNK
    vn = layernorm(v, ln_g, ln_b).reshape(B, n_chunks, CHUNK, G_GROUPS, G_GROUP_DIM)
    causal = jnp.tril(jnp.ones((CHUNK, CHUNK), dtype=bool))
    ws = jnp.where(causal[None], w_s, jnp.zeros((), w_s.dtype))
    mixed = jnp.einsum('gts,bcsgd->bctgd', ws, vn) + jnp.transpose(b_s)[None, None, :, :, None]
    return u * mixed.reshape(B, S, G_WIDTH)


def swa_branch(q, k, v, sinks):
    B, S = q.shape[0], q.shape[1]
    nb = S // WINDOW
    qb = q.reshape(B, nb, WINDOW, SWA_KV_HEADS, SWA_REP, SWA_HEAD_DIM)

    def band(t):
        prev = jnp.pad(t, ((0, 0), (WINDOW, 0), (0, 0), (0, 0)))[:, :S]
        prev = prev.reshape(B, nb, WINDOW, SWA_KV_HEADS, SWA_HEAD_DIM)
        cur = t.reshape(B, nb, WINDOW, SWA_KV_HEADS, SWA_HEAD_DIM)
        return jnp.concatenate([prev, cur], axis=2)

    kb, vb = band(k), band(v)
    scale = 1.0 / math.sqrt(SWA_HEAD_DIM)
    scores = jnp.einsum('bcihrd,bcjhd->bchrij', qb, kb).astype(jnp.float32) * scale
    i = jnp.arange(WINDOW)[:, None]
    j = jnp.arange(2 * WINDOW)[None, :]
    dist = i + WINDOW - j
    key_pos = jnp.arange(nb)[:, None, None] * WINDOW - WINDOW + j[None]
    valid = (dist >= 0)[None] & (dist < WINDOW)[None] & (key_pos >= 0)
    slopes = jnp.exp2(-8.0 * jnp.arange(1, SWA_HEADS + 1, dtype=jnp.float32) / SWA_HEADS)
    slopes = slopes.reshape(SWA_KV_HEADS, SWA_REP)
    alibi = -slopes[:, :, None, None] * dist.astype(jnp.float32)[None, None]
    scores = jnp.where(valid[None, :, None, None], scores + alibi[None, None], NEG)
    sink = sinks.astype(jnp.float32).reshape(SWA_KV_HEADS, SWA_REP)[None, None, :, :, None, None]
    m = jnp.maximum(jnp.max(scores, axis=-1, keepdims=True), sink)
    p = jnp.exp(scores - m)
    denom = jnp.sum(p, axis=-1, keepdims=True) + jnp.exp(sink - m)
    probs = (p / denom).astype(vb.dtype)
    out = jnp.einsum('bchrij,bcjhd->bcihrd', probs, vb)
    return out.reshape(B, S, SWA_WIDTH)


def mem_branch(q, mk, mv):
    B, S = q.shape[0], q.shape[1]
    scale = 1.0 / math.sqrt(MEM_HEAD_DIM)
    scores = jnp.einsum('bshd,bmhd->bhsm', q, mk).astype(jnp.float32) * scale
    probs = jax.nn.softmax(scores, axis=-1).astype(mv.dtype)
    return jnp.einsum('bhsm,bmhd->bshd', probs, mv).reshape(B, S, MEM_WIDTH)


def setup_inputs(seed: int = 0) -> dict:
    key = jax.random.key(seed)
    ks = jax.random.split(key, 24)
    f32 = jnp.float32
    nrm = lambda k, shape, s: jax.random.normal(k, shape, f32) * s
    gain = lambda k, shape: 1.0 + 0.02 * jax.random.normal(k, shape, f32)
    return {
        "x": nrm(ks[0], (BATCH, SEQ, D_MODEL), 1.0),
        "mem": nrm(ks[1], (BATCH, MEM_LEN, D_MODEL), 1.0),
        "g_ffn1": gain(ks[2], (DEPTH, D_MODEL)),
        "w_ffn1_in": nrm(ks[3], (DEPTH, D_MODEL, 2 * D_FF), D_MODEL ** -0.5),
        "w_ffn1_out": nrm(ks[4], (DEPTH, D_FF, D_MODEL), D_FF ** -0.5),
        "g_mix": gain(ks[5], (DEPTH, D_MODEL)),
        "w_in": nrm(ks[6], (DEPTH, D_MODEL, IN_WIDTH), D_MODEL ** -0.5),
        "gmlp_ln_g": gain(ks[7], (DEPTH, G_WIDTH)),
        "gmlp_ln_b": nrm(ks[8], (DEPTH, G_WIDTH), 0.02),
        "w_s": nrm(ks[9], (DEPTH, G_GROUPS, CHUNK, CHUNK), CHUNK ** -0.5),
        "b_s": gain(ks[10], (DEPTH, G_GROUPS, CHUNK)),
        "swa_sinks": nrm(ks[11], (DEPTH, SWA_HEADS), 0.5),
        "g_mem": gain(ks[12], (DEPTH, D_MODEL)),
        "w_mem_kv": nrm(ks[13], (DEPTH, D_MODEL, 2 * MEM_WIDTH), D_MODEL ** -0.5),
        "w_branch": nrm(ks[14], (DEPTH, N_BRANCH, BRANCH_WIDTH, D_MODEL), BRANCH_WIDTH ** -0.5),
        "w_out": nrm(ks[15], (DEPTH, D_MODEL, D_MODEL), D_MODEL ** -0.5),
        "g_ffn2": gain(ks[16], (DEPTH, D_MODEL)),
        "w_ffn2_in": nrm(ks[17], (DEPTH, D_MODEL, 2 * D_FF), D_MODEL ** -0.5),
        "w_ffn2_out": nrm(ks[18], (DEPTH, D_FF, D_MODEL), D_FF ** -0.5),
        "g_final": gain(ks[19], (D_MODEL,)),
    }


def reference(x, mem, g_ffn1, w_ffn1_in, w_ffn1_out, g_mix, w_in, gmlp_ln_g, gmlp_ln_b,
              w_s, b_s, swa_sinks, g_mem, w_mem_kv, w_branch, w_out, g_ffn2, w_ffn2_in,
              w_ffn2_out, g_final):
    B, S, _ = x.shape
    split_at = list(np.cumsum([G_WIDTH, G_WIDTH, SWA_WIDTH, SWA_KV_WIDTH, SWA_KV_WIDTH, MEM_WIDTH]))
    for l in range(DEPTH):
        h = x + 0.5 * swiglu(rmsnorm(x, g_ffn1[l]), w_ffn1_in[l], w_ffn1_out[l])
        n = rmsnorm(h, g_mix[l])
        z = n @ w_in[l]
        z_u, z_v, z_q, z_k, z_vv, z_mq, z_gate = jnp.split(z, split_at, axis=-1)
        o_a = gmlp_branch(jax.nn.gelu(z_u, approximate=False), jax.nn.gelu(z_v, approximate=False),
                          w_s[l], b_s[l], gmlp_ln_g[l], gmlp_ln_b[l])
        o_b = swa_branch(z_q.reshape(B, S, SWA_HEADS, SWA_HEAD_DIM),
                         z_k.reshape(B, S, SWA_KV_HEADS, SWA_HEAD_DIM),
                         z_vv.reshape(B, S, SWA_KV_HEADS, SWA_HEAD_DIM),
                         swa_sinks[l])
        mkv = rmsnorm(mem, g_mem[l]) @ w_mem_kv[l]
        mk, mv = jnp.split(mkv, 2, axis=-1)
        o_c = mem_branch(z_mq.reshape(B, S, MEM_HEADS, MEM_HEAD_DIM),
                         mk.reshape(B, MEM_LEN, MEM_HEADS, MEM_HEAD_DIM),
                         mv.reshape(B, MEM_LEN, MEM_HEADS, MEM_HEAD_DIM))
        gates = jax.nn.sigmoid(z_gate.reshape(B, S, N_BRANCH, D_MODEL))
        y = (gates[:, :, 0] * (o_a @ w_branch[l, 0])
             + gates[:, :, 1] * (o_b @ w_branch[l, 1])
             + gates[:, :, 2] * (o_c @ w_branch[l, 2]))
        h = h + y @ w_out[l]
        x = h + 0.5 * swiglu(rmsnorm(h, g_ffn2[l]), w_ffn2_in[l], w_ffn2_out[l])
    return rmsnorm(x, g_final)
```

```python
import functools
import math

import jax
import jax.numpy as jnp
from jax import lax
from jax.experimental import pallas as pl
from jax.experimental.pallas import tpu as pltpu

D_MODEL = 2048
MEM_LEN = 256
CHUNK = 128
WINDOW = 128
G_GROUPS = 4
G_WIDTH = 1024
G_GROUP_DIM = G_WIDTH // G_GROUPS
SWA_HEADS = 16
SWA_KV_HEADS = 4
SWA_HEAD_DIM = 64
SWA_REP = SWA_HEADS // SWA_KV_HEADS
SWA_WIDTH = SWA_HEADS * SWA_HEAD_DIM
SWA_KV_WIDTH = SWA_KV_HEADS * SWA_HEAD_DIM
MEM_HEADS = 4
MEM_HEAD_DIM = 256
MEM_WIDTH = MEM_HEADS * MEM_HEAD_DIM
N_BRANCH = 3
BRANCH_WIDTH = 1024
D_FF = 5504
EPS = 1e-6
NEG = -1e30

F32 = jnp.float32
BF16 = jnp.bfloat16

LANES = 128
VMEM_LIMIT_BYTES = 56 * 1024 * 1024

FFN_TM = 512
FFN_TF = 512
D_FF_PAD = -(-D_FF // FFN_TF) * FFN_TF
PROJ_TM = 512
MERGE_TM = 256
SWA_TQ = 512
MEM_TQ = 1024


def _params(*semantics):
    return pltpu.CompilerParams(dimension_semantics=semantics, vmem_limit_bytes=VMEM_LIMIT_BYTES)


def _rmsnorm_f32(xf, g):
    return xf * lax.rsqrt(jnp.mean(xf * xf, axis=-1, keepdims=True) + EPS) * g


def _dot(a, b):
    return jnp.dot(a, b, preferred_element_type=F32)


def _dot_nt(a, b):
    return lax.dot_general(a, b, (((1,), (1,)), ((), ())), preferred_element_type=F32)


def _ffn_body(x_ref, g_ref, wa_ref, wb_ref, wo_ref, gf_ref, o_ref, n_ref, *, apply_final_norm):
    j = pl.program_id(1)

    @pl.when(j == 0)
    def _():
        n_ref[...] = _rmsnorm_f32(x_ref[...], g_ref[...]).astype(BF16)
        o_ref[...] = jnp.zeros_like(o_ref)

    n = n_ref[...]
    a = _dot(n, wa_ref[...])
    b = _dot(n, wb_ref[...])
    hid = (a * jax.nn.sigmoid(a) * b).astype(BF16)
    o_ref[...] += _dot(hid, wo_ref[...])

    @pl.when(j == pl.num_programs(1) - 1)
    def _():
        y = x_ref[...] + 0.5 * o_ref[...]
        if apply_final_norm:
            y = _rmsnorm_f32(y, gf_ref[...])
        o_ref[...] = y


def _ffn(x, g, w_in, w_out, g_final, apply_final_norm):
    t, d = x.shape
    pad = D_FF_PAD - D_FF
    wa = jnp.pad(w_in[:, :D_FF].astype(BF16), ((0, 0), (0, pad)))
    wb = jnp.pad(w_in[:, D_FF:].astype(BF16), ((0, 0), (0, pad)))
    wo = jnp.pad(w_out.astype(BF16), ((0, pad), (0, 0)))
    grid = (t // FFN_TM, D_FF_PAD // FFN_TF)
    return pl.pallas_call(
        functools.partial(_ffn_body, apply_final_norm=apply_final_norm),
        grid=grid,
        in_specs=[
            pl.BlockSpec((FFN_TM, d), lambda i, j: (i, 0)),
            pl.BlockSpec((1, d), lambda i, j: (0, 0)),
            pl.BlockSpec((d, FFN_TF), lambda i, j: (0, j)),
            pl.BlockSpec((d, FFN_TF), lambda i, j: (0, j)),
            pl.BlockSpec((FFN_TF, d), lambda i, j: (j, 0)),
            pl.BlockSpec((1, d), lambda i, j: (0, 0)),
        ],
        out_specs=pl.BlockSpec((FFN_TM, d), lambda i, j: (i, 0)),
        out_shape=jax.ShapeDtypeStruct((t, d), F32),
        scratch_shapes=[pltpu.VMEM((FFN_TM, d), BF16)],
        compiler_params=_params("parallel", "arbitrary"),
        name="ffn",
    )(x, g.reshape(1, d), wa, wb, wo, g_final.reshape(1, d))


def _gelu(x):
    return 0.5 * x * (1.0 + lax.erf(x * math.sqrt(0.5)))


def _uv_body(h_ref, g_ref, wu_ref, wv_ref, lng_ref, lnb_ref, ws_ref, bst_ref, n_ref, oa_ref):
    n = _rmsnorm_f32(h_ref[...], g_ref[...]).astype(BF16)
    n_ref[...] = n
    u = _gelu(_dot(n, wu_ref[...]))
    v = _gelu(_dot(n, wv_ref[...]))
    mu = jnp.mean(v, axis=-1, keepdims=True)
    vc = v - mu
    var = jnp.mean(vc * vc, axis=-1, keepdims=True)
    vn = (vc * lax.rsqrt(var + EPS) * lng_ref[...] + lnb_ref[...]).astype(BF16)

    row = lax.broadcasted_iota(jnp.int32, (CHUNK, CHUNK), 0)
    col = lax.broadcasted_iota(jnp.int32, (CHUNK, CHUNK), 1)
    causal = row >= col
    for g in range(G_GROUPS):
        ws = jnp.where(causal, ws_ref[g], 0.0).astype(BF16)
        bias = bst_ref[:, g:g + 1]
        cs = slice(g * G_GROUP_DIM, (g + 1) * G_GROUP_DIM)
        for c in range(PROJ_TM // CHUNK):
            rs = slice(c * CHUNK, (c + 1) * CHUNK)
            mixed = _dot(ws, vn[rs, cs]) + bias
            oa_ref[rs, cs] = (u[rs, cs] * mixed).astype(BF16)


def _uv(h, g, w_uv, ln_g, ln_b, w_s, b_s):
    t, d = h.shape
    wu = w_uv[:, :G_WIDTH].astype(BF16)
    wv = w_uv[:, G_WIDTH:].astype(BF16)
    const = lambda i: (0, 0)
    return pl.pallas_call(
        _uv_body,
        grid=(t // PROJ_TM,),
        in_specs=[
            pl.BlockSpec((PROJ_TM, d), lambda i: (i, 0)),
            pl.BlockSpec((1, d), const),
            pl.BlockSpec((d, G_WIDTH), const),
            pl.BlockSpec((d, G_WIDTH), const),
            pl.BlockSpec((1, G_WIDTH), const),
            pl.BlockSpec((1, G_WIDTH), const),
            pl.BlockSpec((G_GROUPS, CHUNK, CHUNK), lambda i: (0, 0, 0)),
            pl.BlockSpec((CHUNK, G_GROUPS), const),
        ],
        out_specs=[
            pl.BlockSpec((PROJ_TM, d), lambda i: (i, 0)),
            pl.BlockSpec((PROJ_TM, G_WIDTH), lambda i: (i, 0)),
        ],
        out_shape=[
            jax.ShapeDtypeStruct((t, d), BF16),
            jax.ShapeDtypeStruct((t, G_WIDTH), BF16),
        ],
        compiler_params=_params("parallel"),
        name="uv_gmlp",
    )(h, g.reshape(1, d), wu, wv, ln_g.reshape(1, G_WIDTH), ln_b.reshape(1, G_WIDTH), w_s,
      jnp.transpose(b_s))


QKVM_WIDTH = SWA_WIDTH + 2 * SWA_KV_WIDTH + MEM_WIDTH


def _qkvm_body(n_ref, w_ref, q_ref, k_ref, v_ref, mq_ref):
    z = _dot(n_ref[...], w_ref[...]).astype(BF16)
    q_ref[...] = z[:, :SWA_WIDTH]
    k_ref[...] = z[:, SWA_WIDTH:SWA_WIDTH + SWA_KV_WIDTH]
    v_ref[...] = z[:, SWA_WIDTH + SWA_KV_WIDTH:SWA_WIDTH + 2 * SWA_KV_WIDTH]
    mq_ref[...] = z[:, SWA_WIDTH + 2 * SWA_KV_WIDTH:]


def _qkvm(n, w):
    t, d = n.shape
    widths = (SWA_WIDTH, SWA_KV_WIDTH, SWA_KV_WIDTH, MEM_WIDTH)
    return pl.pallas_call(
        _qkvm_body,
        grid=(t // PROJ_TM,),
        in_specs=[
            pl.BlockSpec((PROJ_TM, d), lambda i: (i, 0)),
            pl.BlockSpec((d, QKVM_WIDTH), lambda i: (0, 0)),
        ],
        out_specs=[pl.BlockSpec((PROJ_TM, w_), lambda i: (i, 0)) for w_ in widths],
        out_shape=[jax.ShapeDtypeStruct((t, w_), BF16) for w_ in widths],
        compiler_params=_params("parallel"),
        name="qkvm",
    )(n, w.astype(BF16))


def _gates_body(n_ref, w_ref, o_ref):
    o_ref[...] = jax.nn.sigmoid(_dot(n_ref[...], w_ref[...])).astype(BF16)


def _gates(n, w):
    t, d = n.shape
    return pl.pallas_call(
        _gates_body,
        grid=(N_BRANCH, t // PROJ_TM),
        in_specs=[
            pl.BlockSpec((PROJ_TM, d), lambda b, i: (i, 0)),
            pl.BlockSpec((d, d), lambda b, i: (0, b)),
        ],
        out_specs=pl.BlockSpec((None, PROJ_TM, d), lambda b, i: (b, i, 0)),
        out_shape=jax.ShapeDtypeStruct((N_BRANCH, t, d), BF16),
        compiler_params=_params("parallel", "parallel"),
        name="gates",
    )(n, w.astype(BF16))


def _mkv_body(m_ref, g_ref, w_ref, mk_ref, mv_ref):
    n = _rmsnorm_f32(m_ref[...], g_ref[...]).astype(BF16)
    z = _dot(n, w_ref[...]).astype(BF16)
    mk_ref[...] = z[:, :MEM_WIDTH]
    mv_ref[...] = z[:, MEM_WIDTH:]


def _mkv(mem, g, w):
    t, d = mem.shape
    return pl.pallas_call(
        _mkv_body,
        grid=(t // MEM_LEN,),
        in_specs=[
            pl.BlockSpec((MEM_LEN, d), lambda i: (i, 0)),
            pl.BlockSpec((1, d), lambda i: (0, 0)),
            pl.BlockSpec((d, 2 * MEM_WIDTH), lambda i: (0, 0)),
        ],
        out_specs=[pl.BlockSpec((MEM_LEN, MEM_WIDTH), lambda i: (i, 0))] * 2,
        out_shape=[jax.ShapeDtypeStruct((t, MEM_WIDTH), BF16)] * 2,
        compiler_params=_params("parallel"),
        name="mem_kv",
    )(mem, g.reshape(1, d), w.astype(BF16))


SWA_SLOPES = tuple(2.0 ** (-8.0 * (h + 1) / SWA_HEADS) for h in range(SWA_HEADS))
SWA_SCALE = 1.0 / math.sqrt(SWA_HEAD_DIM)


def _swa_body(sink_ref, q_ref, k_ref, v_ref, kp_ref, vp_ref, o_ref):
    t = pl.program_id(1)
    i = lax.broadcasted_iota(jnp.int32, (WINDOW, 2 * WINDOW), 0)
    j = lax.broadcasted_iota(jnp.int32, (WINDOW, 2 * WINDOW), 1)
    dist = i + WINDOW - j
    in_window = (dist >= 0) & (dist < WINDOW)
    distf = dist.astype(F32)
    first_valid = in_window & ((j >= WINDOW) | (t > 0))

    for blk in range(SWA_TQ // WINDOW):
        rs = slice(blk * WINDOW, (blk + 1) * WINDOW)
        if blk == 0:
            k_prev, v_prev, valid = kp_ref[...], vp_ref[...], first_valid
        else:
            ps = slice((blk - 1) * WINDOW, blk * WINDOW)
            k_prev, v_prev, valid = k_ref[ps, :], v_ref[ps, :], in_window
        kb = jnp.concatenate([k_prev, k_ref[rs, :]], axis=0)
        vb = jnp.concatenate([v_prev, v_ref[rs, :]], axis=0)
        outs = []
        for head in range(SWA_HEADS):
            kvh = head // SWA_REP
            hs = slice(kvh * SWA_HEAD_DIM, (kvh + 1) * SWA_HEAD_DIM)
            qh = q_ref[rs, head * SWA_HEAD_DIM:(head + 1) * SWA_HEAD_DIM]
            s = _dot_nt(qh, kb[:, hs]) * SWA_SCALE
            s = jnp.where(valid, s - SWA_SLOPES[head] * distf, NEG)
            sink = sink_ref[head]
            m = jnp.maximum(jnp.max(s, axis=-1, keepdims=True), sink)
            p = jnp.exp(s - m)
            denom = jnp.sum(p, axis=-1, keepdims=True) + jnp.exp(sink - m)
            probs = (p / denom).astype(BF16)
            outs.append(_dot(probs, vb[:, hs]))
        o_ref[rs, :] = jnp.concatenate(outs, axis=1).astype(BF16)


def _swa(q, k, v, sinks, batch, seq):
    t = q.shape[0]
    steps = seq // SWA_TQ
    blocks_per_step = SWA_TQ // WINDOW
    blocks_per_seq = seq // WINDOW

    def cur(b, s):
        return (b * steps + s, 0)

    def prev(b, s):
        return (b * blocks_per_seq + jnp.maximum(s * blocks_per_step - 1, 0), 0)

    return pl.pallas_call(
        _swa_body,
        grid=(batch, steps),
        in_specs=[
            pl.BlockSpec(memory_space=pltpu.SMEM),
            pl.BlockSpec((SWA_TQ, SWA_WIDTH), cur),
            pl.BlockSpec((SWA_TQ, SWA_KV_WIDTH), cur),
            pl.BlockSpec((SWA_TQ, SWA_KV_WIDTH), cur),
            pl.BlockSpec((WINDOW, SWA_KV_WIDTH), prev),
            pl.BlockSpec((WINDOW, SWA_KV_WIDTH), prev),
        ],
        out_specs=pl.BlockSpec((SWA_TQ, SWA_WIDTH), cur),
        out_shape=jax.ShapeDtypeStruct((t, SWA_WIDTH), BF16),
        compiler_params=_params("parallel", "parallel"),
        name="swa",
    )(sinks, q, k, v, k, v)


MEM_SCALE = 1.0 / math.sqrt(MEM_HEAD_DIM)


def _memattn_body(q_ref, mk_ref, mv_ref, o_ref):
    for h in range(MEM_HEADS):
        hs = slice(h * MEM_HEAD_DIM, (h + 1) * MEM_HEAD_DIM)
        s = _dot_nt(q_ref[:, hs], mk_ref[:, hs]) * MEM_SCALE
        m = jnp.max(s, axis=-1, keepdims=True)
        p = jnp.exp(s - m)
        probs = (p / jnp.sum(p, axis=-1, keepdims=True)).astype(BF16)
        o_ref[:, hs] = _dot(probs, mv_ref[:, hs]).astype(BF16)


def _memattn(mq, mk, mv, batch, seq):
    t = mq.shape[0]
    steps = seq // MEM_TQ
    return pl.pallas_call(
        _memattn_body,
        grid=(batch, steps),
        in_specs=[
            pl.BlockSpec((MEM_TQ, MEM_WIDTH), lambda b, s: (b * steps + s, 0)),
            pl.BlockSpec((MEM_LEN, MEM_WIDTH), lambda b, s: (b, 0)),
            pl.BlockSpec((MEM_LEN, MEM_WIDTH), lambda b, s: (b, 0)),
        ],
        out_specs=pl.BlockSpec((MEM_TQ, MEM_WIDTH), lambda b, s: (b * steps + s, 0)),
        out_shape=jax.ShapeDtypeStruct((t, MEM_WIDTH), BF16),
        compiler_params=_params("parallel", "parallel"),
        name="mem_attn",
    )(mq, mk, mv)


def _merge_body(h_ref, gt_ref, oa_ref, ob_ref, oc_ref, wbr_ref, wo_ref, o_ref):
    y = gt_ref[0].astype(F32) * _dot(oa_ref[...], wbr_ref[0])
    y += gt_ref[1].astype(F32) * _dot(ob_ref[...], wbr_ref[1])
    y += gt_ref[2].astype(F32) * _dot(oc_ref[...], wbr_ref[2])
    o_ref[...] = h_ref[...] + _dot(y.astype(BF16), wo_ref[...])


def _merge(h, gates, o_a, o_b, o_c, w_branch, w_out):
    t, d = h.shape
    row = lambda i: (i, 0)
    single = pl.Buffered(1)
    return pl.pallas_call(
        _merge_body,
        grid=(t // MERGE_TM,),
        in_specs=[
            pl.BlockSpec((MERGE_TM, d), row),
            pl.BlockSpec((N_BRANCH, MERGE_TM, d), lambda i: (0, i, 0)),
            pl.BlockSpec((MERGE_TM, BRANCH_WIDTH), row),
            pl.BlockSpec((MERGE_TM, BRANCH_WIDTH), row),
            pl.BlockSpec((MERGE_TM, BRANCH_WIDTH), row),
            pl.BlockSpec((N_BRANCH, BRANCH_WIDTH, d), lambda i: (0, 0, 0), pipeline_mode=single),
            pl.BlockSpec((d, d), lambda i: (0, 0), pipeline_mode=single),
        ],
        out_specs=pl.BlockSpec((MERGE_TM, d), row),
        out_shape=jax.ShapeDtypeStruct((t, d), F32),
        compiler_params=_params("parallel"),
        name="merge",
    )(h, gates, o_a, o_b, o_c, w_branch.astype(BF16), w_out.astype(BF16))


UV_END = 2 * G_WIDTH
QKVM_END = UV_END + QKVM_WIDTH


def kernel(x, mem, g_ffn1, w_ffn1_in, w_ffn1_out, g_mix, w_in, gmlp_ln_g, gmlp_ln_b, w_s, b_s, swa_sinks, g_mem, w_mem_kv, w_branch, w_out, g_ffn2, w_ffn2_in, w_ffn2_out, g_final):
    batch, seq, d = x.shape
    depth = w_in.shape[0]
    xt = x.reshape(batch * seq, d)
    memt = mem.reshape(batch * MEM_LEN, d)
    for l in range(depth):
        h = _ffn(xt, g_ffn1[l], w_ffn1_in[l], w_ffn1_out[l], g_final, False)
        n, o_a = _uv(h, g_mix[l], w_in[l][:, :UV_END], gmlp_ln_g[l], gmlp_ln_b[l], w_s[l], b_s[l])
        q, k, v, mq = _qkvm(n, w_in[l][:, UV_END:QKVM_END])
        gates = _gates(n, w_in[l][:, QKVM_END:])
        mk, mv = _mkv(memt, g_mem[l], w_mem_kv[l])
        o_b = _swa(q, k, v, swa_sinks[l], batch, seq)
        o_c = _memattn(mq, mk, mv, batch, seq)
        h = _merge(h, gates, o_a, o_b, o_c, w_branch[l], w_out[l])
        xt = _ffn(h, g_ffn2[l], w_ffn2_in[l], w_ffn2_out[l], g_final, l == depth - 1)
    return xt.reshape(batch, seq, d)
```

```python
import functools
import math

import jax
import jax.numpy as jnp
from jax import lax
from jax.experimental import pallas as pl
from jax.experimental.pallas import tpu as pltpu

D_MODEL = 2048
MEM_LEN = 256
CHUNK = 128
WINDOW = 128
G_GROUPS = 4
G_WIDTH = 1024
G_GROUP_DIM = G_WIDTH // G_GROUPS
SWA_HEADS = 16
SWA_KV_HEADS = 4
SWA_HEAD_DIM = 64
SWA_REP = SWA_HEADS // SWA_KV_HEADS
SWA_WIDTH = SWA_HEADS * SWA_HEAD_DIM
SWA_KV_WIDTH = SWA_KV_HEADS * SWA_HEAD_DIM
MEM_HEADS = 4
MEM_HEAD_DIM = 256
MEM_WIDTH = MEM_HEADS * MEM_HEAD_DIM
N_BRANCH = 3
BRANCH_WIDTH = 1024
D_FF = 5504
EPS = 1e-6
NEG = -1e30

F32 = jnp.float32
BF16 = jnp.bfloat16

LANES = 128
VMEM_LIMIT_BYTES = 56 * 1024 * 1024

FFN_TM = 512
FFN_TF = 512
D_FF_PAD = -(-D_FF // FFN_TF) * FFN_TF
PROJ_TM = 512
GATES_TM = 1024
MERGE_TM = 256
SWA_TQ = 512
MEM_TQ = 1024
CAST_ROWS = 512
CAST_COLS = 512

SINGLE = pl.Buffered(1)


def _params(*semantics):
    return pltpu.CompilerParams(dimension_semantics=semantics, vmem_limit_bytes=VMEM_LIMIT_BYTES)


def _rmsnorm_f32(xf, g):
    return xf * lax.rsqrt(jnp.mean(xf * xf, axis=-1, keepdims=True) + EPS) * g


def _dot(a, b):
    return jnp.dot(a, b, preferred_element_type=F32)


def _dot_nt(a, b):
    return lax.dot_general(a, b, (((1,), (1,)), ((), ())), preferred_element_type=F32)


def _cast_body(w_ref, o_ref):
    o_ref[...] = w_ref[...].astype(BF16)


def _cast_cols(w, l, col0, ncols, rows_blk, cols_blk):
    _, rows, _ = w.shape
    assert rows % rows_blk == 0 and col0 % cols_blk == 0 and ncols % cols_blk == 0
    off = col0 // cols_blk
    return pl.pallas_call(
        _cast_body,
        grid=(rows // rows_blk, ncols // cols_blk),
        in_specs=[pl.BlockSpec((None, rows_blk, cols_blk), lambda i, j: (l, i, j + off))],
        out_specs=pl.BlockSpec((rows_blk, cols_blk), lambda i, j: (i, j)),
        out_shape=jax.ShapeDtypeStruct((rows, ncols), BF16),
        compiler_params=_params("parallel", "parallel"),
        name="cast_cols",
    )(w)


def _cast_ffn_in_body(w_ref, o_ref):
    o_ref[:, :D_FF] = w_ref[...].astype(BF16)
    o_ref[:, D_FF:] = jnp.zeros((o_ref.shape[0], D_FF_PAD - D_FF), BF16)


def _cast_ffn_in(w, l):
    _, d, _ = w.shape
    rows_blk = CAST_ROWS // 2
    return pl.pallas_call(
        _cast_ffn_in_body,
        grid=(2, d // rows_blk),
        in_specs=[pl.BlockSpec((None, rows_blk, D_FF), lambda s, i: (l, i, s))],
        out_specs=pl.BlockSpec((None, rows_blk, D_FF_PAD), lambda s, i: (s, i, 0)),
        out_shape=jax.ShapeDtypeStruct((2, d, D_FF_PAD), BF16),
        compiler_params=_params("parallel", "parallel"),
        name="cast_ffn_in",
    )(w)


FFN_OUT_CAST_ROWS = D_FF // 8


def _cast_ffn_out_body(w_ref, o_ref):
    is_param = pl.program_id(0) < D_FF // FFN_OUT_CAST_ROWS

    @pl.when(is_param)
    def _():
        o_ref[...] = w_ref[...].astype(BF16)

    @pl.when(jnp.logical_not(is_param))
    def _():
        o_ref[...] = jnp.zeros_like(o_ref)


def _cast_ffn_out(w, l):
    _, _, d = w.shape
    n_src = D_FF // FFN_OUT_CAST_ROWS
    return pl.pallas_call(
        _cast_ffn_out_body,
        grid=(pl.cdiv(D_FF_PAD, FFN_OUT_CAST_ROWS),),
        in_specs=[pl.BlockSpec((None, FFN_OUT_CAST_ROWS, d), lambda i: (l, jnp.minimum(i, n_src - 1), 0))],
        out_specs=pl.BlockSpec((FFN_OUT_CAST_ROWS, d), lambda i: (i, 0)),
        out_shape=jax.ShapeDtypeStruct((D_FF_PAD, d), BF16),
        compiler_params=_params("parallel"),
        name="cast_ffn_out",
    )(w)


def _ffn_body(x_ref, g_ref, wa_ref, wb_ref, wo_ref, gf_ref, o_ref, n_ref, *, apply_final_norm):
    j = pl.program_id(1)

    @pl.when(j == 0)
    def _():
        n_ref[...] = _rmsnorm_f32(x_ref[...], g_ref[...]).astype(BF16)
        o_ref[...] = jnp.zeros_like(o_ref)

    n = n_ref[...]
    a = _dot(n, wa_ref[...])
    b = _dot(n, wb_ref[...])
    hid = (a * jax.nn.sigmoid(a) * b).astype(BF16)
    o_ref[...] += _dot(hid, wo_ref[...])

    @pl.when(j == pl.num_programs(1) - 1)
    def _():
        y = x_ref[...] + 0.5 * o_ref[...]
        if apply_final_norm:
            y = _rmsnorm_f32(y, gf_ref[...])
        o_ref[...] = y


def _ffn(x, g, w_in, w_out, l, g_final, apply_final_norm):
    t, d = x.shape
    wab = _cast_ffn_in(w_in, l)
    wo = _cast_ffn_out(w_out, l)
    grid = (t // FFN_TM, D_FF_PAD // FFN_TF)
    return pl.pallas_call(
        functools.partial(_ffn_body, apply_final_norm=apply_final_norm),
        grid=grid,
        in_specs=[
            pl.BlockSpec((FFN_TM, d), lambda i, j: (i, 0)),
            pl.BlockSpec((1, d), lambda i, j: (0, 0)),
            pl.BlockSpec((None, d, FFN_TF), lambda i, j: (0, 0, j)),
            pl.BlockSpec((None, d, FFN_TF), lambda i, j: (1, 0, j)),
            pl.BlockSpec((FFN_TF, d), lambda i, j: (j, 0)),
            pl.BlockSpec((1, d), lambda i, j: (0, 0)),
        ],
        out_specs=pl.BlockSpec((FFN_TM, d), lambda i, j: (i, 0)),
        out_shape=jax.ShapeDtypeStruct((t, d), F32),
        scratch_shapes=[pltpu.VMEM((FFN_TM, d), BF16)],
        compiler_params=_params("parallel", "arbitrary"),
        name="ffn",
    )(x, g.reshape(1, d), wab, wab, wo, g_final.reshape(1, d))


def _gelu(x):
    return 0.5 * x * (1.0 + lax.erf(x * math.sqrt(0.5)))


def _uv_body(h_ref, g_ref, wu_ref, wv_ref, lng_ref, lnb_ref, ws_ref, bst_ref, n_ref, oa_ref):
    n = _rmsnorm_f32(h_ref[...], g_ref[...]).astype(BF16)
    n_ref[...] = n
    u = _gelu(_dot(n, wu_ref[...]))
    v = _gelu(_dot(n, wv_ref[...]))
    mu = jnp.mean(v, axis=-1, keepdims=True)
    vc = v - mu
    var = jnp.mean(vc * vc, axis=-1, keepdims=True)
    vn = (vc * lax.rsqrt(var + EPS) * lng_ref[...] + lnb_ref[...]).astype(BF16)

    row = lax.broadcasted_iota(jnp.int32, (CHUNK, CHUNK), 0)
    col = lax.broadcasted_iota(jnp.int32, (CHUNK, CHUNK), 1)
    causal = row >= col
    for g in range(G_GROUPS):
        ws = jnp.where(causal, ws_ref[g], 0.0).astype(BF16)
        bias = bst_ref[:, g:g + 1]
        cs = slice(g * G_GROUP_DIM, (g + 1) * G_GROUP_DIM)
        for c in range(PROJ_TM // CHUNK):
            rs = slice(c * CHUNK, (c + 1) * CHUNK)
            mixed = _dot(ws, vn[rs, cs]) + bias
            oa_ref[rs, cs] = (u[rs, cs] * mixed).astype(BF16)


def _uv(h, g, w_uv, ln_g, ln_b, w_s, b_s):
    t, d = h.shape
    const = lambda i: (0, 0)
    return pl.pallas_call(
        _uv_body,
        grid=(t // PROJ_TM,),
        in_specs=[
            pl.BlockSpec((PROJ_TM, d), lambda i: (i, 0)),
            pl.BlockSpec((1, d), const),
            pl.BlockSpec((d, G_WIDTH), lambda i: (0, 0), pipeline_mode=SINGLE),
            pl.BlockSpec((d, G_WIDTH), lambda i: (0, 1), pipeline_mode=SINGLE),
            pl.BlockSpec((1, G_WIDTH), const),
            pl.BlockSpec((1, G_WIDTH), const),
            pl.BlockSpec((G_GROUPS, CHUNK, CHUNK), lambda i: (0, 0, 0)),
            pl.BlockSpec((CHUNK, G_GROUPS), const),
        ],
        out_specs=[
            pl.BlockSpec((PROJ_TM, d), lambda i: (i, 0)),
            pl.BlockSpec((PROJ_TM, G_WIDTH), lambda i: (i, 0)),
        ],
        out_shape=[
            jax.ShapeDtypeStruct((t, d), BF16),
            jax.ShapeDtypeStruct((t, G_WIDTH), BF16),
        ],
        compiler_params=_params("parallel"),
        name="uv_gmlp",
    )(h, g.reshape(1, d), w_uv, w_uv, ln_g.reshape(1, G_WIDTH), ln_b.reshape(1, G_WIDTH), w_s,
      jnp.transpose(b_s))


QKVM_WIDTH = SWA_WIDTH + 2 * SWA_KV_WIDTH + MEM_WIDTH
SWA_SCALE = 1.0 / math.sqrt(SWA_HEAD_DIM)
HALF_LANES = LANES // 2
KV_PLACED_WIDTH = SWA_KV_HEADS * LANES


def _place_heads(z):
    rows = z.shape[0]
    low_half = lax.broadcasted_iota(jnp.int32, (rows, LANES), 1) < HALF_LANES
    zero = jnp.zeros((rows, LANES), F32)
    low, high = [], []
    for pair in range(SWA_KV_HEADS // 2):
        zg = z[:, pair * LANES:(pair + 1) * LANES]
        swapped = pltpu.roll(zg, HALF_LANES, axis=1)
        low += [jnp.where(low_half, zg, zero), jnp.where(low_half, swapped, zero)]
        high += [jnp.where(low_half, zero, swapped), jnp.where(low_half, zero, zg)]
    return (jnp.concatenate(low, axis=1).astype(BF16), jnp.concatenate(high, axis=1).astype(BF16))


def _qkvm_body(n_ref, w_ref, q_ref, klo_ref, khi_ref, vlo_ref, vhi_ref, mq_ref):
    z = _dot(n_ref[...], w_ref[...])
    q_ref[...] = (z[:, :SWA_WIDTH] * SWA_SCALE).astype(BF16)
    klo_ref[...], khi_ref[...] = _place_heads(z[:, SWA_WIDTH:SWA_WIDTH + SWA_KV_WIDTH])
    vlo_ref[...], vhi_ref[...] = _place_heads(
        z[:, SWA_WIDTH + SWA_KV_WIDTH:SWA_WIDTH + 2 * SWA_KV_WIDTH])
    mq_ref[...] = z[:, SWA_WIDTH + 2 * SWA_KV_WIDTH:].astype(BF16)


def _qkvm(n, w):
    t, d = n.shape
    widths = (SWA_WIDTH,) + (KV_PLACED_WIDTH,) * 4 + (MEM_WIDTH,)
    return pl.pallas_call(
        _qkvm_body,
        grid=(t // PROJ_TM,),
        in_specs=[
            pl.BlockSpec((PROJ_TM, d), lambda i: (i, 0)),
            pl.BlockSpec((d, QKVM_WIDTH), lambda i: (0, 0), pipeline_mode=SINGLE),
        ],
        out_specs=[pl.BlockSpec((PROJ_TM, w_), lambda i: (i, 0)) for w_ in widths],
        out_shape=[jax.ShapeDtypeStruct((t, w_), BF16) for w_ in widths],
        compiler_params=_params("parallel"),
        name="qkvm",
    )(n, w)


def _gates_body(n_ref, w_ref, o_ref):
    o_ref[...] = jax.nn.sigmoid(_dot(n_ref[...], w_ref[...])).astype(BF16)


def _gates(n, w):
    t, d = n.shape
    return pl.pallas_call(
        _gates_body,
        grid=(N_BRANCH, t // GATES_TM),
        in_specs=[
            pl.BlockSpec((GATES_TM, d), lambda b, i: (i, 0)),
            pl.BlockSpec((d, d), lambda b, i: (0, b)),
        ],
        out_specs=pl.BlockSpec((None, GATES_TM, d), lambda b, i: (b, i, 0)),
        out_shape=jax.ShapeDtypeStruct((N_BRANCH, t, d), BF16),
        compiler_params=_params("parallel", "parallel"),
        name="gates",
    )(n, w)


def _mkv_body(m_ref, g_ref, w_ref, mk_ref, mv_ref):
    n = _rmsnorm_f32(m_ref[...], g_ref[...]).astype(BF16)
    z = _dot(n, w_ref[...]).astype(BF16)
    mk_ref[...] = z[:, :MEM_WIDTH]
    mv_ref[...] = z[:, MEM_WIDTH:]


def _mkv(mem, g, w):
    t, d = mem.shape
    return pl.pallas_call(
        _mkv_body,
        grid=(t // MEM_LEN,),
        in_specs=[
            pl.BlockSpec((MEM_LEN, d), lambda i: (i, 0)),
            pl.BlockSpec((1, d), lambda i: (0, 0)),
            pl.BlockSpec((d, 2 * MEM_WIDTH), lambda i: (0, 0), pipeline_mode=SINGLE),
        ],
        out_specs=[pl.BlockSpec((MEM_LEN, MEM_WIDTH), lambda i: (i, 0))] * 2,
        out_shape=[jax.ShapeDtypeStruct((t, MEM_WIDTH), BF16)] * 2,
        compiler_params=_params("parallel"),
        name="mem_kv",
    )(mem, g.reshape(1, d), w)


SWA_SLOPES = tuple(2.0 ** (-8.0 * (h + 1) / SWA_HEADS) for h in range(SWA_HEADS))


def _swa_body(sink_ref, q_ref, klo_ref, khi_ref, vlo_ref, vhi_ref,
              klo_p_ref, khi_p_ref, vlo_p_ref, vhi_p_ref, o_ref):
    w = WINDOW
    t = pl.program_id(1)
    i2 = lax.broadcasted_iota(jnp.int32, (2 * w, w), 0) & (w - 1)
    j2 = lax.broadcasted_iota(jnp.int32, (2 * w, w), 1)
    from_prev = j2 > i2
    dist = ((i2 - j2) & (w - 1)).astype(F32)
    no_prev = j2 > jnp.maximum(i2, jnp.where(t == 0, -1, w))
    low_half = j2 < HALF_LANES
    zero = jnp.zeros((2 * w, w), F32)
    ones_cols = (jnp.where(low_half, 1.0, 0.0).astype(BF16), jnp.where(low_half, 0.0, 1.0).astype(BF16))

    def column(top, bottom):
        return jnp.concatenate([jnp.full((w, 1), top, F32), jnp.full((w, 1), bottom, F32)], axis=0)

    for blk in range(SWA_TQ // w):
        rs = slice(blk * w, (blk + 1) * w)

        def band(cur_ref, prev_ref, cols):
            prev = prev_ref[:, cols] if blk == 0 else cur_ref[(blk - 1) * w:blk * w, cols]
            return jnp.concatenate([prev, cur_ref[rs, cols]], axis=0)

        for kvh in range(SWA_KV_HEADS):
            cols = slice(kvh * LANES, (kvh + 1) * LANES)
            q2 = jnp.concatenate([q_ref[rs, (2 * kvh) * LANES:(2 * kvh + 1) * LANES],
                                  q_ref[rs, (2 * kvh + 1) * LANES:(2 * kvh + 2) * LANES]], axis=0)
            keys = jnp.concatenate([band(klo_ref, klo_p_ref, cols), band(khi_ref, khi_p_ref, cols)], axis=0)
            s = _dot_nt(q2, keys)
            acc = None
            sink_terms = []
            for par, (v_ref, v_p_ref) in enumerate(((vlo_ref, vlo_p_ref), (vhi_ref, vhi_p_ref))):
                heads = (kvh * SWA_REP + par, kvh * SWA_REP + 2 + par)
                base = par * 2 * w
                sc = jnp.where(from_prev, s[:, base:base + w], s[:, base + w:base + 2 * w])
                sc = sc - column(SWA_SLOPES[heads[0]], SWA_SLOPES[heads[1]]) * dist
                if blk == 0:
                    sc = jnp.where(no_prev, NEG, sc)
                sink = column(sink_ref[heads[0]], sink_ref[heads[1]])
                m = jnp.maximum(jnp.max(sc, axis=-1, keepdims=True), sink)
                p = jnp.exp(sc - m)
                pcat = jnp.concatenate([jnp.where(from_prev, p, zero).astype(BF16),
                                        jnp.where(from_prev, zero, p).astype(BF16)], axis=1)
                vext = jnp.concatenate([band(v_ref, v_p_ref, cols), ones_cols[par]], axis=1)
                part = _dot(pcat, vext)
                acc = part if acc is None else acc + part
                sink_terms.append(jnp.exp(sink - m))
            denom = acc[:, w:] + jnp.where(low_half, sink_terms[0], sink_terms[1])
            out = (acc[:, :w] / denom).astype(BF16)
            o_ref[rs, (2 * kvh) * LANES:(2 * kvh + 1) * LANES] = out[:w]
            o_ref[rs, (2 * kvh + 1) * LANES:(2 * kvh + 2) * LANES] = out[w:]


def _swa(q, k_lo, k_hi, v_lo, v_hi, sinks, batch, seq):
    t = q.shape[0]
    steps = seq // SWA_TQ
    blocks_per_step = SWA_TQ // WINDOW
    blocks_per_seq = seq // WINDOW

    def cur(b, s):
        return (b * steps + s, 0)

    def prev(b, s):
        return (b * blocks_per_seq + jnp.maximum(s * blocks_per_step - 1, 0), 0)

    kv_cur = pl.BlockSpec((SWA_TQ, KV_PLACED_WIDTH), cur)
    kv_prev = pl.BlockSpec((WINDOW, KV_PLACED_WIDTH), prev)
    return pl.pallas_call(
        _swa_body,
        grid=(batch, steps),
        in_specs=[pl.BlockSpec(memory_space=pltpu.SMEM), pl.BlockSpec((SWA_TQ, SWA_WIDTH), cur)]
        + [kv_cur] * 4 + [kv_prev] * 4,
        out_specs=pl.BlockSpec((SWA_TQ, SWA_WIDTH), cur),
        out_shape=jax.ShapeDtypeStruct((t, SWA_WIDTH), BF16),
        compiler_params=_params("parallel", "parallel"),
        name="swa",
    )(sinks, q, k_lo, k_hi, v_lo, v_hi, k_lo, k_hi, v_lo, v_hi)


MEM_SCALE = 1.0 / math.sqrt(MEM_HEAD_DIM)


def _memattn_body(q_ref, mk_ref, mv_ref, o_ref):
    for h in range(MEM_HEADS):
        hs = slice(h * MEM_HEAD_DIM, (h + 1) * MEM_HEAD_DIM)
        s = _dot_nt(q_ref[:, hs], mk_ref[:, hs]) * MEM_SCALE
        m = jnp.max(s, axis=-1, keepdims=True)
        p = jnp.exp(s - m)
        probs = (p / jnp.sum(p, axis=-1, keepdims=True)).astype(BF16)
        o_ref[:, hs] = _dot(probs, mv_ref[:, hs]).astype(BF16)


def _memattn(mq, mk, mv, batch, seq):
    t = mq.shape[0]
    steps = seq // MEM_TQ
    return pl.pallas_call(
        _memattn_body,
        grid=(batch, steps),
        in_specs=[
            pl.BlockSpec((MEM_TQ, MEM_WIDTH), lambda b, s: (b * steps + s, 0)),
            pl.BlockSpec((MEM_LEN, MEM_WIDTH), lambda b, s: (b, 0)),
            pl.BlockSpec((MEM_LEN, MEM_WIDTH), lambda b, s: (b, 0)),
        ],
        out_specs=pl.BlockSpec((MEM_TQ, MEM_WIDTH), lambda b, s: (b * steps + s, 0)),
        out_shape=jax.ShapeDtypeStruct((t, MEM_WIDTH), BF16),
        compiler_params=_params("parallel", "parallel"),
        name="mem_attn",
    )(mq, mk, mv)


def _merge_body(h_ref, gt_ref, oa_ref, ob_ref, oc_ref, wbr_ref, wo_ref, o_ref):
    y = gt_ref[0].astype(F32) * _dot(oa_ref[...], wbr_ref[0])
    y += gt_ref[1].astype(F32) * _dot(ob_ref[...], wbr_ref[1])
    y += gt_ref[2].astype(F32) * _dot(oc_ref[...], wbr_ref[2])
    o_ref[...] = h_ref[...] + _dot(y.astype(BF16), wo_ref[...])


def _merge(h, gates, o_a, o_b, o_c, w_branch, w_out):
    t, d = h.shape
    row = lambda i: (i, 0)
    return pl.pallas_call(
        _merge_body,
        grid=(t // MERGE_TM,),
        in_specs=[
            pl.BlockSpec((MERGE_TM, d), row),
            pl.BlockSpec((N_BRANCH, MERGE_TM, d), lambda i: (0, i, 0)),
            pl.BlockSpec((MERGE_TM, BRANCH_WIDTH), row),
            pl.BlockSpec((MERGE_TM, BRANCH_WIDTH), row),
            pl.BlockSpec((MERGE_TM, BRANCH_WIDTH), row),
            pl.BlockSpec((N_BRANCH, BRANCH_WIDTH, d), lambda i: (0, 0, 0), pipeline_mode=SINGLE),
            pl.BlockSpec((d, d), lambda i: (0, 0), pipeline_mode=SINGLE),
        ],
        out_specs=pl.BlockSpec((MERGE_TM, d), row),
        out_shape=jax.ShapeDtypeStruct((t, d), F32),
        compiler_params=_params("parallel"),
        name="merge",
    )(h, gates, o_a, o_b, o_c, w_branch, w_out)


UV_END = 2 * G_WIDTH
QKVM_END = UV_END + QKVM_WIDTH


def kernel(x, mem, g_ffn1, w_ffn1_in, w_ffn1_out, g_mix, w_in, gmlp_ln_g, gmlp_ln_b, w_s, b_s, swa_sinks, g_mem, w_mem_kv, w_branch, w_out, g_ffn2, w_ffn2_in, w_ffn2_out, g_final):
    batch, seq, d = x.shape
    depth = w_in.shape[0]
    in_width = w_in.shape[2]
    xt = x.reshape(batch * seq, d)
    memt = mem.reshape(batch * MEM_LEN, d)
    w_branch_rows = w_branch.reshape(depth, N_BRANCH * BRANCH_WIDTH, d)
    for l in range(depth):
        h = _ffn(xt, g_ffn1[l], w_ffn1_in, w_ffn1_out, l, g_final, False)
        w_uv = _cast_cols(w_in, l, 0, UV_END, CAST_ROWS, UV_END)
        n, o_a = _uv(h, g_mix[l], w_uv, gmlp_ln_g[l], gmlp_ln_b[l], w_s[l], b_s[l])
        w_qkvm = _cast_cols(w_in, l, UV_END, QKVM_WIDTH, d, CAST_COLS)
        q, k_lo, k_hi, v_lo, v_hi, mq = _qkvm(n, w_qkvm)
        w_gate = _cast_cols(w_in, l, QKVM_END, in_width - QKVM_END, d, CAST_COLS)
        gates = _gates(n, w_gate)
        w_mkv = _cast_cols(w_mem_kv, l, 0, 2 * MEM_WIDTH, CAST_ROWS, 2 * MEM_WIDTH)
        mk, mv = _mkv(memt, g_mem[l], w_mkv)
        o_b = _swa(q, k_lo, k_hi, v_lo, v_hi, swa_sinks[l], batch, seq)
        o_c = _memattn(mq, mk, mv, batch, seq)
        w_br = _cast_cols(w_branch_rows, l, 0, d, CAST_ROWS, d).reshape(N_BRANCH, BRANCH_WIDTH, d)
        w_o = _cast_cols(w_out, l, 0, d, CAST_ROWS, d)
        h = _merge(h, gates, o_a, o_b, o_c, w_br, w_o)
        xt = _ffn(h, g_ffn2[l], w_ffn2_in, w_ffn2_out, l, g_final, l == depth - 1)
    return xt.reshape(batch, seq, d)
```

```python
import functools
import math

import jax
import jax.numpy as jnp
from jax import lax
from jax.experimental import pallas as pl
from jax.experimental.pallas import tpu as pltpu

D_MODEL = 2048
MEM_LEN = 256
CHUNK = 128
WINDOW = 128
G_GROUPS = 4
G_WIDTH = 1024
G_GROUP_DIM = G_WIDTH // G_GROUPS
SWA_HEADS = 16
SWA_KV_HEADS = 4
SWA_HEAD_DIM = 64
SWA_REP = SWA_HEADS // SWA_KV_HEADS
SWA_WIDTH = SWA_HEADS * SWA_HEAD_DIM
SWA_KV_WIDTH = SWA_KV_HEADS * SWA_HEAD_DIM
MEM_HEADS = 4
MEM_HEAD_DIM = 256
MEM_WIDTH = MEM_HEADS * MEM_HEAD_DIM
N_BRANCH = 3
BRANCH_WIDTH = 1024
D_FF = 5504
EPS = 1e-6
NEG = -1e30

F32 = jnp.float32
BF16 = jnp.bfloat16

LANES = 128
VMEM_LIMIT_BYTES = 60 * 1024 * 1024

FFN_TM = 1024
FFN_TF = 512
D_FF_PAD = -(-D_FF // FFN_TF) * FFN_TF
PROJ_TM = 512
GATES_TM = 1024
MERGE_TM = 256
SWA_TQ = 512
MEM_TQ = 1024
CAST_ROWS = 512
CAST_COLS = 512

SINGLE = pl.Buffered(1)


def _params(*semantics):
    return pltpu.CompilerParams(dimension_semantics=semantics, vmem_limit_bytes=VMEM_LIMIT_BYTES)


def _rmsnorm_f32(xf, g):
    return xf * lax.rsqrt(jnp.mean(xf * xf, axis=-1, keepdims=True) + EPS) * g


def _dot(a, b):
    return jnp.dot(a, b, preferred_element_type=F32)


def _dot_nt(a, b):
    return lax.dot_general(a, b, (((1,), (1,)), ((), ())), preferred_element_type=F32)


def _cast_body(w_ref, o_ref):
    o_ref[...] = w_ref[...].astype(BF16)


def _cast_cols(w, l, col0, ncols, rows_blk, cols_blk):
    _, rows, _ = w.shape
    assert rows % rows_blk == 0 and col0 % cols_blk == 0 and ncols % cols_blk == 0
    off = col0 // cols_blk
    return pl.pallas_call(
        _cast_body,
        grid=(rows // rows_blk, ncols // cols_blk),
        in_specs=[pl.BlockSpec((None, rows_blk, cols_blk), lambda i, j: (l, i, j + off))],
        out_specs=pl.BlockSpec((rows_blk, cols_blk), lambda i, j: (i, j)),
        out_shape=jax.ShapeDtypeStruct((rows, ncols), BF16),
        compiler_params=_params("parallel", "parallel"),
        name="cast_cols",
    )(w)


def _cast_ffn_in_body(w_ref, o_ref):
    o_ref[:, :D_FF] = w_ref[...].astype(BF16)
    o_ref[:, D_FF:] = jnp.zeros((o_ref.shape[0], D_FF_PAD - D_FF), BF16)


def _cast_ffn_in(w, l):
    _, d, _ = w.shape
    rows_blk = CAST_ROWS // 2
    return pl.pallas_call(
        _cast_ffn_in_body,
        grid=(2, d // rows_blk),
        in_specs=[pl.BlockSpec((None, rows_blk, D_FF), lambda s, i: (l, i, s))],
        out_specs=pl.BlockSpec((None, rows_blk, D_FF_PAD), lambda s, i: (s, i, 0)),
        out_shape=jax.ShapeDtypeStruct((2, d, D_FF_PAD), BF16),
        compiler_params=_params("parallel", "parallel"),
        name="cast_ffn_in",
    )(w)


FFN_OUT_CAST_ROWS = D_FF // 8


def _cast_ffn_out_body(w_ref, o_ref):
    is_param = pl.program_id(0) < D_FF // FFN_OUT_CAST_ROWS

    @pl.when(is_param)
    def _():
        o_ref[...] = w_ref[...].astype(BF16)

    @pl.when(jnp.logical_not(is_param))
    def _():
        o_ref[...] = jnp.zeros_like(o_ref)


def _cast_ffn_out(w, l):
    _, _, d = w.shape
    n_src = D_FF // FFN_OUT_CAST_ROWS
    return pl.pallas_call(
        _cast_ffn_out_body,
        grid=(pl.cdiv(D_FF_PAD, FFN_OUT_CAST_ROWS),),
        in_specs=[pl.BlockSpec((None, FFN_OUT_CAST_ROWS, d), lambda i: (l, jnp.minimum(i, n_src - 1), 0))],
        out_specs=pl.BlockSpec((FFN_OUT_CAST_ROWS, d), lambda i: (i, 0)),
        out_shape=jax.ShapeDtypeStruct((D_FF_PAD, d), BF16),
        compiler_params=_params("parallel"),
        name="cast_ffn_out",
    )(w)


def _ffn_body(x_ref, g_ref, wa_ref, wb_ref, wo_ref, gf_ref, o_ref, n_ref, *, apply_final_norm):
    j = pl.program_id(1)

    @pl.when(j == 0)
    def _():
        x = x_ref[...]
        n_ref[...] = _rmsnorm_f32(x, g_ref[...]).astype(BF16)
        o_ref[...] = x

    n = n_ref[...]
    a = _dot(n, wa_ref[...])
    b = _dot(n, wb_ref[...])
    hid = (0.5 * (a * jax.nn.sigmoid(a) * b)).astype(BF16)
    for c in range(o_ref.shape[1] // FFN_TF):
        cs = slice(c * FFN_TF, (c + 1) * FFN_TF)
        o_ref[:, cs] += _dot(hid, wo_ref[:, cs])

    if apply_final_norm:
        @pl.when(j == pl.num_programs(1) - 1)
        def _():
            o_ref[...] = _rmsnorm_f32(o_ref[...], gf_ref[...])


def _ffn(x, g, w_in, w_out, l, g_final, apply_final_norm):
    t, d = x.shape
    wab = _cast_ffn_in(w_in, l)
    wo = _cast_ffn_out(w_out, l)
    grid = (t // FFN_TM, D_FF_PAD // FFN_TF)
    return pl.pallas_call(
        functools.partial(_ffn_body, apply_final_norm=apply_final_norm),
        grid=grid,
        in_specs=[
            pl.BlockSpec((FFN_TM, d), lambda i, j: (i, 0)),
            pl.BlockSpec((1, d), lambda i, j: (0, 0)),
            pl.BlockSpec((None, d, FFN_TF), lambda i, j: (0, 0, j)),
            pl.BlockSpec((None, d, FFN_TF), lambda i, j: (1, 0, j)),
            pl.BlockSpec((FFN_TF, d), lambda i, j: (j, 0)),
            pl.BlockSpec((1, d), lambda i, j: (0, 0)),
        ],
        out_specs=pl.BlockSpec((FFN_TM, d), lambda i, j: (i, 0)),
        out_shape=jax.ShapeDtypeStruct((t, d), F32),
        scratch_shapes=[pltpu.VMEM((FFN_TM, d), BF16)],
        compiler_params=_params("parallel", "arbitrary"),
        name="ffn",
    )(x, g.reshape(1, d), wab, wab, wo, g_final.reshape(1, d))


def _gelu(x):
    return 0.5 * x * (1.0 + lax.erf(x * math.sqrt(0.5)))


def _uv_body(h_ref, g_ref, wu_ref, wv_ref, lng_ref, lnb_ref, ws_ref, bst_ref, n_ref, oa_ref):
    n = _rmsnorm_f32(h_ref[...], g_ref[...]).astype(BF16)
    n_ref[...] = n
    u = _gelu(_dot(n, wu_ref[...]))
    v = _gelu(_dot(n, wv_ref[...]))
    mu = jnp.mean(v, axis=-1, keepdims=True)
    vc = v - mu
    var = jnp.mean(vc * vc, axis=-1, keepdims=True)
    vn = (vc * lax.rsqrt(var + EPS) * lng_ref[...] + lnb_ref[...]).astype(BF16)

    row = lax.broadcasted_iota(jnp.int32, (CHUNK, CHUNK), 0)
    col = lax.broadcasted_iota(jnp.int32, (CHUNK, CHUNK), 1)
    causal = row >= col
    for g in range(G_GROUPS):
        ws = jnp.where(causal, ws_ref[g], 0.0).astype(BF16)
        bias = bst_ref[:, g:g + 1]
        cs = slice(g * G_GROUP_DIM, (g + 1) * G_GROUP_DIM)
        for c in range(PROJ_TM // CHUNK):
            rs = slice(c * CHUNK, (c + 1) * CHUNK)
            mixed = _dot(ws, vn[rs, cs]) + bias
            oa_ref[rs, cs] = (u[rs, cs] * mixed).astype(BF16)


def _uv(h, g, w_uv, ln_g, ln_b, w_s, b_s):
    t, d = h.shape
    const = lambda i: (0, 0)
    return pl.pallas_call(
        _uv_body,
        grid=(t // PROJ_TM,),
        in_specs=[
            pl.BlockSpec((PROJ_TM, d), lambda i: (i, 0)),
            pl.BlockSpec((1, d), const),
            pl.BlockSpec((d, G_WIDTH), lambda i: (0, 0), pipeline_mode=SINGLE),
            pl.BlockSpec((d, G_WIDTH), lambda i: (0, 1), pipeline_mode=SINGLE),
            pl.BlockSpec((1, G_WIDTH), const),
            pl.BlockSpec((1, G_WIDTH), const),
            pl.BlockSpec((G_GROUPS, CHUNK, CHUNK), lambda i: (0, 0, 0)),
            pl.BlockSpec((CHUNK, G_GROUPS), const),
        ],
        out_specs=[
            pl.BlockSpec((PROJ_TM, d), lambda i: (i, 0)),
            pl.BlockSpec((PROJ_TM, G_WIDTH), lambda i: (i, 0)),
        ],
        out_shape=[
            jax.ShapeDtypeStruct((t, d), BF16),
            jax.ShapeDtypeStruct((t, G_WIDTH), BF16),
        ],
        compiler_params=_params("parallel"),
        name="uv_gmlp",
    )(h, g.reshape(1, d), w_uv, w_uv, ln_g.reshape(1, G_WIDTH), ln_b.reshape(1, G_WIDTH), w_s,
      jnp.transpose(b_s))


QKVM_WIDTH = SWA_WIDTH + 2 * SWA_KV_WIDTH + MEM_WIDTH
SWA_SCALE = 1.0 / math.sqrt(SWA_HEAD_DIM)
HALF_LANES = LANES // 2
KV_PLACED_WIDTH = SWA_KV_HEADS * LANES


def _place_heads(z):
    rows = z.shape[0]
    low_half = lax.broadcasted_iota(jnp.int32, (rows, LANES), 1) < HALF_LANES
    zero = jnp.zeros((rows, LANES), F32)
    low, high = [], []
    for pair in range(SWA_KV_HEADS // 2):
        zg = z[:, pair * LANES:(pair + 1) * LANES]
        swapped = pltpu.roll(zg, HALF_LANES, axis=1)
        low += [jnp.where(low_half, zg, zero), jnp.where(low_half, swapped, zero)]
        high += [jnp.where(low_half, zero, swapped), jnp.where(low_half, zero, zg)]
    return (jnp.concatenate(low, axis=1).astype(BF16), jnp.concatenate(high, axis=1).astype(BF16))


def _qkvm_body(n_ref, w_ref, q_ref, klo_ref, khi_ref, vlo_ref, vhi_ref, mq_ref):
    z = _dot(n_ref[...], w_ref[...])
    q_ref[...] = (z[:, :SWA_WIDTH] * SWA_SCALE).astype(BF16)
    klo_ref[...], khi_ref[...] = _place_heads(z[:, SWA_WIDTH:SWA_WIDTH + SWA_KV_WIDTH])
    vlo_ref[...], vhi_ref[...] = _place_heads(
        z[:, SWA_WIDTH + SWA_KV_WIDTH:SWA_WIDTH + 2 * SWA_KV_WIDTH])
    mq_ref[...] = z[:, SWA_WIDTH + 2 * SWA_KV_WIDTH:].astype(BF16)


def _qkvm(n, w):
    t, d = n.shape
    widths = (SWA_WIDTH,) + (KV_PLACED_WIDTH,) * 4 + (MEM_WIDTH,)
    return pl.pallas_call(
        _qkvm_body,
        grid=(t // PROJ_TM,),
        in_specs=[
            pl.BlockSpec((PROJ_TM, d), lambda i: (i, 0)),
            pl.BlockSpec((d, QKVM_WIDTH), lambda i: (0, 0), pipeline_mode=SINGLE),
        ],
        out_specs=[pl.BlockSpec((PROJ_TM, w_), lambda i: (i, 0)) for w_ in widths],
        out_shape=[jax.ShapeDtypeStruct((t, w_), BF16) for w_ in widths],
        compiler_params=_params("parallel"),
        name="qkvm",
    )(n, w)


def _gates_body(n_ref, w_ref, o_ref):
    o_ref[...] = jax.nn.sigmoid(_dot(n_ref[...], w_ref[...])).astype(BF16)


def _gates(n, w):
    t, d = n.shape
    return pl.pallas_call(
        _gates_body,
        grid=(N_BRANCH, t // GATES_TM),
        in_specs=[
            pl.BlockSpec((GATES_TM, d), lambda b, i: (i, 0)),
            pl.BlockSpec((d, d), lambda b, i: (0, b)),
        ],
        out_specs=pl.BlockSpec((None, GATES_TM, d), lambda b, i: (b, i, 0)),
        out_shape=jax.ShapeDtypeStruct((N_BRANCH, t, d), BF16),
        compiler_params=_params("parallel", "parallel"),
        name="gates",
    )(n, w)


def _mkv_body(m_ref, g_ref, w_ref, mk_ref, mv_ref):
    n = _rmsnorm_f32(m_ref[...], g_ref[...]).astype(BF16)
    z = _dot(n, w_ref[...]).astype(BF16)
    mk_ref[...] = z[:, :MEM_WIDTH]
    mv_ref[...] = z[:, MEM_WIDTH:]


def _mkv(mem, g, w):
    t, d = mem.shape
    return pl.pallas_call(
        _mkv_body,
        grid=(t // MEM_LEN,),
        in_specs=[
            pl.BlockSpec((MEM_LEN, d), lambda i: (i, 0)),
            pl.BlockSpec((1, d), lambda i: (0, 0)),
            pl.BlockSpec((d, 2 * MEM_WIDTH), lambda i: (0, 0), pipeline_mode=SINGLE),
        ],
        out_specs=[pl.BlockSpec((MEM_LEN, MEM_WIDTH), lambda i: (i, 0))] * 2,
        out_shape=[jax.ShapeDtypeStruct((t, MEM_WIDTH), BF16)] * 2,
        compiler_params=_params("parallel"),
        name="mem_kv",
    )(mem, g.reshape(1, d), w)


SWA_SLOPES = tuple(2.0 ** (-8.0 * (h + 1) / SWA_HEADS) for h in range(SWA_HEADS))


def _swa_body(sink_ref, q_ref, klo_ref, khi_ref, vlo_ref, vhi_ref,
              klo_p_ref, khi_p_ref, vlo_p_ref, vhi_p_ref, o_ref):
    w = WINDOW
    t = pl.program_id(1)
    i2 = lax.broadcasted_iota(jnp.int32, (2 * w, w), 0) & (w - 1)
    j2 = lax.broadcasted_iota(jnp.int32, (2 * w, w), 1)
    from_prev = j2 > i2
    dist = ((i2 - j2) & (w - 1)).astype(F32)
    no_prev = j2 > jnp.maximum(i2, jnp.where(t == 0, -1, w))
    low_half = j2 < HALF_LANES
    zero = jnp.zeros((2 * w, w), F32)
    ones_cols = (jnp.where(low_half, 1.0, 0.0).astype(BF16), jnp.where(low_half, 0.0, 1.0).astype(BF16))

    def rows2(top, bottom):
        return jnp.concatenate([jnp.full((w, w), top, F32), jnp.full((w, w), bottom, F32)], axis=0)

    for blk in range(SWA_TQ // w):
        rs = slice(blk * w, (blk + 1) * w)

        def band(cur_ref, prev_ref, cols):
            prev = prev_ref[:, cols] if blk == 0 else cur_ref[(blk - 1) * w:blk * w, cols]
            return jnp.concatenate([prev, cur_ref[rs, cols]], axis=0)

        for kvh in range(SWA_KV_HEADS):
            cols = slice(kvh * LANES, (kvh + 1) * LANES)
            q2 = jnp.concatenate([q_ref[rs, (2 * kvh) * LANES:(2 * kvh + 1) * LANES],
                                  q_ref[rs, (2 * kvh + 1) * LANES:(2 * kvh + 2) * LANES]], axis=0)
            keys = jnp.concatenate([band(klo_ref, klo_p_ref, cols), band(khi_ref, khi_p_ref, cols)], axis=0)
            s = _dot_nt(q2, keys)
            acc = None
            sink_terms = []
            for par, (v_ref, v_p_ref) in enumerate(((vlo_ref, vlo_p_ref), (vhi_ref, vhi_p_ref))):
                heads = (kvh * SWA_REP + par, kvh * SWA_REP + 2 + par)
                base = par * 2 * w
                sc = jnp.where(from_prev, s[:, base:base + w], s[:, base + w:base + 2 * w])
                sc = sc - rows2(SWA_SLOPES[heads[0]], SWA_SLOPES[heads[1]]) * dist
                if blk == 0:
                    sc = jnp.where(no_prev, NEG, sc)
                sink = rows2(sink_ref[heads[0]], sink_ref[heads[1]])
                m = jnp.maximum(jnp.broadcast_to(jnp.max(sc, axis=-1, keepdims=True), (2 * w, w)), sink)
                p = jnp.exp(sc - m)
                pcat = jnp.concatenate([jnp.where(from_prev, p, zero).astype(BF16),
                                        jnp.where(from_prev, zero, p).astype(BF16)], axis=1)
                vext = jnp.concatenate([band(v_ref, v_p_ref, cols), ones_cols[par]], axis=1)
                part = _dot(pcat, vext)
                acc = part if acc is None else acc + part
                sink_terms.append(jnp.exp(sink - m))
            denom = acc[:, w:] + jnp.where(low_half, sink_terms[0], sink_terms[1])
            out = (acc[:, :w] / denom).astype(BF16)
            o_ref[rs, (2 * kvh) * LANES:(2 * kvh + 1) * LANES] = out[:w]
            o_ref[rs, (2 * kvh + 1) * LANES:(2 * kvh + 2) * LANES] = out[w:]


def _swa(q, k_lo, k_hi, v_lo, v_hi, sinks, batch, seq):
    t = q.shape[0]
    steps = seq // SWA_TQ
    blocks_per_step = SWA_TQ // WINDOW
    blocks_per_seq = seq // WINDOW

    def cur(b, s):
        return (b * steps + s, 0)

    def prev(b, s):
        return (b * blocks_per_seq + jnp.maximum(s * blocks_per_step - 1, 0), 0)

    kv_cur = pl.BlockSpec((SWA_TQ, KV_PLACED_WIDTH), cur)
    kv_prev = pl.BlockSpec((WINDOW, KV_PLACED_WIDTH), prev)
    return pl.pallas_call(
        _swa_body,
        grid=(batch, steps),
        in_specs=[pl.BlockSpec(memory_space=pltpu.SMEM), pl.BlockSpec((SWA_TQ, SWA_WIDTH), cur)]
        + [kv_cur] * 4 + [kv_prev] * 4,
        out_specs=pl.BlockSpec((SWA_TQ, SWA_WIDTH), cur),
        out_shape=jax.ShapeDtypeStruct((t, SWA_WIDTH), BF16),
        compiler_params=_params("parallel", "parallel"),
        name="swa",
    )(sinks, q, k_lo, k_hi, v_lo, v_hi, k_lo, k_hi, v_lo, v_hi)


MEM_SCALE = 1.0 / math.sqrt(MEM_HEAD_DIM)


def _memattn_body(q_ref, mk_ref, mv_ref, o_ref):
    for h in range(MEM_HEADS):
        hs = slice(h * MEM_HEAD_DIM, (h + 1) * MEM_HEAD_DIM)
        s = _dot_nt(q_ref[:, hs], mk_ref[:, hs]) * MEM_SCALE
        m = jnp.max(s, axis=-1, keepdims=True)
        p = jnp.exp(s - m)
        probs = (p / jnp.sum(p, axis=-1, keepdims=True)).astype(BF16)
        o_ref[:, hs] = _dot(probs, mv_ref[:, hs]).astype(BF16)


def _memattn(mq, mk, mv, batch, seq):
    t = mq.shape[0]
    steps = seq // MEM_TQ
    return pl.pallas_call(
        _memattn_body,
        grid=(batch, steps),
        in_specs=[
            pl.BlockSpec((MEM_TQ, MEM_WIDTH), lambda b, s: (b * steps + s, 0)),
            pl.BlockSpec((MEM_LEN, MEM_WIDTH), lambda b, s: (b, 0)),
            pl.BlockSpec((MEM_LEN, MEM_WIDTH), lambda b, s: (b, 0)),
        ],
        out_specs=pl.BlockSpec((MEM_TQ, MEM_WIDTH), lambda b, s: (b * steps + s, 0)),
        out_shape=jax.ShapeDtypeStruct((t, MEM_WIDTH), BF16),
        compiler_params=_params("parallel", "parallel"),
        name="mem_attn",
    )(mq, mk, mv)


def _merge_body(h_ref, gt_ref, oa_ref, ob_ref, oc_ref, wbr_ref, wo_ref, o_ref):
    y = gt_ref[0].astype(F32) * _dot(oa_ref[...], wbr_ref[0])
    y += gt_ref[1].astype(F32) * _dot(ob_ref[...], wbr_ref[1])
    y += gt_ref[2].astype(F32) * _dot(oc_ref[...], wbr_ref[2])
    o_ref[...] = h_ref[...] + _dot(y.astype(BF16), wo_ref[...])


def _merge(h, gates, o_a, o_b, o_c, w_branch, w_out):
    t, d = h.shape
    row = lambda i: (i, 0)
    return pl.pallas_call(
        _merge_body,
        grid=(t // MERGE_TM,),
        in_specs=[
            pl.BlockSpec((MERGE_TM, d), row),
            pl.BlockSpec((N_BRANCH, MERGE_TM, d), lambda i: (0, i, 0)),
            pl.BlockSpec((MERGE_TM, BRANCH_WIDTH), row),
            pl.BlockSpec((MERGE_TM, BRANCH_WIDTH), row),
            pl.BlockSpec((MERGE_TM, BRANCH_WIDTH), row),
            pl.BlockSpec((N_BRANCH, BRANCH_WIDTH, d), lambda i: (0, 0, 0), pipeline_mode=SINGLE),
            pl.BlockSpec((d, d), lambda i: (0, 0), pipeline_mode=SINGLE),
        ],
        out_specs=pl.BlockSpec((MERGE_TM, d), row),
        out_shape=jax.ShapeDtypeStruct((t, d), F32),
        compiler_params=_params("parallel"),
        name="merge",
    )(h, gates, o_a, o_b, o_c, w_branch, w_out)


UV_END = 2 * G_WIDTH
QKVM_END = UV_END + QKVM_WIDTH


def kernel(x, mem, g_ffn1, w_ffn1_in, w_ffn1_out, g_mix, w_in, gmlp_ln_g, gmlp_ln_b, w_s, b_s, swa_sinks, g_mem, w_mem_kv, w_branch, w_out, g_ffn2, w_ffn2_in, w_ffn2_out, g_final):
    batch, seq, d = x.shape
    depth = w_in.shape[0]
    in_width = w_in.shape[2]
    xt = x.reshape(batch * seq, d)
    memt = mem.reshape(batch * MEM_LEN, d)
    w_branch_rows = w_branch.reshape(depth, N_BRANCH * BRANCH_WIDTH, d)
    for l in range(depth):
        h = _ffn(xt, g_ffn1[l], w_ffn1_in, w_ffn1_out, l, g_final, False)
        w_uv = _cast_cols(w_in, l, 0, UV_END, CAST_ROWS, UV_END)
        n, o_a = _uv(h, g_mix[l], w_uv, gmlp_ln_g[l], gmlp_ln_b[l], w_s[l], b_s[l])
        w_qkvm = _cast_cols(w_in, l, UV_END, QKVM_WIDTH, d, CAST_COLS)
        q, k_lo, k_hi, v_lo, v_hi, mq = _qkvm(n, w_qkvm)
        w_gate = _cast_cols(w_in, l, QKVM_END, in_width - QKVM_END, d, CAST_COLS)
        gates = _gates(n, w_gate)
        w_mkv = _cast_cols(w_mem_kv, l, 0, 2 * MEM_WIDTH, CAST_ROWS, 2 * MEM_WIDTH)
        mk, mv = _mkv(memt, g_mem[l], w_mkv)
        o_b = _swa(q, k_lo, k_hi, v_lo, v_hi, swa_sinks[l], batch, seq)
        o_c = _memattn(mq, mk, mv, batch, seq)
        w_br = _cast_cols(w_branch_rows, l, 0, d, CAST_ROWS, d).reshape(N_BRANCH, BRANCH_WIDTH, d)
        w_o = _cast_cols(w_out, l, 0, d, CAST_ROWS, d)
        h = _merge(h, gates, o_a, o_b, o_c, w_br, w_o)
        xt = _ffn(h, g_ffn2[l], w_ffn2_in, w_ffn2_out, l, g_final, l == depth - 1)
    return xt.reshape(batch, seq, d)
```

```python
import functools
import math

import jax
import jax.numpy as jnp
from jax import lax
from jax.experimental import pallas as pl
from jax.experimental.pallas import tpu as pltpu

D_MODEL = 2048
MEM_LEN = 256
CHUNK = 128
WINDOW = 128
G_GROUPS = 4
G_WIDTH = 1024
G_GROUP_DIM = G_WIDTH // G_GROUPS
SWA_HEADS = 16
SWA_KV_HEADS = 4
SWA_HEAD_DIM = 64
SWA_REP = SWA_HEADS // SWA_KV_HEADS
SWA_WIDTH = SWA_HEADS * SWA_HEAD_DIM
SWA_KV_WIDTH = SWA_KV_HEADS * SWA_HEAD_DIM
MEM_HEADS = 4
MEM_HEAD_DIM = 256
MEM_WIDTH = MEM_HEADS * MEM_HEAD_DIM
N_BRANCH = 3
BRANCH_WIDTH = 1024
D_FF = 5504
EPS = 1e-6
NEG = -1e30

F32 = jnp.float32
BF16 = jnp.bfloat16

LANES = 128
VMEM_LIMIT_BYTES = 60 * 1024 * 1024

FFN_TM = 1024
FFN_TF = 512
D_FF_PAD = -(-D_FF // FFN_TF) * FFN_TF
PROJ_TM = 1024
GATES_TM = 1024
MERGE_TM = 256
SWA_TQ = 1024
MEM_TQ = 2048
CAST_ROWS = 512
CAST_COLS = 512

SINGLE = pl.Buffered(1)


def _params(*semantics):
    return pltpu.CompilerParams(dimension_semantics=semantics, vmem_limit_bytes=VMEM_LIMIT_BYTES)


def _rmsnorm_f32(xf, g):
    return xf * lax.rsqrt(jnp.mean(xf * xf, axis=-1, keepdims=True) + EPS) * g


def _dot(a, b):
    return jnp.dot(a, b, preferred_element_type=F32)


def _dot_nt(a, b):
    return lax.dot_general(a, b, (((1,), (1,)), ((), ())), preferred_element_type=F32)


def _cast_body(w_ref, o_ref):
    o_ref[...] = w_ref[...].astype(BF16)


def _cast_cols(w, l, col0, ncols, rows_blk, cols_blk):
    _, rows, _ = w.shape
    assert rows % rows_blk == 0 and col0 % cols_blk == 0 and ncols % cols_blk == 0
    off = col0 // cols_blk
    return pl.pallas_call(
        _cast_body,
        grid=(rows // rows_blk, ncols // cols_blk),
        in_specs=[pl.BlockSpec((None, rows_blk, cols_blk), lambda i, j: (l, i, j + off))],
        out_specs=pl.BlockSpec((rows_blk, cols_blk), lambda i, j: (i, j)),
        out_shape=jax.ShapeDtypeStruct((rows, ncols), BF16),
        compiler_params=_params("parallel", "parallel"),
        name="cast_cols",
    )(w)


def _cast_ffn_in_body(w_ref, o_ref):
    o_ref[:, :D_FF] = w_ref[...].astype(BF16)
    o_ref[:, D_FF:] = jnp.zeros((o_ref.shape[0], D_FF_PAD - D_FF), BF16)


def _cast_ffn_in(w, l):
    _, d, _ = w.shape
    rows_blk = CAST_ROWS // 2
    return pl.pallas_call(
        _cast_ffn_in_body,
        grid=(2, d // rows_blk),
        in_specs=[pl.BlockSpec((None, rows_blk, D_FF), lambda s, i: (l, i, s))],
        out_specs=pl.BlockSpec((None, rows_blk, D_FF_PAD), lambda s, i: (s, i, 0)),
        out_shape=jax.ShapeDtypeStruct((2, d, D_FF_PAD), BF16),
        compiler_params=_params("parallel", "parallel"),
        name="cast_ffn_in",
    )(w)


FFN_OUT_CAST_ROWS = D_FF // 8


def _cast_ffn_out_body(w_ref, o_ref):
    is_param = pl.program_id(0) < D_FF // FFN_OUT_CAST_ROWS

    @pl.when(is_param)
    def _():
        o_ref[...] = w_ref[...].astype(BF16)

    @pl.when(jnp.logical_not(is_param))
    def _():
        o_ref[...] = jnp.zeros_like(o_ref)


def _cast_ffn_out(w, l):
    _, _, d = w.shape
    n_src = D_FF // FFN_OUT_CAST_ROWS
    return pl.pallas_call(
        _cast_ffn_out_body,
        grid=(pl.cdiv(D_FF_PAD, FFN_OUT_CAST_ROWS),),
        in_specs=[pl.BlockSpec((None, FFN_OUT_CAST_ROWS, d), lambda i: (l, jnp.minimum(i, n_src - 1), 0))],
        out_specs=pl.BlockSpec((FFN_OUT_CAST_ROWS, d), lambda i: (i, 0)),
        out_shape=jax.ShapeDtypeStruct((D_FF_PAD, d), BF16),
        compiler_params=_params("parallel"),
        name="cast_ffn_out",
    )(w)


def _ffn_body(x_ref, g_ref, wa_ref, wb_ref, wo_ref, gf_ref, o_ref, n_ref, *, apply_final_norm):
    j = pl.program_id(1)

    @pl.when(j == 0)
    def _():
        x = x_ref[...]
        n_ref[...] = _rmsnorm_f32(x, g_ref[...]).astype(BF16)
        o_ref[...] = x

    n = n_ref[...]
    a = _dot(n, wa_ref[...])
    b = _dot(n, wb_ref[...])
    hid = (0.5 * (a * jax.nn.sigmoid(a) * b)).astype(BF16)
    for c in range(o_ref.shape[1] // FFN_TF):
        cs = slice(c * FFN_TF, (c + 1) * FFN_TF)
        o_ref[:, cs] += _dot(hid, wo_ref[:, cs])

    if apply_final_norm:
        @pl.when(j == pl.num_programs(1) - 1)
        def _():
            o_ref[...] = _rmsnorm_f32(o_ref[...], gf_ref[...])


def _ffn(x, g, w_in, w_out, l, g_final, apply_final_norm):
    t, d = x.shape
    wab = _cast_ffn_in(w_in, l)
    wo = _cast_ffn_out(w_out, l)
    grid = (t // FFN_TM, D_FF_PAD // FFN_TF)
    return pl.pallas_call(
        functools.partial(_ffn_body, apply_final_norm=apply_final_norm),
        grid=grid,
        in_specs=[
            pl.BlockSpec((FFN_TM, d), lambda i, j: (i, 0)),
            pl.BlockSpec((1, d), lambda i, j: (0, 0)),
            pl.BlockSpec((None, d, FFN_TF), lambda i, j: (0, 0, j)),
            pl.BlockSpec((None, d, FFN_TF), lambda i, j: (1, 0, j)),
            pl.BlockSpec((FFN_TF, d), lambda i, j: (j, 0)),
            pl.BlockSpec((1, d), lambda i, j: (0, 0)),
        ],
        out_specs=pl.BlockSpec((FFN_TM, d), lambda i, j: (i, 0)),
        out_shape=jax.ShapeDtypeStruct((t, d), F32),
        scratch_shapes=[pltpu.VMEM((FFN_TM, d), BF16)],
        compiler_params=_params("parallel", "arbitrary"),
        name="ffn",
    )(x, g.reshape(1, d), wab, wab, wo, g_final.reshape(1, d))


def _gelu(x):
    return 0.5 * x * (1.0 + lax.erf(x * math.sqrt(0.5)))


def _uv_body(h_ref, g_ref, wu_ref, wv_ref, lng_ref, lnb_ref, ws_ref, bst_ref, n_ref, oa_ref):
    n = _rmsnorm_f32(h_ref[...], g_ref[...]).astype(BF16)
    n_ref[...] = n
    u = _gelu(_dot(n, wu_ref[...]))
    v = _gelu(_dot(n, wv_ref[...]))
    mu = jnp.mean(v, axis=-1, keepdims=True)
    vc = v - mu
    var = jnp.mean(vc * vc, axis=-1, keepdims=True)
    vn = (vc * lax.rsqrt(var + EPS) * lng_ref[...] + lnb_ref[...]).astype(BF16)

    row = lax.broadcasted_iota(jnp.int32, (CHUNK, CHUNK), 0)
    col = lax.broadcasted_iota(jnp.int32, (CHUNK, CHUNK), 1)
    causal = row >= col
    for g in range(G_GROUPS):
        ws = jnp.where(causal, ws_ref[g], 0.0).astype(BF16)
        bias = bst_ref[:, g:g + 1]
        cs = slice(g * G_GROUP_DIM, (g + 1) * G_GROUP_DIM)
        for c in range(PROJ_TM // CHUNK):
            rs = slice(c * CHUNK, (c + 1) * CHUNK)
            mixed = _dot(ws, vn[rs, cs]) + bias
            oa_ref[rs, cs] = (u[rs, cs] * mixed).astype(BF16)


def _uv(h, g, w_uv, ln_g, ln_b, w_s, b_s):
    t, d = h.shape
    const = lambda i: (0, 0)
    return pl.pallas_call(
        _uv_body,
        grid=(t // PROJ_TM,),
        in_specs=[
            pl.BlockSpec((PROJ_TM, d), lambda i: (i, 0)),
            pl.BlockSpec((1, d), const),
            pl.BlockSpec((d, G_WIDTH), lambda i: (0, 0), pipeline_mode=SINGLE),
            pl.BlockSpec((d, G_WIDTH), lambda i: (0, 1), pipeline_mode=SINGLE),
            pl.BlockSpec((1, G_WIDTH), const),
            pl.BlockSpec((1, G_WIDTH), const),
            pl.BlockSpec((G_GROUPS, CHUNK, CHUNK), lambda i: (0, 0, 0)),
            pl.BlockSpec((CHUNK, G_GROUPS), const),
        ],
        out_specs=[
            pl.BlockSpec((PROJ_TM, d), lambda i: (i, 0)),
            pl.BlockSpec((PROJ_TM, G_WIDTH), lambda i: (i, 0)),
        ],
        out_shape=[
            jax.ShapeDtypeStruct((t, d), BF16),
            jax.ShapeDtypeStruct((t, G_WIDTH), BF16),
        ],
        compiler_params=_params("parallel"),
        name="uv_gmlp",
    )(h, g.reshape(1, d), w_uv, w_uv, ln_g.reshape(1, G_WIDTH), ln_b.reshape(1, G_WIDTH), w_s,
      jnp.transpose(b_s))


QKVM_WIDTH = SWA_WIDTH + 2 * SWA_KV_WIDTH + MEM_WIDTH
SWA_SCALE = 1.0 / math.sqrt(SWA_HEAD_DIM)
HALF_LANES = LANES // 2
KV_PLACED_WIDTH = SWA_KV_HEADS * LANES


def _place_heads(z):
    rows = z.shape[0]
    low_half = lax.broadcasted_iota(jnp.int32, (rows, LANES), 1) < HALF_LANES
    zero = jnp.zeros((rows, LANES), F32)
    low, high = [], []
    for pair in range(SWA_KV_HEADS // 2):
        zg = z[:, pair * LANES:(pair + 1) * LANES]
        swapped = pltpu.roll(zg, HALF_LANES, axis=1)
        low += [jnp.where(low_half, zg, zero), jnp.where(low_half, swapped, zero)]
        high += [jnp.where(low_half, zero, swapped), jnp.where(low_half, zero, zg)]
    return (jnp.concatenate(low, axis=1).astype(BF16), jnp.concatenate(high, axis=1).astype(BF16))


def _qkvm_body(n_ref, w_ref, q_ref, klo_ref, khi_ref, vlo_ref, vhi_ref, mq_ref):
    z = _dot(n_ref[...], w_ref[...])
    q_ref[...] = (z[:, :SWA_WIDTH] * SWA_SCALE).astype(BF16)
    klo_ref[...], khi_ref[...] = _place_heads(z[:, SWA_WIDTH:SWA_WIDTH + SWA_KV_WIDTH])
    vlo_ref[...], vhi_ref[...] = _place_heads(
        z[:, SWA_WIDTH + SWA_KV_WIDTH:SWA_WIDTH + 2 * SWA_KV_WIDTH])
    mq_ref[...] = z[:, SWA_WIDTH + 2 * SWA_KV_WIDTH:].astype(BF16)


def _qkvm(n, w):
    t, d = n.shape
    widths = (SWA_WIDTH,) + (KV_PLACED_WIDTH,) * 4 + (MEM_WIDTH,)
    return pl.pallas_call(
        _qkvm_body,
        grid=(t // PROJ_TM,),
        in_specs=[
            pl.BlockSpec((PROJ_TM, d), lambda i: (i, 0)),
            pl.BlockSpec((d, QKVM_WIDTH), lambda i: (0, 0), pipeline_mode=SINGLE),
        ],
        out_specs=[pl.BlockSpec((PROJ_TM, w_), lambda i: (i, 0)) for w_ in widths],
        out_shape=[jax.ShapeDtypeStruct((t, w_), BF16) for w_ in widths],
        compiler_params=_params("parallel"),
        name="qkvm",
    )(n, w)


GATES_TN = 512


def _gates_body(n_ref, w_ref, o_ref):
    n = n_ref[...]
    for c in range(o_ref.shape[1] // GATES_TN):
        cs = slice(c * GATES_TN, (c + 1) * GATES_TN)
        o_ref[:, cs] = jax.nn.sigmoid(_dot(n, w_ref[:, cs])).astype(BF16)


def _gates(n, w):
    t, d = n.shape
    return pl.pallas_call(
        _gates_body,
        grid=(N_BRANCH, t // GATES_TM),
        in_specs=[
            pl.BlockSpec((GATES_TM, d), lambda b, i: (i, 0)),
            pl.BlockSpec((d, d), lambda b, i: (0, b)),
        ],
        out_specs=pl.BlockSpec((None, GATES_TM, d), lambda b, i: (b, i, 0)),
        out_shape=jax.ShapeDtypeStruct((N_BRANCH, t, d), BF16),
        compiler_params=_params("parallel", "parallel"),
        name="gates",
    )(n, w)


def _mkv_body(m_ref, g_ref, w_ref, mk_ref, mv_ref):
    n = _rmsnorm_f32(m_ref[...], g_ref[...]).astype(BF16)
    z = _dot(n, w_ref[...]).astype(BF16)
    mk_ref[...] = z[:, :MEM_WIDTH]
    mv_ref[...] = z[:, MEM_WIDTH:]


def _mkv(mem, g, w):
    t, d = mem.shape
    return pl.pallas_call(
        _mkv_body,
        grid=(t // MEM_LEN,),
        in_specs=[
            pl.BlockSpec((MEM_LEN, d), lambda i: (i, 0)),
            pl.BlockSpec((1, d), lambda i: (0, 0)),
            pl.BlockSpec((d, 2 * MEM_WIDTH), lambda i: (0, 0), pipeline_mode=SINGLE),
        ],
        out_specs=[pl.BlockSpec((MEM_LEN, MEM_WIDTH), lambda i: (i, 0))] * 2,
        out_shape=[jax.ShapeDtypeStruct((t, MEM_WIDTH), BF16)] * 2,
        compiler_params=_params("parallel"),
        name="mem_kv",
    )(mem, g.reshape(1, d), w)


SWA_SLOPES = tuple(2.0 ** (-8.0 * (h + 1) / SWA_HEADS) for h in range(SWA_HEADS))


def _swa_body(sink_ref, q_ref, klo_ref, khi_ref, vlo_ref, vhi_ref,
              klo_p_ref, khi_p_ref, vlo_p_ref, vhi_p_ref, o_ref):
    w = WINDOW
    t = pl.program_id(1)
    i2 = lax.broadcasted_iota(jnp.int32, (2 * w, w), 0) & (w - 1)
    j2 = lax.broadcasted_iota(jnp.int32, (2 * w, w), 1)
    from_prev = j2 > i2
    dist = ((i2 - j2) & (w - 1)).astype(F32)
    no_prev = j2 > jnp.maximum(i2, jnp.where(t == 0, -1, w))
    low_half = j2 < HALF_LANES
    zero = jnp.zeros((2 * w, w), F32)
    ones_cols = (jnp.where(low_half, 1.0, 0.0).astype(BF16), jnp.where(low_half, 0.0, 1.0).astype(BF16))

    def rows2(top, bottom):
        return jnp.concatenate([jnp.full((w, w), top, F32), jnp.full((w, w), bottom, F32)], axis=0)

    for blk in range(SWA_TQ // w):
        rs = slice(blk * w, (blk + 1) * w)

        def band(cur_ref, prev_ref, cols):
            prev = prev_ref[:, cols] if blk == 0 else cur_ref[(blk - 1) * w:blk * w, cols]
            return jnp.concatenate([prev, cur_ref[rs, cols]], axis=0)

        for kvh in range(SWA_KV_HEADS):
            cols = slice(kvh * LANES, (kvh + 1) * LANES)
            q2 = jnp.concatenate([q_ref[rs, (2 * kvh) * LANES:(2 * kvh + 1) * LANES],
                                  q_ref[rs, (2 * kvh + 1) * LANES:(2 * kvh + 2) * LANES]], axis=0)
            keys = jnp.concatenate([band(klo_ref, klo_p_ref, cols), band(khi_ref, khi_p_ref, cols)], axis=0)
            s = _dot_nt(q2, keys)
            acc = None
            sink_terms = []
            for par, (v_ref, v_p_ref) in enumerate(((vlo_ref, vlo_p_ref), (vhi_ref, vhi_p_ref))):
                heads = (kvh * SWA_REP + par, kvh * SWA_REP + 2 + par)
                base = par * 2 * w
                sc = jnp.where(from_prev, s[:, base:base + w], s[:, base + w:base + 2 * w])
                sc = sc - rows2(SWA_SLOPES[heads[0]], SWA_SLOPES[heads[1]]) * dist
                if blk == 0:
                    sc = jnp.where(no_prev, NEG, sc)
                sink = rows2(sink_ref[heads[0]], sink_ref[heads[1]])
                m = jnp.maximum(jnp.broadcast_to(jnp.max(sc, axis=-1, keepdims=True), (2 * w, w)), sink)
                p = jnp.exp(sc - m)
                pcat = jnp.concatenate([jnp.where(from_prev, p, zero).astype(BF16),
                                        jnp.where(from_prev, zero, p).astype(BF16)], axis=1)
                vext = jnp.concatenate([band(v_ref, v_p_ref, cols), ones_cols[par]], axis=1)
                part = _dot(pcat, vext)
                acc = part if acc is None else acc + part
                sink_terms.append(jnp.exp(sink - m))
            denom = acc[:, w:] + jnp.where(low_half, sink_terms[0], sink_terms[1])
            out = (acc[:, :w] / denom).astype(BF16)
            o_ref[rs, (2 * kvh) * LANES:(2 * kvh + 1) * LANES] = out[:w]
            o_ref[rs, (2 * kvh + 1) * LANES:(2 * kvh + 2) * LANES] = out[w:]


def _swa(q, k_lo, k_hi, v_lo, v_hi, sinks, batch, seq):
    t = q.shape[0]
    steps = seq // SWA_TQ
    blocks_per_step = SWA_TQ // WINDOW
    blocks_per_seq = seq // WINDOW

    def cur(b, s):
        return (b * steps + s, 0)

    def prev(b, s):
        return (b * blocks_per_seq + jnp.maximum(s * blocks_per_step - 1, 0), 0)

    kv_cur = pl.BlockSpec((SWA_TQ, KV_PLACED_WIDTH), cur)
    kv_prev = pl.BlockSpec((WINDOW, KV_PLACED_WIDTH), prev)
    return pl.pallas_call(
        _swa_body,
        grid=(batch, steps),
        in_specs=[pl.BlockSpec(memory_space=pltpu.SMEM), pl.BlockSpec((SWA_TQ, SWA_WIDTH), cur)]
        + [kv_cur] * 4 + [kv_prev] * 4,
        out_specs=pl.BlockSpec((SWA_TQ, SWA_WIDTH), cur),
        out_shape=jax.ShapeDtypeStruct((t, SWA_WIDTH), BF16),
        compiler_params=_params("parallel", "parallel"),
        name="swa",
    )(sinks, q, k_lo, k_hi, v_lo, v_hi, k_lo, k_hi, v_lo, v_hi)


MEM_SCALE = 1.0 / math.sqrt(MEM_HEAD_DIM)


def _memattn_body(q_ref, mk_ref, mv_ref, o_ref):
    for h in range(MEM_HEADS):
        hs = slice(h * MEM_HEAD_DIM, (h + 1) * MEM_HEAD_DIM)
        s = _dot_nt(q_ref[:, hs], mk_ref[:, hs]) * MEM_SCALE
        m = jnp.max(s, axis=-1, keepdims=True)
        p = jnp.exp(s - m)
        probs = (p / jnp.sum(p, axis=-1, keepdims=True)).astype(BF16)
        o_ref[:, hs] = _dot(probs, mv_ref[:, hs]).astype(BF16)


def _memattn(mq, mk, mv, batch, seq):
    t = mq.shape[0]
    steps = seq // MEM_TQ
    return pl.pallas_call(
        _memattn_body,
        grid=(batch, steps),
        in_specs=[
            pl.BlockSpec((MEM_TQ, MEM_WIDTH), lambda b, s: (b * steps + s, 0)),
            pl.BlockSpec((MEM_LEN, MEM_WIDTH), lambda b, s: (b, 0)),
            pl.BlockSpec((MEM_LEN, MEM_WIDTH), lambda b, s: (b, 0)),
        ],
        out_specs=pl.BlockSpec((MEM_TQ, MEM_WIDTH), lambda b, s: (b * steps + s, 0)),
        out_shape=jax.ShapeDtypeStruct((t, MEM_WIDTH), BF16),
        compiler_params=_params("parallel", "parallel"),
        name="mem_attn",
    )(mq, mk, mv)


def _merge_body(h_ref, gt_ref, oa_ref, ob_ref, oc_ref, wbr_ref, wo_ref, o_ref):
    y = gt_ref[0].astype(F32) * _dot(oa_ref[...], wbr_ref[0])
    y += gt_ref[1].astype(F32) * _dot(ob_ref[...], wbr_ref[1])
    y += gt_ref[2].astype(F32) * _dot(oc_ref[...], wbr_ref[2])
    o_ref[...] = h_ref[...] + _dot(y.astype(BF16), wo_ref[...])


def _merge(h, gates, o_a, o_b, o_c, w_branch, w_out):
    t, d = h.shape
    row = lambda i: (i, 0)
    return pl.pallas_call(
        _merge_body,
        grid=(t // MERGE_TM,),
        in_specs=[
            pl.BlockSpec((MERGE_TM, d), row),
            pl.BlockSpec((N_BRANCH, MERGE_TM, d), lambda i: (0, i, 0)),
            pl.BlockSpec((MERGE_TM, BRANCH_WIDTH), row),
            pl.BlockSpec((MERGE_TM, BRANCH_WIDTH), row),
            pl.BlockSpec((MERGE_TM, BRANCH_WIDTH), row),
            pl.BlockSpec((N_BRANCH, BRANCH_WIDTH, d), lambda i: (0, 0, 0), pipeline_mode=SINGLE),
            pl.BlockSpec((d, d), lambda i: (0, 0), pipeline_mode=SINGLE),
        ],
        out_specs=pl.BlockSpec((MERGE_TM, d), row),
        out_shape=jax.ShapeDtypeStruct((t, d), F32),
        compiler_params=_params("parallel"),
        name="merge",
    )(h, gates, o_a, o_b, o_c, w_branch, w_out)


UV_END = 2 * G_WIDTH
QKVM_END = UV_END + QKVM_WIDTH


def kernel(x, mem, g_ffn1, w_ffn1_in, w_ffn1_out, g_mix, w_in, gmlp_ln_g, gmlp_ln_b, w_s, b_s, swa_sinks, g_mem, w_mem_kv, w_branch, w_out, g_ffn2, w_ffn2_in, w_ffn2_out, g_final):
    batch, seq, d = x.shape
    depth = w_in.shape[0]
    in_width = w_in.shape[2]
    xt = x.reshape(batch * seq, d)
    memt = mem.reshape(batch * MEM_LEN, d)
    w_branch_rows = w_branch.reshape(depth, N_BRANCH * BRANCH_WIDTH, d)
    for l in range(depth):
        h = _ffn(xt, g_ffn1[l], w_ffn1_in, w_ffn1_out, l, g_final, False)
        w_uv = _cast_cols(w_in, l, 0, UV_END, CAST_ROWS, UV_END)
        n, o_a = _uv(h, g_mix[l], w_uv, gmlp_ln_g[l], gmlp_ln_b[l], w_s[l], b_s[l])
        w_qkvm = _cast_cols(w_in, l, UV_END, QKVM_WIDTH, d, CAST_COLS)
        q, k_lo, k_hi, v_lo, v_hi, mq = _qkvm(n, w_qkvm)
        w_gate = _cast_cols(w_in, l, QKVM_END, in_width - QKVM_END, d, CAST_COLS)
        gates = _gates(n, w_gate)
        w_mkv = _cast_cols(w_mem_kv, l, 0, 2 * MEM_WIDTH, CAST_ROWS, 2 * MEM_WIDTH)
        mk, mv = _mkv(memt, g_mem[l], w_mkv)
        o_b = _swa(q, k_lo, k_hi, v_lo, v_hi, swa_sinks[l], batch, seq)
        o_c = _memattn(mq, mk, mv, batch, seq)
        w_br = _cast_cols(w_branch_rows, l, 0, d, CAST_ROWS, d).reshape(N_BRANCH, BRANCH_WIDTH, d)
        w_o = _cast_cols(w_out, l, 0, d, CAST_ROWS, d)
        h = _merge(h, gates, o_a, o_b, o_c, w_br, w_o)
        xt = _ffn(h, g_ffn2[l], w_ffn2_in, w_ffn2_out, l, g_final, l == depth - 1)
    return xt.reshape(batch, seq, d)
```

```python
import functools
import math
from typing import Callable, NamedTuple

import jax
import jax.numpy as jnp
from jax import lax
from jax.experimental import pallas as pl
from jax.experimental.pallas import tpu as pltpu

D_MODEL = 2048
MEM_LEN = 256
CHUNK = 128
WINDOW = 128
G_GROUPS = 4
G_WIDTH = 1024
G_GROUP_DIM = G_WIDTH // G_GROUPS
SWA_HEADS = 16
SWA_KV_HEADS = 4
SWA_HEAD_DIM = 64
SWA_REP = SWA_HEADS // SWA_KV_HEADS
SWA_WIDTH = SWA_HEADS * SWA_HEAD_DIM
SWA_KV_WIDTH = SWA_KV_HEADS * SWA_HEAD_DIM
MEM_HEADS = 4
MEM_HEAD_DIM = 256
MEM_WIDTH = MEM_HEADS * MEM_HEAD_DIM
N_BRANCH = 3
BRANCH_WIDTH = 1024
D_FF = 5504
EPS = 1e-6
NEG = -1e30

F32 = jnp.float32
BF16 = jnp.bfloat16

LANES = 128
VMEM_LIMIT_BYTES = 60 * 1024 * 1024

FFN_TM = 1024
FFN_TF = 512
D_FF_PAD = -(-D_FF // FFN_TF) * FFN_TF
UV_TM = 512
PROJ_TM = 1024
GATES_TM = 1024
MERGE_TM = 256
SWA_TQ = 1024
MEM_TQ = 2048
W_COLS = 512
CAST_ROWS = 128

SINGLE = pl.Buffered(1)


def _params(*semantics):
    return pltpu.CompilerParams(dimension_semantics=semantics, vmem_limit_bytes=VMEM_LIMIT_BYTES)


def _rmsnorm_f32(xf, g):
    return xf * lax.rsqrt(jnp.mean(xf * xf, axis=-1, keepdims=True) + EPS) * g


def _dot(a, b):
    return jnp.dot(a, b, preferred_element_type=F32)


def _dot_nt(a, b):
    return lax.dot_general(a, b, (((1,), (1,)), ((), ())), preferred_element_type=F32)


class _SideCast(NamedTuple):
    src: jax.Array
    in_spec: pl.BlockSpec
    out_spec: pl.BlockSpec
    out_shape: jax.ShapeDtypeStruct
    write: Callable
    n_blocks: int


def _side_rows(w, l, rows_blk, step_of):
    _, rows, cols = w.shape
    n_blocks = rows // rows_blk
    assert rows % rows_blk == 0

    def blk(*g):
        return jnp.minimum(step_of(*g), n_blocks - 1)

    def write(step, in_ref, out_ref):
        @pl.when(step < n_blocks)
        def _():
            out_ref[...] = in_ref[...].astype(BF16)

    return _SideCast(
        w,
        pl.BlockSpec((None, rows_blk, cols), lambda *g: (l, blk(*g), 0)),
        pl.BlockSpec((rows_blk, cols), lambda *g: (blk(*g), 0)),
        jax.ShapeDtypeStruct((rows, cols), BF16), write, n_blocks)


def _side_ffn_in(w, l, step_of):
    _, d, _ = w.shape
    per_half = d // CAST_ROWS
    n_blocks = 2 * per_half

    def blk(*g):
        return jnp.minimum(step_of(*g), n_blocks - 1)

    def write(step, in_ref, out_ref):
        @pl.when(step < n_blocks)
        def _():
            out_ref[:, :D_FF] = in_ref[...].astype(BF16)
            out_ref[:, D_FF:] = jnp.zeros((CAST_ROWS, D_FF_PAD - D_FF), BF16)

    return _SideCast(
        w,
        pl.BlockSpec((None, CAST_ROWS, D_FF), lambda *g: (l, blk(*g) % per_half, blk(*g) // per_half)),
        pl.BlockSpec((None, CAST_ROWS, D_FF_PAD), lambda *g: (blk(*g) // per_half, blk(*g) % per_half, 0)),
        jax.ShapeDtypeStruct((2, d, D_FF_PAD), BF16), write, n_blocks)


def _side_ffn_out(w, l, step_of):
    _, _, d = w.shape
    assert D_FF % CAST_ROWS == 0 and D_FF_PAD % CAST_ROWS == 0
    n_src = D_FF // CAST_ROWS
    n_blocks = D_FF_PAD // CAST_ROWS

    def write(step, in_ref, out_ref):
        @pl.when(step < n_src)
        def _():
            out_ref[...] = in_ref[...].astype(BF16)

        @pl.when((step >= n_src) & (step < n_blocks))
        def _():
            out_ref[...] = jnp.zeros_like(out_ref)

    return _SideCast(
        w,
        pl.BlockSpec((None, CAST_ROWS, d), lambda *g: (l, jnp.minimum(step_of(*g), n_src - 1), 0)),
        pl.BlockSpec((CAST_ROWS, d), lambda *g: (jnp.minimum(step_of(*g), n_blocks - 1), 0)),
        jax.ShapeDtypeStruct((D_FF_PAD, d), BF16), write, n_blocks)


def _hosted_call(body, *, grid, step_of, in_specs, out_specs, out_shape, args, sides, semantics, name,
                 scratch_shapes=()):
    n_in, n_out, k = len(in_specs), len(out_specs), len(sides)
    n_steps = math.prod(grid)
    assert all(s.n_blocks <= n_steps for s in sides)

    def hosted(*refs):
        ins, side_ins = refs[:n_in], refs[n_in:n_in + k]
        outs = refs[n_in + k:n_in + k + n_out]
        side_outs = refs[n_in + k + n_out:n_in + 2 * k + n_out]
        scratch = refs[n_in + 2 * k + n_out:]
        if k:
            step = step_of(*[pl.program_id(a) for a in range(len(grid))])
            for side, i_ref, o_ref in zip(sides, side_ins, side_outs):
                side.write(step, i_ref, o_ref)
        body(*ins, *outs, *scratch)

    res = pl.pallas_call(
        hosted,
        grid=grid,
        in_specs=list(in_specs) + [s.in_spec for s in sides],
        out_specs=list(out_specs) + [s.out_spec for s in sides],
        out_shape=list(out_shape) + [s.out_shape for s in sides],
        scratch_shapes=list(scratch_shapes),
        compiler_params=_params(*semantics),
        name=name,
    )(*args, *[s.src for s in sides])
    return res[:n_out], res[n_out:]


def _cast_alone(make_side):
    side = make_side(lambda i: i)
    _, (out,) = _hosted_call(lambda: None, grid=(side.n_blocks,), step_of=lambda i: i, in_specs=[],
                             out_specs=[], out_shape=[], args=[], sides=[side],
                             semantics=("arbitrary",), name="cast")
    return out


def _ffn_body(x_ref, g_ref, wa_ref, wb_ref, wo_ref, gf_ref, o_ref, n_ref, *, apply_final_norm):
    j = pl.program_id(1)

    @pl.when(j == 0)
    def _():
        x = x_ref[...]
        n_ref[...] = _rmsnorm_f32(x, g_ref[...]).astype(BF16)
        o_ref[...] = x

    n = n_ref[...]
    a = _dot(n, wa_ref[...])
    b = _dot(n, wb_ref[...])
    hid = (0.5 * (a * jax.nn.sigmoid(a) * b)).astype(BF16)
    for c in range(o_ref.shape[1] // FFN_TF):
        cs = slice(c * FFN_TF, (c + 1) * FFN_TF)
        o_ref[:, cs] += _dot(hid, wo_ref[:, cs])

    if apply_final_norm:
        @pl.when(j == pl.num_programs(1) - 1)
        def _():
            o_ref[...] = _rmsnorm_f32(o_ref[...], gf_ref[...])


def _ffn(x, g, wab, wo, g_final, apply_final_norm):
    t, d = x.shape
    grid = (t // FFN_TM, D_FF_PAD // FFN_TF)
    return pl.pallas_call(
        functools.partial(_ffn_body, apply_final_norm=apply_final_norm),
        grid=grid,
        in_specs=[
            pl.BlockSpec((FFN_TM, d), lambda i, j: (i, 0)),
            pl.BlockSpec((1, d), lambda i, j: (0, 0)),
            pl.BlockSpec((None, d, FFN_TF), lambda i, j: (0, 0, j)),
            pl.BlockSpec((None, d, FFN_TF), lambda i, j: (1, 0, j)),
            pl.BlockSpec((FFN_TF, d), lambda i, j: (j, 0)),
            pl.BlockSpec((1, d), lambda i, j: (0, 0)),
        ],
        out_specs=pl.BlockSpec((FFN_TM, d), lambda i, j: (i, 0)),
        out_shape=jax.ShapeDtypeStruct((t, d), F32),
        scratch_shapes=[pltpu.VMEM((FFN_TM, d), BF16)],
        compiler_params=_params("parallel", "arbitrary"),
        name="ffn",
    )(x, g.reshape(1, d), wab, wab, wo, g_final.reshape(1, d))


UV_END = 2 * G_WIDTH
QKVM_WIDTH = SWA_WIDTH + 2 * SWA_KV_WIDTH + MEM_WIDTH
QKVM_END = UV_END + QKVM_WIDTH


def _gelu(x):
    return 0.5 * x * (1.0 + lax.erf(x * math.sqrt(0.5)))


def _uv_body(h_ref, g_ref, wu_ref, wv_ref, lng_ref, lnb_ref, ws_ref, bst_ref, n_ref, oa_ref):
    n = _rmsnorm_f32(h_ref[...], g_ref[...]).astype(BF16)
    n_ref[...] = n
    u = _gelu(_dot(n, wu_ref[...]))
    v = _gelu(_dot(n, wv_ref[...]))
    mu = jnp.mean(v, axis=-1, keepdims=True)
    vc = v - mu
    var = jnp.mean(vc * vc, axis=-1, keepdims=True)
    vn = (vc * lax.rsqrt(var + EPS) * lng_ref[...] + lnb_ref[...]).astype(BF16)

    row = lax.broadcasted_iota(jnp.int32, (CHUNK, CHUNK), 0)
    col = lax.broadcasted_iota(jnp.int32, (CHUNK, CHUNK), 1)
    causal = row >= col
    for g in range(G_GROUPS):
        ws = jnp.where(causal, ws_ref[g], 0.0).astype(BF16)
        bias = bst_ref[:, g:g + 1]
        cs = slice(g * G_GROUP_DIM, (g + 1) * G_GROUP_DIM)
        for c in range(UV_TM // CHUNK):
            rs = slice(c * CHUNK, (c + 1) * CHUNK)
            mixed = _dot(ws, vn[rs, cs]) + bias
            oa_ref[rs, cs] = (u[rs, cs] * mixed).astype(BF16)


def _uv(h, g, w_in_b, ln_g, ln_b, w_s, b_s, make_sides):
    t, d = h.shape
    const = lambda i: (0, 0)
    (n, o_a), side_out = _hosted_call(
        _uv_body,
        grid=(t // UV_TM,),
        step_of=lambda i: i,
        in_specs=[
            pl.BlockSpec((UV_TM, d), lambda i: (i, 0)),
            pl.BlockSpec((1, d), const),
            pl.BlockSpec((d, G_WIDTH), lambda i: (0, 0), pipeline_mode=SINGLE),
            pl.BlockSpec((d, G_WIDTH), lambda i: (0, 1), pipeline_mode=SINGLE),
            pl.BlockSpec((1, G_WIDTH), const),
            pl.BlockSpec((1, G_WIDTH), const),
            pl.BlockSpec((G_GROUPS, CHUNK, CHUNK), lambda i: (0, 0, 0)),
            pl.BlockSpec((CHUNK, G_GROUPS), const),
        ],
        out_specs=[
            pl.BlockSpec((UV_TM, d), lambda i: (i, 0)),
            pl.BlockSpec((UV_TM, G_WIDTH), lambda i: (i, 0)),
        ],
        out_shape=[
            jax.ShapeDtypeStruct((t, d), BF16),
            jax.ShapeDtypeStruct((t, G_WIDTH), BF16),
        ],
        args=(h, g.reshape(1, d), w_in_b, w_in_b, ln_g.reshape(1, G_WIDTH), ln_b.reshape(1, G_WIDTH), w_s,
              jnp.transpose(b_s)),
        sides=[m(lambda i: i) for m in make_sides],
        semantics=("arbitrary",),
        name="uv_gmlp",
    )
    return n, o_a, side_out


SWA_SCALE = 1.0 / math.sqrt(SWA_HEAD_DIM)
HALF_LANES = LANES // 2
KV_PLACED_WIDTH = SWA_KV_HEADS * LANES
QKVM_BLOCKS = QKVM_WIDTH // W_COLS


def _place_heads(z):
    rows = z.shape[0]
    low_half = lax.broadcasted_iota(jnp.int32, (rows, LANES), 1) < HALF_LANES
    zero = jnp.zeros((rows, LANES), F32)
    low, high = [], []
    for pair in range(SWA_KV_HEADS // 2):
        zg = z[:, pair * LANES:(pair + 1) * LANES]
        swapped = pltpu.roll(zg, HALF_LANES, axis=1)
        low += [jnp.where(low_half, zg, zero), jnp.where(low_half, swapped, zero)]
        high += [jnp.where(low_half, zero, swapped), jnp.where(low_half, zero, zg)]
    return (jnp.concatenate(low, axis=1).astype(BF16), jnp.concatenate(high, axis=1).astype(BF16))


def _qkvm_body(n_ref, wq0_ref, wq1_ref, wkv_ref, wm0_ref, wm1_ref,
               q_ref, klo_ref, khi_ref, vlo_ref, vhi_ref, mq_ref):
    n = n_ref[...]
    q_ref[:, :W_COLS] = (_dot(n, wq0_ref[...]) * SWA_SCALE).astype(BF16)
    q_ref[:, W_COLS:] = (_dot(n, wq1_ref[...]) * SWA_SCALE).astype(BF16)
    kv = _dot(n, wkv_ref[...])
    klo_ref[...], khi_ref[...] = _place_heads(kv[:, :SWA_KV_WIDTH])
    vlo_ref[...], vhi_ref[...] = _place_heads(kv[:, SWA_KV_WIDTH:])
    mq_ref[:, :W_COLS] = _dot(n, wm0_ref[...]).astype(BF16)
    mq_ref[:, W_COLS:] = _dot(n, wm1_ref[...]).astype(BF16)


def _qkvm(n, w_in_b, make_sides):
    t, d = n.shape
    assert SWA_WIDTH == 2 * W_COLS and 2 * SWA_KV_WIDTH == W_COLS and MEM_WIDTH == 2 * W_COLS
    first = UV_END // W_COLS
    widths = (SWA_WIDTH,) + (KV_PLACED_WIDTH,) * 4 + (MEM_WIDTH,)
    w_specs = [pl.BlockSpec((d, W_COLS), functools.partial(lambda c, i: (0, c), first + c),
                            pipeline_mode=SINGLE) for c in range(QKVM_BLOCKS)]
    outs, side_out = _hosted_call(
        _qkvm_body,
        grid=(t // PROJ_TM,),
        step_of=lambda i: i,
        in_specs=[pl.BlockSpec((PROJ_TM, d), lambda i: (i, 0))] + w_specs,
        out_specs=[pl.BlockSpec((PROJ_TM, w_), lambda i: (i, 0)) for w_ in widths],
        out_shape=[jax.ShapeDtypeStruct((t, w_), BF16) for w_ in widths],
        args=(n,) + (w_in_b,) * QKVM_BLOCKS,
        sides=[m(lambda i: i) for m in make_sides],
        semantics=("arbitrary",),
        name="qkvm",
    )
    return outs, side_out


GATE_BLOCKS = D_MODEL // W_COLS


def _gates_body(n_ref, *refs):
    w_refs, o_ref = refs[:GATE_BLOCKS], refs[GATE_BLOCKS]
    n = n_ref[...]
    for c, w_ref in enumerate(w_refs):
        o_ref[:, c * W_COLS:(c + 1) * W_COLS] = jax.nn.sigmoid(_dot(n, w_ref[...])).astype(BF16)


def _gates(n, w_in_b, make_sides):
    t, d = n.shape
    tiles = t // GATES_TM
    first = QKVM_END // W_COLS
    step_of = lambda b, i: b * tiles + i
    w_specs = [pl.BlockSpec((d, W_COLS), functools.partial(lambda c, b, i: (0, first + b * GATE_BLOCKS + c), c))
               for c in range(GATE_BLOCKS)]
    (gates,), side_out = _hosted_call(
        _gates_body,
        grid=(N_BRANCH, tiles),
        step_of=step_of,
        in_specs=[pl.BlockSpec((GATES_TM, d), lambda b, i: (i, 0))] + w_specs,
        out_specs=[pl.BlockSpec((None, GATES_TM, d), lambda b, i: (b, i, 0))],
        out_shape=[jax.ShapeDtypeStruct((N_BRANCH, t, d), BF16)],
        args=(n,) + (w_in_b,) * GATE_BLOCKS,
        sides=[m(step_of) for m in make_sides],
        semantics=("arbitrary", "arbitrary"),
        name="gates",
    )
    return gates, side_out


def _mkv_body(m_ref, g_ref, w_ref, mk_ref, mv_ref):
    n = _rmsnorm_f32(m_ref[...], g_ref[...]).astype(BF16)
    z = _dot(n, w_ref[...]).astype(BF16)
    mk_ref[...] = z[:, :MEM_WIDTH]
    mv_ref[...] = z[:, MEM_WIDTH:]


def _mkv(mem, g, w):
    t, d = mem.shape
    return pl.pallas_call(
        _mkv_body,
        grid=(t // MEM_LEN,),
        in_specs=[
            pl.BlockSpec((MEM_LEN, d), lambda i: (i, 0)),
            pl.BlockSpec((1, d), lambda i: (0, 0)),
            pl.BlockSpec((d, 2 * MEM_WIDTH), lambda i: (0, 0), pipeline_mode=SINGLE),
        ],
        out_specs=[pl.BlockSpec((MEM_LEN, MEM_WIDTH), lambda i: (i, 0))] * 2,
        out_shape=[jax.ShapeDtypeStruct((t, MEM_WIDTH), BF16)] * 2,
        compiler_params=_params("parallel"),
        name="mem_kv",
    )(mem, g.reshape(1, d), w)


SWA_SLOPES = tuple(2.0 ** (-8.0 * (h + 1) / SWA_HEADS) for h in range(SWA_HEADS))


def _swa_body(sink_ref, q_ref, klo_ref, khi_ref, vlo_ref, vhi_ref,
              klo_p_ref, khi_p_ref, vlo_p_ref, vhi_p_ref, o_ref):
    w = WINDOW
    t = pl.program_id(1)
    i2 = lax.broadcasted_iota(jnp.int32, (2 * w, w), 0) & (w - 1)
    j2 = lax.broadcasted_iota(jnp.int32, (2 * w, w), 1)
    from_prev = j2 > i2
    dist = ((i2 - j2) & (w - 1)).astype(F32)
    no_prev = j2 > jnp.maximum(i2, jnp.where(t == 0, -1, w))
    low_half = j2 < HALF_LANES
    zero = jnp.zeros((2 * w, w), F32)
    ones_cols = (jnp.where(low_half, 1.0, 0.0).astype(BF16), jnp.where(low_half, 0.0, 1.0).astype(BF16))

    def rows2(top, bottom):
        return jnp.concatenate([jnp.full((w, w), top, F32), jnp.full((w, w), bottom, F32)], axis=0)

    for blk in range(SWA_TQ // w):
        rs = slice(blk * w, (blk + 1) * w)

        def band(cur_ref, prev_ref, cols):
            prev = prev_ref[:, cols] if blk == 0 else cur_ref[(blk - 1) * w:blk * w, cols]
            return jnp.concatenate([prev, cur_ref[rs, cols]], axis=0)

        for kvh in range(SWA_KV_HEADS):
            cols = slice(kvh * LANES, (kvh + 1) * LANES)
            q2 = jnp.concatenate([q_ref[rs, (2 * kvh) * LANES:(2 * kvh + 1) * LANES],
                                  q_ref[rs, (2 * kvh + 1) * LANES:(2 * kvh + 2) * LANES]], axis=0)
            keys = jnp.concatenate([band(klo_ref, klo_p_ref, cols), band(khi_ref, khi_p_ref, cols)], axis=0)
            s = _dot_nt(q2, keys)
            acc = None
            sink_terms = []
            for par, (v_ref, v_p_ref) in enumerate(((vlo_ref, vlo_p_ref), (vhi_ref, vhi_p_ref))):
                heads = (kvh * SWA_REP + par, kvh * SWA_REP + 2 + par)
                base = par * 2 * w
                sc = jnp.where(from_prev, s[:, base:base + w], s[:, base + w:base + 2 * w])
                sc = sc - rows2(SWA_SLOPES[heads[0]], SWA_SLOPES[heads[1]]) * dist
                if blk == 0:
                    sc = jnp.where(no_prev, NEG, sc)
                sink = rows2(sink_ref[heads[0]], sink_ref[heads[1]])
                m = jnp.maximum(jnp.broadcast_to(jnp.max(sc, axis=-1, keepdims=True), (2 * w, w)), sink)
                p = jnp.exp(sc - m)
                pcat = jnp.concatenate([jnp.where(from_prev, p, zero).astype(BF16),
                                        jnp.where(from_prev, zero, p).astype(BF16)], axis=1)
                vext = jnp.concatenate([band(v_ref, v_p_ref, cols), ones_cols[par]], axis=1)
                part = _dot(pcat, vext)
                acc = part if acc is None else acc + part
                sink_terms.append(jnp.exp(sink - m))
            denom = acc[:, w:] + jnp.where(low_half, sink_terms[0], sink_terms[1])
            out = (acc[:, :w] / denom).astype(BF16)
            o_ref[rs, (2 * kvh) * LANES:(2 * kvh + 1) * LANES] = out[:w]
            o_ref[rs, (2 * kvh + 1) * LANES:(2 * kvh + 2) * LANES] = out[w:]


def _swa(q, k_lo, k_hi, v_lo, v_hi, sinks, batch, seq):
    t = q.shape[0]
    steps = seq // SWA_TQ
    blocks_per_step = SWA_TQ // WINDOW
    blocks_per_seq = seq // WINDOW

    def cur(b, s):
        return (b * steps + s, 0)

    def prev(b, s):
        return (b * blocks_per_seq + jnp.maximum(s * blocks_per_step - 1, 0), 0)

    kv_cur = pl.BlockSpec((SWA_TQ, KV_PLACED_WIDTH), cur)
    kv_prev = pl.BlockSpec((WINDOW, KV_PLACED_WIDTH), prev)
    return pl.pallas_call(
        _swa_body,
        grid=(batch, steps),
        in_specs=[pl.BlockSpec(memory_space=pltpu.SMEM), pl.BlockSpec((SWA_TQ, SWA_WIDTH), cur)]
        + [kv_cur] * 4 + [kv_prev] * 4,
        out_specs=pl.BlockSpec((SWA_TQ, SWA_WIDTH), cur),
        out_shape=jax.ShapeDtypeStruct((t, SWA_WIDTH), BF16),
        compiler_params=_params("parallel", "parallel"),
        name="swa",
    )(sinks, q, k_lo, k_hi, v_lo, v_hi, k_lo, k_hi, v_lo, v_hi)


MEM_SCALE = 1.0 / math.sqrt(MEM_HEAD_DIM)


def _memattn_body(q_ref, mk_ref, mv_ref, o_ref):
    for h in range(MEM_HEADS):
        hs = slice(h * MEM_HEAD_DIM, (h + 1) * MEM_HEAD_DIM)
        s = _dot_nt(q_ref[:, hs], mk_ref[:, hs]) * MEM_SCALE
        m = jnp.max(s, axis=-1, keepdims=True)
        p = jnp.exp(s - m)
        probs = (p / jnp.sum(p, axis=-1, keepdims=True)).astype(BF16)
        o_ref[:, hs] = _dot(probs, mv_ref[:, hs]).astype(BF16)


def _memattn(mq, mk, mv, batch, seq):
    t = mq.shape[0]
    steps = seq // MEM_TQ
    return pl.pallas_call(
        _memattn_body,
        grid=(batch, steps),
        in_specs=[
            pl.BlockSpec((MEM_TQ, MEM_WIDTH), lambda b, s: (b * steps + s, 0)),
            pl.BlockSpec((MEM_LEN, MEM_WIDTH), lambda b, s: (b, 0)),
            pl.BlockSpec((MEM_LEN, MEM_WIDTH), lambda b, s: (b, 0)),
        ],
        out_specs=pl.BlockSpec((MEM_TQ, MEM_WIDTH), lambda b, s: (b * steps + s, 0)),
        out_shape=jax.ShapeDtypeStruct((t, MEM_WIDTH), BF16),
        compiler_params=_params("parallel", "parallel"),
        name="mem_attn",
    )(mq, mk, mv)


def _merge_body(h_ref, gt_ref, oa_ref, ob_ref, oc_ref, wbr_ref, wo_ref, o_ref):
    y = gt_ref[0].astype(F32) * _dot(oa_ref[...], wbr_ref[0])
    y += gt_ref[1].astype(F32) * _dot(ob_ref[...], wbr_ref[1])
    y += gt_ref[2].astype(F32) * _dot(oc_ref[...], wbr_ref[2])
    o_ref[...] = h_ref[...] + _dot(y.astype(BF16), wo_ref[...])


def _merge(h, gates, o_a, o_b, o_c, w_branch, w_out, make_sides):
    t, d = h.shape
    row = lambda i: (i, 0)
    (out,), side_out = _hosted_call(
        _merge_body,
        grid=(t // MERGE_TM,),
        step_of=lambda i: i,
        in_specs=[
            pl.BlockSpec((MERGE_TM, d), row),
            pl.BlockSpec((N_BRANCH, MERGE_TM, d), lambda i: (0, i, 0)),
            pl.BlockSpec((MERGE_TM, BRANCH_WIDTH), row),
            pl.BlockSpec((MERGE_TM, BRANCH_WIDTH), row),
            pl.BlockSpec((MERGE_TM, BRANCH_WIDTH), row),
            pl.BlockSpec((N_BRANCH, BRANCH_WIDTH, d), lambda i: (0, 0, 0), pipeline_mode=SINGLE),
            pl.BlockSpec((d, d), lambda i: (0, 0), pipeline_mode=SINGLE),
        ],
        out_specs=[pl.BlockSpec((MERGE_TM, d), row)],
        out_shape=[jax.ShapeDtypeStruct((t, d), F32)],
        args=(h, gates, o_a, o_b, o_c, w_branch, w_out),
        sides=[m(lambda i: i) for m in make_sides],
        semantics=("arbitrary",),
        name="merge",
    )
    return out, side_out


def kernel(x, mem, g_ffn1, w_ffn1_in, w_ffn1_out, g_mix, w_in, gmlp_ln_g, gmlp_ln_b, w_s, b_s, swa_sinks, g_mem, w_mem_kv, w_branch, w_out, g_ffn2, w_ffn2_in, w_ffn2_out, g_final):
    batch, seq, d = x.shape
    depth = w_in.shape[0]
    xt = x.reshape(batch * seq, d)
    memt = mem.reshape(batch * MEM_LEN, d)
    w_branch_rows = w_branch.reshape(depth, N_BRANCH * BRANCH_WIDTH, d)
    branch_cast_rows = N_BRANCH * BRANCH_WIDTH // (d // CAST_ROWS)

    def mixer_casts(l):
        on_uv = [functools.partial(_side_rows, w_in, l, CAST_ROWS // 2)]
        on_qkvm = [functools.partial(_side_rows, w_mem_kv, l, CAST_ROWS),
                   functools.partial(_side_rows, w_out, l, CAST_ROWS),
                   functools.partial(_side_rows, w_branch_rows, l, branch_cast_rows)]
        return on_uv, on_qkvm

    def ffn_casts(w_in_ffn, w_out_ffn, l):
        return [functools.partial(_side_ffn_in, w_in_ffn, l), functools.partial(_side_ffn_out, w_out_ffn, l)]

    wab1, wo1 = [_cast_alone(m) for m in ffn_casts(w_ffn1_in, w_ffn1_out, 0)]
    on_uv, on_qkvm = mixer_casts(0)
    w_in_b, w_mkv, w_o, w_br = [_cast_alone(m) for m in on_uv + on_qkvm]

    for l in range(depth):
        more = l + 1 < depth
        next_uv, next_qkvm = mixer_casts(l + 1) if more else ([], [])
        h = _ffn(xt, g_ffn1[l], wab1, wo1, g_final, False)
        n, o_a, uv_side = _uv(h, g_mix[l], w_in_b, gmlp_ln_g[l], gmlp_ln_b[l], w_s[l], b_s[l], next_uv)
        (q, k_lo, k_hi, v_lo, v_hi, mq), qkvm_side = _qkvm(n, w_in_b, next_qkvm)
        gates, (wab2, wo2) = _gates(n, w_in_b, ffn_casts(w_ffn2_in, w_ffn2_out, l))
        mk, mv = _mkv(memt, g_mem[l], w_mkv)
        o_b = _swa(q, k_lo, k_hi, v_lo, v_hi, swa_sinks[l], batch, seq)
        o_c = _memattn(mq, mk, mv, batch, seq)
        h, merge_side = _merge(h, gates, o_a, o_b, o_c, w_br.reshape(N_BRANCH, BRANCH_WIDTH, d), w_o,
                               ffn_casts(w_ffn1_in, w_ffn1_out, l + 1) if more else [])
        xt = _ffn(h, g_ffn2[l], wab2, wo2, g_final, not more)
        if more:
            (w_in_b,), (w_mkv, w_o, w_br), (wab1, wo1) = uv_side, qkvm_side, merge_side
    return xt.reshape(batch, seq, d)
```

```python
import functools
import math
from typing import Callable, NamedTuple

import jax
import jax.numpy as jnp
from jax import lax
from jax.experimental import pallas as pl
from jax.experimental.pallas import tpu as pltpu

D_MODEL = 2048
MEM_LEN = 256
CHUNK = 128
WINDOW = 128
G_GROUPS = 4
G_WIDTH = 1024
G_GROUP_DIM = G_WIDTH // G_GROUPS
SWA_HEADS = 16
SWA_KV_HEADS = 4
SWA_HEAD_DIM = 64
SWA_REP = SWA_HEADS // SWA_KV_HEADS
SWA_WIDTH = SWA_HEADS * SWA_HEAD_DIM
SWA_KV_WIDTH = SWA_KV_HEADS * SWA_HEAD_DIM
MEM_HEADS = 4
MEM_HEAD_DIM = 256
MEM_WIDTH = MEM_HEADS * MEM_HEAD_DIM
N_BRANCH = 3
BRANCH_WIDTH = 1024
D_FF = 5504
EPS = 1e-6
NEG = -1e30

F32 = jnp.float32
BF16 = jnp.bfloat16

LANES = 128
VMEM_LIMIT_BYTES = 60 * 1024 * 1024

FFN_TM = 1024
FFN_TF = 512
D_FF_PAD = -(-D_FF // FFN_TF) * FFN_TF
UV_TM = 1024
UV_SUB = 1024
PROJ_TM = 1024
GATES_TM = 1024
MERGE_TM = 256
SWA_TQ = 1024
MEM_TQ = 2048
W_COLS = 512
CAST_ROWS = 128
MIXER_CAST_ROWS = 16
ALONE_IN_ROWS = 256
ALONE_OUT_ROWS = D_FF // 8

SINGLE = pl.Buffered(1)


def _params(*semantics):
    return pltpu.CompilerParams(dimension_semantics=semantics, vmem_limit_bytes=VMEM_LIMIT_BYTES)


def _rmsnorm_f32(xf, g):
    return xf * lax.rsqrt(jnp.mean(xf * xf, axis=-1, keepdims=True) + EPS) * g


def _dot(a, b):
    return jnp.dot(a, b, preferred_element_type=F32)


def _dot_nt(a, b):
    return lax.dot_general(a, b, (((1,), (1,)), ((), ())), preferred_element_type=F32)


class _SideCast(NamedTuple):
    src: jax.Array
    in_spec: pl.BlockSpec
    out_spec: pl.BlockSpec
    out_shape: jax.ShapeDtypeStruct
    write: Callable
    n_blocks: int


def _side_rows(w, l, rows_blk, step_of):
    _, rows, cols = w.shape
    n_blocks = rows // rows_blk
    assert rows % rows_blk == 0

    def blk(*g):
        return jnp.minimum(step_of(*g), n_blocks - 1)

    def write(step, in_ref, out_ref):
        @pl.when(step < n_blocks)
        def _():
            out_ref[...] = in_ref[...].astype(BF16)

    return _SideCast(
        w,
        pl.BlockSpec((None, rows_blk, cols), lambda *g: (l, blk(*g), 0)),
        pl.BlockSpec((rows_blk, cols), lambda *g: (blk(*g), 0)),
        jax.ShapeDtypeStruct((rows, cols), BF16), write, n_blocks)


def _side_ffn_in(w, l, rows_blk, step_of):
    _, d, _ = w.shape
    assert d % rows_blk == 0
    per_half = d // rows_blk
    n_blocks = 2 * per_half

    def blk(*g):
        return jnp.minimum(step_of(*g), n_blocks - 1)

    def write(step, in_ref, out_ref):
        @pl.when(step < n_blocks)
        def _():
            out_ref[:, :D_FF] = in_ref[...].astype(BF16)
            out_ref[:, D_FF:] = jnp.zeros((rows_blk, D_FF_PAD - D_FF), BF16)

    return _SideCast(
        w,
        pl.BlockSpec((None, rows_blk, D_FF), lambda *g: (l, blk(*g) % per_half, blk(*g) // per_half)),
        pl.BlockSpec((None, rows_blk, D_FF_PAD), lambda *g: (blk(*g) // per_half, blk(*g) % per_half, 0)),
        jax.ShapeDtypeStruct((2, d, D_FF_PAD), BF16), write, n_blocks)


def _side_ffn_out(w, l, rows_blk, step_of):
    _, _, d = w.shape
    assert D_FF % rows_blk == 0
    n_src = D_FF // rows_blk
    n_blocks = pl.cdiv(D_FF_PAD, rows_blk)

    def write(step, in_ref, out_ref):
        @pl.when(step < n_src)
        def _():
            out_ref[...] = in_ref[...].astype(BF16)

        @pl.when((step >= n_src) & (step < n_blocks))
        def _():
            out_ref[...] = jnp.zeros_like(out_ref)

    return _SideCast(
        w,
        pl.BlockSpec((None, rows_blk, d), lambda *g: (l, jnp.minimum(step_of(*g), n_src - 1), 0)),
        pl.BlockSpec((rows_blk, d), lambda *g: (jnp.minimum(step_of(*g), n_blocks - 1), 0)),
        jax.ShapeDtypeStruct((D_FF_PAD, d), BF16), write, n_blocks)


def _hosted_call(body, *, grid, step_of, in_specs, out_specs, out_shape, args, sides, semantics, name,
                 scratch_shapes=()):
    n_in, n_out, k = len(in_specs), len(out_specs), len(sides)
    n_steps = math.prod(grid)
    assert all(s.n_blocks <= n_steps for s in sides)

    def hosted(*refs):
        ins, side_ins = refs[:n_in], refs[n_in:n_in + k]
        outs = refs[n_in + k:n_in + k + n_out]
        side_outs = refs[n_in + k + n_out:n_in + 2 * k + n_out]
        scratch = refs[n_in + 2 * k + n_out:]
        if k:
            step = step_of(*[pl.program_id(a) for a in range(len(grid))])
            for side, i_ref, o_ref in zip(sides, side_ins, side_outs):
                side.write(step, i_ref, o_ref)
        body(*ins, *outs, *scratch)

    res = pl.pallas_call(
        hosted,
        grid=grid,
        in_specs=list(in_specs) + [s.in_spec for s in sides],
        out_specs=list(out_specs) + [s.out_spec for s in sides],
        out_shape=list(out_shape) + [s.out_shape for s in sides],
        scratch_shapes=list(scratch_shapes),
        compiler_params=_params(*semantics),
        name=name,
    )(*args, *[s.src for s in sides])
    return res[:n_out], res[n_out:]


def _cast_alone(make_side):
    side = make_side(lambda i: i)
    _, (out,) = _hosted_call(lambda: None, grid=(side.n_blocks,), step_of=lambda i: i, in_specs=[],
                             out_specs=[], out_shape=[], args=[], sides=[side],
                             semantics=("arbitrary",), name="cast")
    return out


def _ffn_body(x_ref, g_ref, wa_ref, wb_ref, wo_ref, gf_ref, o_ref, n_ref, *, apply_final_norm):
    j = pl.program_id(1)

    @pl.when(j == 0)
    def _():
        x = x_ref[...]
        n_ref[...] = _rmsnorm_f32(x, g_ref[...]).astype(BF16)
        o_ref[...] = x

    n = n_ref[...]
    a = _dot(n, wa_ref[...])
    b = _dot(n, wb_ref[...])
    hid = (0.5 * (a * jax.nn.sigmoid(a) * b)).astype(BF16)
    for c in range(o_ref.shape[1] // FFN_TF):
        cs = slice(c * FFN_TF, (c + 1) * FFN_TF)
        o_ref[:, cs] += _dot(hid, wo_ref[:, cs])

    if apply_final_norm:
        @pl.when(j == pl.num_programs(1) - 1)
        def _():
            o_ref[...] = _rmsnorm_f32(o_ref[...], gf_ref[...])


def _ffn(x, g, wab, wo, g_final, apply_final_norm, make_sides=()):
    t, d = x.shape
    hidden_steps = D_FF_PAD // FFN_TF
    step_of = lambda i, j: i * hidden_steps + j
    (out,), side_out = _hosted_call(
        functools.partial(_ffn_body, apply_final_norm=apply_final_norm),
        grid=(t // FFN_TM, hidden_steps),
        step_of=step_of,
        in_specs=[
            pl.BlockSpec((FFN_TM, d), lambda i, j: (i, 0)),
            pl.BlockSpec((1, d), lambda i, j: (0, 0)),
            pl.BlockSpec((None, d, FFN_TF), lambda i, j: (0, 0, j)),
            pl.BlockSpec((None, d, FFN_TF), lambda i, j: (1, 0, j)),
            pl.BlockSpec((FFN_TF, d), lambda i, j: (j, 0)),
            pl.BlockSpec((1, d), lambda i, j: (0, 0)),
        ],
        out_specs=[pl.BlockSpec((FFN_TM, d), lambda i, j: (i, 0))],
        out_shape=[jax.ShapeDtypeStruct((t, d), F32)],
        args=(x, g.reshape(1, d), wab, wab, wo, g_final.reshape(1, d)),
        sides=[m(step_of) for m in make_sides],
        semantics=("arbitrary", "arbitrary"),
        scratch_shapes=[pltpu.VMEM((FFN_TM, d), BF16)],
        name="ffn",
    )
    return out, side_out


UV_END = 2 * G_WIDTH
QKVM_WIDTH = SWA_WIDTH + 2 * SWA_KV_WIDTH + MEM_WIDTH
QKVM_END = UV_END + QKVM_WIDTH


def _gelu(x):
    return 0.5 * x * (1.0 + lax.erf(x * math.sqrt(0.5)))


def _uv_body(h_ref, g_ref, wu_ref, wv_ref, lng_ref, lnb_ref, ws_ref, bst_ref, n_ref, oa_ref):
    row = lax.broadcasted_iota(jnp.int32, (CHUNK, CHUNK), 0)
    col = lax.broadcasted_iota(jnp.int32, (CHUNK, CHUNK), 1)
    causal = row >= col
    ws = [jnp.where(causal, ws_ref[g], 0.0).astype(BF16) for g in range(G_GROUPS)]

    for sub in range(UV_TM // UV_SUB):
        sub_rows = slice(sub * UV_SUB, (sub + 1) * UV_SUB)
        n = _rmsnorm_f32(h_ref[sub_rows, :], g_ref[...]).astype(BF16)
        n_ref[sub_rows, :] = n
        u = _gelu(_dot(n, wu_ref[...]))
        v = _gelu(_dot(n, wv_ref[...]))
        mu = jnp.mean(v, axis=-1, keepdims=True)
        vc = v - mu
        var = jnp.mean(vc * vc, axis=-1, keepdims=True)
        vn = (vc * lax.rsqrt(var + EPS) * lng_ref[...] + lnb_ref[...]).astype(BF16)
        for g in range(G_GROUPS):
            bias = bst_ref[:, g:g + 1]
            cs = slice(g * G_GROUP_DIM, (g + 1) * G_GROUP_DIM)
            for c in range(UV_SUB // CHUNK):
                rs = slice(c * CHUNK, (c + 1) * CHUNK)
                mixed = _dot(ws[g], vn[rs, cs]) + bias
                out_rows = slice(sub * UV_SUB + c * CHUNK, sub * UV_SUB + (c + 1) * CHUNK)
                oa_ref[out_rows, cs] = (u[rs, cs] * mixed).astype(BF16)


def _uv(h, g, w_in_b, ln_g, ln_b, w_s, b_s):
    t, d = h.shape
    const = lambda i: (0, 0)
    (n, o_a), _ = _hosted_call(
        _uv_body,
        grid=(t // UV_TM,),
        step_of=lambda i: i,
        in_specs=[
            pl.BlockSpec((UV_TM, d), lambda i: (i, 0)),
            pl.BlockSpec((1, d), const),
            pl.BlockSpec((d, G_WIDTH), lambda i: (0, 0), pipeline_mode=SINGLE),
            pl.BlockSpec((d, G_WIDTH), lambda i: (0, 1), pipeline_mode=SINGLE),
            pl.BlockSpec((1, G_WIDTH), const),
            pl.BlockSpec((1, G_WIDTH), const),
            pl.BlockSpec((G_GROUPS, CHUNK, CHUNK), lambda i: (0, 0, 0)),
            pl.BlockSpec((CHUNK, G_GROUPS), const),
        ],
        out_specs=[
            pl.BlockSpec((UV_TM, d), lambda i: (i, 0)),
            pl.BlockSpec((UV_TM, G_WIDTH), lambda i: (i, 0)),
        ],
        out_shape=[
            jax.ShapeDtypeStruct((t, d), BF16),
            jax.ShapeDtypeStruct((t, G_WIDTH), BF16),
        ],
        args=(h, g.reshape(1, d), w_in_b, w_in_b, ln_g.reshape(1, G_WIDTH), ln_b.reshape(1, G_WIDTH), w_s,
              jnp.transpose(b_s)),
        sides=[],
        semantics=("parallel",),
        name="uv_gmlp",
    )
    return n, o_a


SWA_SCALE = 1.0 / math.sqrt(SWA_HEAD_DIM)
HALF_LANES = LANES // 2
KV_PLACED_WIDTH = SWA_KV_HEADS * LANES
QKVM_BLOCKS = QKVM_WIDTH // W_COLS


def _place_heads(z):
    rows = z.shape[0]
    low_half = lax.broadcasted_iota(jnp.int32, (rows, LANES), 1) < HALF_LANES
    zero = jnp.zeros((rows, LANES), F32)
    low, high = [], []
    for pair in range(SWA_KV_HEADS // 2):
        zg = z[:, pair * LANES:(pair + 1) * LANES]
        swapped = pltpu.roll(zg, HALF_LANES, axis=1)
        low += [jnp.where(low_half, zg, zero), jnp.where(low_half, swapped, zero)]
        high += [jnp.where(low_half, zero, swapped), jnp.where(low_half, zero, zg)]
    return (jnp.concatenate(low, axis=1).astype(BF16), jnp.concatenate(high, axis=1).astype(BF16))


def _qkvm_body(n_ref, wq0_ref, wq1_ref, wkv_ref, wm0_ref, wm1_ref,
               q_ref, klo_ref, khi_ref, vlo_ref, vhi_ref, mq_ref):
    n = n_ref[...]
    q_ref[:, :W_COLS] = (_dot(n, wq0_ref[...]) * SWA_SCALE).astype(BF16)
    q_ref[:, W_COLS:] = (_dot(n, wq1_ref[...]) * SWA_SCALE).astype(BF16)
    kv = _dot(n, wkv_ref[...])
    klo_ref[...], khi_ref[...] = _place_heads(kv[:, :SWA_KV_WIDTH])
    vlo_ref[...], vhi_ref[...] = _place_heads(kv[:, SWA_KV_WIDTH:])
    mq_ref[:, :W_COLS] = _dot(n, wm0_ref[...]).astype(BF16)
    mq_ref[:, W_COLS:] = _dot(n, wm1_ref[...]).astype(BF16)


def _qkvm(n, w_in_b):
    t, d = n.shape
    assert SWA_WIDTH == 2 * W_COLS and 2 * SWA_KV_WIDTH == W_COLS and MEM_WIDTH == 2 * W_COLS
    first = UV_END // W_COLS
    widths = (SWA_WIDTH,) + (KV_PLACED_WIDTH,) * 4 + (MEM_WIDTH,)
    w_specs = [pl.BlockSpec((d, W_COLS), functools.partial(lambda c, i: (0, c), first + c),
                            pipeline_mode=SINGLE) for c in range(QKVM_BLOCKS)]
    outs, _ = _hosted_call(
        _qkvm_body,
        grid=(t // PROJ_TM,),
        step_of=lambda i: i,
        in_specs=[pl.BlockSpec((PROJ_TM, d), lambda i: (i, 0))] + w_specs,
        out_specs=[pl.BlockSpec((PROJ_TM, w_), lambda i: (i, 0)) for w_ in widths],
        out_shape=[jax.ShapeDtypeStruct((t, w_), BF16) for w_ in widths],
        args=(n,) + (w_in_b,) * QKVM_BLOCKS,
        sides=[],
        semantics=("parallel",),
        name="qkvm",
    )
    return outs


GATE_BLOCKS = D_MODEL // W_COLS


def _gates_body(n_ref, *refs):
    w_refs, o_ref = refs[:GATE_BLOCKS], refs[GATE_BLOCKS]
    n = n_ref[...]
    for c, w_ref in enumerate(w_refs):
        z = _dot(n, w_ref[...])
        o_ref[:, c * W_COLS:(c + 1) * W_COLS] = (0.5 * jnp.tanh(0.5 * z) + 0.5).astype(BF16)


def _gates(n, w_in_b, make_sides):
    t, d = n.shape
    tiles = t // GATES_TM
    first = QKVM_END // W_COLS
    step_of = lambda b, i: b * tiles + i
    w_specs = [pl.BlockSpec((d, W_COLS), functools.partial(lambda c, b, i: (0, first + b * GATE_BLOCKS + c), c))
               for c in range(GATE_BLOCKS)]
    (gates,), side_out = _hosted_call(
        _gates_body,
        grid=(N_BRANCH, tiles),
        step_of=step_of,
        in_specs=[pl.BlockSpec((GATES_TM, d), lambda b, i: (i, 0))] + w_specs,
        out_specs=[pl.BlockSpec((None, GATES_TM, d), lambda b, i: (b, i, 0))],
        out_shape=[jax.ShapeDtypeStruct((N_BRANCH, t, d), BF16)],
        args=(n,) + (w_in_b,) * GATE_BLOCKS,
        sides=[m(step_of) for m in make_sides],
        semantics=("arbitrary", "arbitrary"),
        name="gates",
    )
    return gates, side_out


def _mkv_body(m_ref, g_ref, w_ref, mk_ref, mv_ref):
    n = _rmsnorm_f32(m_ref[...], g_ref[...]).astype(BF16)
    z = _dot(n, w_ref[...]).astype(BF16)
    mk_ref[...] = z[:, :MEM_WIDTH]
    mv_ref[...] = z[:, MEM_WIDTH:]


def _mkv(mem, g, w):
    t, d = mem.shape
    return pl.pallas_call(
        _mkv_body,
        grid=(t // MEM_LEN,),
        in_specs=[
            pl.BlockSpec((MEM_LEN, d), lambda i: (i, 0)),
            pl.BlockSpec((1, d), lambda i: (0, 0)),
            pl.BlockSpec((d, 2 * MEM_WIDTH), lambda i: (0, 0), pipeline_mode=SINGLE),
        ],
        out_specs=[pl.BlockSpec((MEM_LEN, MEM_WIDTH), lambda i: (i, 0))] * 2,
        out_shape=[jax.ShapeDtypeStruct((t, MEM_WIDTH), BF16)] * 2,
        compiler_params=_params("parallel"),
        name="mem_kv",
    )(mem, g.reshape(1, d), w)


SWA_SLOPES = tuple(2.0 ** (-8.0 * (h + 1) / SWA_HEADS) for h in range(SWA_HEADS))


def _swa_body(sink_ref, q_ref, klo_ref, khi_ref, vlo_ref, vhi_ref,
              klo_p_ref, khi_p_ref, vlo_p_ref, vhi_p_ref, o_ref):
    w = WINDOW
    t = pl.program_id(1)
    i2 = lax.broadcasted_iota(jnp.int32, (2 * w, w), 0) & (w - 1)
    j2 = lax.broadcasted_iota(jnp.int32, (2 * w, w), 1)
    from_prev = j2 > i2
    dist = ((i2 - j2) & (w - 1)).astype(F32)
    no_prev = j2 > jnp.maximum(i2, jnp.where(t == 0, -1, w))
    low_half = j2 < HALF_LANES
    zero = jnp.zeros((2 * w, w), F32)
    ones_cols = (jnp.where(low_half, 1.0, 0.0).astype(BF16), jnp.where(low_half, 0.0, 1.0).astype(BF16))

    def rows2(top, bottom):
        return jnp.concatenate([jnp.full((w, w), top, F32), jnp.full((w, w), bottom, F32)], axis=0)

    for blk in range(SWA_TQ // w):
        rs = slice(blk * w, (blk + 1) * w)

        def band(cur_ref, prev_ref, cols):
            prev = prev_ref[:, cols] if blk == 0 else cur_ref[(blk - 1) * w:blk * w, cols]
            return jnp.concatenate([prev, cur_ref[rs, cols]], axis=0)

        for kvh in range(SWA_KV_HEADS):
            cols = slice(kvh * LANES, (kvh + 1) * LANES)
            q2 = jnp.concatenate([q_ref[rs, (2 * kvh) * LANES:(2 * kvh + 1) * LANES],
                                  q_ref[rs, (2 * kvh + 1) * LANES:(2 * kvh + 2) * LANES]], axis=0)
            keys = jnp.concatenate([band(klo_ref, klo_p_ref, cols), band(khi_ref, khi_p_ref, cols)], axis=0)
            s = _dot_nt(q2, keys)
            acc = None
            sink_terms = []
            for par, (v_ref, v_p_ref) in enumerate(((vlo_ref, vlo_p_ref), (vhi_ref, vhi_p_ref))):
                heads = (kvh * SWA_REP + par, kvh * SWA_REP + 2 + par)
                base = par * 2 * w
                sc = jnp.where(from_prev, s[:, base:base + w], s[:, base + w:base + 2 * w])
                sc = sc - rows2(SWA_SLOPES[heads[0]], SWA_SLOPES[heads[1]]) * dist
                if blk == 0:
                    sc = jnp.where(no_prev, NEG, sc)
                sink = rows2(sink_ref[heads[0]], sink_ref[heads[1]])
                m = jnp.maximum(jnp.broadcast_to(jnp.max(sc, axis=-1, keepdims=True), (2 * w, w)), sink)
                p = jnp.exp(sc - m)
                pcat = jnp.concatenate([jnp.where(from_prev, p, zero).astype(BF16),
                                        jnp.where(from_prev, zero, p).astype(BF16)], axis=1)
                vext = jnp.concatenate([band(v_ref, v_p_ref, cols), ones_cols[par]], axis=1)
                part = _dot(pcat, vext)
                acc = part if acc is None else acc + part
                sink_terms.append(jnp.exp(sink - m))
            denom = acc[:, w:] + jnp.where(low_half, sink_terms[0], sink_terms[1])
            out = (acc[:, :w] / denom).astype(BF16)
            o_ref[rs, (2 * kvh) * LANES:(2 * kvh + 1) * LANES] = out[:w]
            o_ref[rs, (2 * kvh + 1) * LANES:(2 * kvh + 2) * LANES] = out[w:]


def _swa(q, k_lo, k_hi, v_lo, v_hi, sinks, batch, seq):
    t = q.shape[0]
    steps = seq // SWA_TQ
    blocks_per_step = SWA_TQ // WINDOW
    blocks_per_seq = seq // WINDOW

    def cur(b, s):
        return (b * steps + s, 0)

    def prev(b, s):
        return (b * blocks_per_seq + jnp.maximum(s * blocks_per_step - 1, 0), 0)

    kv_cur = pl.BlockSpec((SWA_TQ, KV_PLACED_WIDTH), cur)
    kv_prev = pl.BlockSpec((WINDOW, KV_PLACED_WIDTH), prev)
    return pl.pallas_call(
        _swa_body,
        grid=(batch, steps),
        in_specs=[pl.BlockSpec(memory_space=pltpu.SMEM), pl.BlockSpec((SWA_TQ, SWA_WIDTH), cur)]
        + [kv_cur] * 4 + [kv_prev] * 4,
        out_specs=pl.BlockSpec((SWA_TQ, SWA_WIDTH), cur),
        out_shape=jax.ShapeDtypeStruct((t, SWA_WIDTH), BF16),
        compiler_params=_params("parallel", "parallel"),
        name="swa",
    )(sinks, q, k_lo, k_hi, v_lo, v_hi, k_lo, k_hi, v_lo, v_hi)


MEM_SCALE = 1.0 / math.sqrt(MEM_HEAD_DIM)


def _memattn_body(q_ref, mk_ref, mv_ref, o_ref):
    for h in range(MEM_HEADS):
        hs = slice(h * MEM_HEAD_DIM, (h + 1) * MEM_HEAD_DIM)
        s = _dot_nt(q_ref[:, hs], mk_ref[:, hs]) * MEM_SCALE
        m = jnp.max(s, axis=-1, keepdims=True)
        p = jnp.exp(s - m)
        probs = (p / jnp.sum(p, axis=-1, keepdims=True)).astype(BF16)
        o_ref[:, hs] = _dot(probs, mv_ref[:, hs]).astype(BF16)


def _memattn(mq, mk, mv, batch, seq):
    t = mq.shape[0]
    steps = seq // MEM_TQ
    return pl.pallas_call(
        _memattn_body,
        grid=(batch, steps),
        in_specs=[
            pl.BlockSpec((MEM_TQ, MEM_WIDTH), lambda b, s: (b * steps + s, 0)),
            pl.BlockSpec((MEM_LEN, MEM_WIDTH), lambda b, s: (b, 0)),
            pl.BlockSpec((MEM_LEN, MEM_WIDTH), lambda b, s: (b, 0)),
        ],
        out_specs=pl.BlockSpec((MEM_TQ, MEM_WIDTH), lambda b, s: (b * steps + s, 0)),
        out_shape=jax.ShapeDtypeStruct((t, MEM_WIDTH), BF16),
        compiler_params=_params("parallel", "parallel"),
        name="mem_attn",
    )(mq, mk, mv)


def _merge_body(h_ref, gt_ref, oa_ref, ob_ref, oc_ref, wbr_ref, wo_ref, o_ref):
    y = gt_ref[0].astype(F32) * _dot(oa_ref[...], wbr_ref[0])
    y += gt_ref[1].astype(F32) * _dot(ob_ref[...], wbr_ref[1])
    y += gt_ref[2].astype(F32) * _dot(oc_ref[...], wbr_ref[2])
    o_ref[...] = h_ref[...] + _dot(y.astype(BF16), wo_ref[...])


def _merge(h, gates, o_a, o_b, o_c, w_branch, w_out, make_sides):
    t, d = h.shape
    row = lambda i: (i, 0)
    (out,), side_out = _hosted_call(
        _merge_body,
        grid=(t // MERGE_TM,),
        step_of=lambda i: i,
        in_specs=[
            pl.BlockSpec((MERGE_TM, d), row),
            pl.BlockSpec((N_BRANCH, MERGE_TM, d), lambda i: (0, i, 0)),
            pl.BlockSpec((MERGE_TM, BRANCH_WIDTH), row),
            pl.BlockSpec((MERGE_TM, BRANCH_WIDTH), row),
            pl.BlockSpec((MERGE_TM, BRANCH_WIDTH), row),
            pl.BlockSpec((N_BRANCH, BRANCH_WIDTH, d), lambda i: (0, 0, 0), pipeline_mode=SINGLE),
            pl.BlockSpec((d, d), lambda i: (0, 0), pipeline_mode=SINGLE),
        ],
        out_specs=[pl.BlockSpec((MERGE_TM, d), row)],
        out_shape=[jax.ShapeDtypeStruct((t, d), F32)],
        args=(h, gates, o_a, o_b, o_c, w_branch, w_out),
        sides=[m(lambda i: i) for m in make_sides],
        semantics=("arbitrary",),
        name="merge",
    )
    return out, side_out


def kernel(x, mem, g_ffn1, w_ffn1_in, w_ffn1_out, g_mix, w_in, gmlp_ln_g, gmlp_ln_b, w_s, b_s, swa_sinks, g_mem, w_mem_kv, w_branch, w_out, g_ffn2, w_ffn2_in, w_ffn2_out, g_final):
    batch, seq, d = x.shape
    depth = w_in.shape[0]
    xt = x.reshape(batch * seq, d)
    memt = mem.reshape(batch * MEM_LEN, d)
    w_branch_rows = w_branch.reshape(depth, N_BRANCH * BRANCH_WIDTH, d)

    def mixer_casts(l):
        return [functools.partial(_side_rows, w_in, l, MIXER_CAST_ROWS),
                functools.partial(_side_rows, w_mem_kv, l, MIXER_CAST_ROWS),
                functools.partial(_side_rows, w_out, l, MIXER_CAST_ROWS),
                functools.partial(_side_rows, w_branch_rows, l, 2 * MIXER_CAST_ROWS)]

    def ffn_casts(w_in_ffn, w_out_ffn, l, rows_in=CAST_ROWS, rows_out=CAST_ROWS):
        return [functools.partial(_side_ffn_in, w_in_ffn, l, rows_in),
                functools.partial(_side_ffn_out, w_out_ffn, l, rows_out)]

    wab1, wo1 = [_cast_alone(m) for m in ffn_casts(w_ffn1_in, w_ffn1_out, 0, ALONE_IN_ROWS, ALONE_OUT_ROWS)]

    for l in range(depth):
        more = l + 1 < depth
        h, (w_in_b, w_mkv, w_o, w_br) = _ffn(xt, g_ffn1[l], wab1, wo1, g_final, False, mixer_casts(l))
        n, o_a = _uv(h, g_mix[l], w_in_b, gmlp_ln_g[l], gmlp_ln_b[l], w_s[l], b_s[l])
        q, k_lo, k_hi, v_lo, v_hi, mq = _qkvm(n, w_in_b)
        gates, (wab2, wo2) = _gates(n, w_in_b, ffn_casts(w_ffn2_in, w_ffn2_out, l))
        mk, mv = _mkv(memt, g_mem[l], w_mkv)
        o_b = _swa(q, k_lo, k_hi, v_lo, v_hi, swa_sinks[l], batch, seq)
        o_c = _memattn(mq, mk, mv, batch, seq)
        h, merge_side = _merge(h, gates, o_a, o_b, o_c, w_br.reshape(N_BRANCH, BRANCH_WIDTH, d), w_o,
                               ffn_casts(w_ffn1_in, w_ffn1_out, l + 1) if more else [])
        xt, _ = _ffn(h, g_ffn2[l], wab2, wo2, g_final, not more)
        if more:
            wab1, wo1 = merge_side
    return xt.reshape(batch, seq, d)
```

```python
import functools
import math
from typing import Callable, NamedTuple

import jax
import jax.numpy as jnp
from jax import lax
from jax.experimental import pallas as pl
from jax.experimental.pallas import tpu as pltpu

D_MODEL = 2048
MEM_LEN = 256
CHUNK = 128
WINDOW = 128
G_GROUPS = 4
G_WIDTH = 1024
G_GROUP_DIM = G_WIDTH // G_GROUPS
SWA_HEADS = 16
SWA_KV_HEADS = 4
SWA_HEAD_DIM = 64
SWA_REP = SWA_HEADS // SWA_KV_HEADS
SWA_WIDTH = SWA_HEADS * SWA_HEAD_DIM
SWA_KV_WIDTH = SWA_KV_HEADS * SWA_HEAD_DIM
MEM_HEADS = 4
MEM_HEAD_DIM = 256
MEM_WIDTH = MEM_HEADS * MEM_HEAD_DIM
N_BRANCH = 3
BRANCH_WIDTH = 1024
D_FF = 5504
EPS = 1e-6
NEG = -1e30

F32 = jnp.float32
BF16 = jnp.bfloat16

LANES = 128
VMEM_LIMIT_BYTES = 60 * 1024 * 1024

FFN_TM = 1024
FFN_TF = 512
D_FF_PAD = -(-D_FF // FFN_TF) * FFN_TF
UV_TM = 1024
UV_SUB = 1024
PROJ_TM = 1024
GATES_TM = 1024
MERGE_TM = 256
SWA_TQ = 1024
MEM_TQ = 2048
W_COLS = 512
CAST_ROWS = 128
MIXER_CAST_ROWS = 16
ALONE_IN_ROWS = 256
ALONE_OUT_ROWS = D_FF // 8

SINGLE = pl.Buffered(1)


def _params(*semantics):
    return pltpu.CompilerParams(dimension_semantics=semantics, vmem_limit_bytes=VMEM_LIMIT_BYTES)


def _rmsnorm_f32(xf, g):
    return xf * lax.rsqrt(jnp.mean(xf * xf, axis=-1, keepdims=True) + EPS) * g


def _dot(a, b):
    return jnp.dot(a, b, preferred_element_type=F32)


def _dot_nt(a, b):
    return lax.dot_general(a, b, (((1,), (1,)), ((), ())), preferred_element_type=F32)


class _SideCast(NamedTuple):
    src: jax.Array
    in_spec: pl.BlockSpec
    out_spec: pl.BlockSpec
    out_shape: jax.ShapeDtypeStruct
    write: Callable
    n_blocks: int


def _side_rows(w, l, rows_blk, step_of):
    _, rows, cols = w.shape
    n_blocks = rows // rows_blk
    assert rows % rows_blk == 0

    def blk(*g):
        return jnp.minimum(step_of(*g), n_blocks - 1)

    def write(step, in_ref, out_ref):
        @pl.when(step < n_blocks)
        def _():
            out_ref[...] = in_ref[...].astype(BF16)

    return _SideCast(
        w,
        pl.BlockSpec((None, rows_blk, cols), lambda *g: (l, blk(*g), 0)),
        pl.BlockSpec((rows_blk, cols), lambda *g: (blk(*g), 0)),
        jax.ShapeDtypeStruct((rows, cols), BF16), write, n_blocks)


def _side_ffn_in(w, l, rows_blk, step_of):
    _, d, _ = w.shape
    assert d % rows_blk == 0
    per_half = d // rows_blk
    n_blocks = 2 * per_half

    def blk(*g):
        return jnp.minimum(step_of(*g), n_blocks - 1)

    def write(step, in_ref, out_ref):
        @pl.when(step < n_blocks)
        def _():
            out_ref[:, :D_FF] = in_ref[...].astype(BF16)
            out_ref[:, D_FF:] = jnp.zeros((rows_blk, D_FF_PAD - D_FF), BF16)

    return _SideCast(
        w,
        pl.BlockSpec((None, rows_blk, D_FF), lambda *g: (l, blk(*g) % per_half, blk(*g) // per_half)),
        pl.BlockSpec((None, rows_blk, D_FF_PAD), lambda *g: (blk(*g) // per_half, blk(*g) % per_half, 0)),
        jax.ShapeDtypeStruct((2, d, D_FF_PAD), BF16), write, n_blocks)


def _side_ffn_out(w, l, rows_blk, step_of):
    _, _, d = w.shape
    assert D_FF % rows_blk == 0
    n_src = D_FF // rows_blk
    n_blocks = pl.cdiv(D_FF_PAD, rows_blk)

    def write(step, in_ref, out_ref):
        @pl.when(step < n_src)
        def _():
            out_ref[...] = in_ref[...].astype(BF16)

        @pl.when((step >= n_src) & (step < n_blocks))
        def _():
            out_ref[...] = jnp.zeros_like(out_ref)

    return _SideCast(
        w,
        pl.BlockSpec((None, rows_blk, d), lambda *g: (l, jnp.minimum(step_of(*g), n_src - 1), 0)),
        pl.BlockSpec((rows_blk, d), lambda *g: (jnp.minimum(step_of(*g), n_blocks - 1), 0)),
        jax.ShapeDtypeStruct((D_FF_PAD, d), BF16), write, n_blocks)


def _hosted_call(body, *, grid, step_of, in_specs, out_specs, out_shape, args, sides, semantics, name,
                 scratch_shapes=()):
    n_in, n_out, k = len(in_specs), len(out_specs), len(sides)
    n_steps = math.prod(grid)
    assert all(s.n_blocks <= n_steps for s in sides)

    def hosted(*refs):
        ins, side_ins = refs[:n_in], refs[n_in:n_in + k]
        outs = refs[n_in + k:n_in + k + n_out]
        side_outs = refs[n_in + k + n_out:n_in + 2 * k + n_out]
        scratch = refs[n_in + 2 * k + n_out:]
        if k:
            step = step_of(*[pl.program_id(a) for a in range(len(grid))])
            for side, i_ref, o_ref in zip(sides, side_ins, side_outs):
                side.write(step, i_ref, o_ref)
        body(*ins, *outs, *scratch)

    res = pl.pallas_call(
        hosted,
        grid=grid,
        in_specs=list(in_specs) + [s.in_spec for s in sides],
        out_specs=list(out_specs) + [s.out_spec for s in sides],
        out_shape=list(out_shape) + [s.out_shape for s in sides],
        scratch_shapes=list(scratch_shapes),
        compiler_params=_params(*semantics),
        name=name,
    )(*args, *[s.src for s in sides])
    return res[:n_out], res[n_out:]


def _cast_alone(make_side):
    side = make_side(lambda i: i)
    _, (out,) = _hosted_call(lambda: None, grid=(side.n_blocks,), step_of=lambda i: i, in_specs=[],
                             out_specs=[], out_shape=[], args=[], sides=[side],
                             semantics=("arbitrary",), name="cast")
    return out


def _ffn_body(x_ref, g_ref, wa_ref, wb_ref, wo_ref, gf_ref, o_ref, n_ref, *, apply_final_norm):
    j = pl.program_id(1)

    @pl.when(j == 0)
    def _():
        x = x_ref[...]
        n_ref[...] = _rmsnorm_f32(x, g_ref[...]).astype(BF16)
        o_ref[...] = x

    n = n_ref[...]
    a = _dot(n, wa_ref[...])
    b = _dot(n, wb_ref[...])
    half_a = 0.5 * a
    hid = (0.5 * ((half_a * jnp.tanh(half_a) + half_a) * b)).astype(BF16)
    for c in range(o_ref.shape[1] // FFN_TF):
        cs = slice(c * FFN_TF, (c + 1) * FFN_TF)
        o_ref[:, cs] += _dot(hid, wo_ref[:, cs])

    if apply_final_norm:
        @pl.when(j == pl.num_programs(1) - 1)
        def _():
            o_ref[...] = _rmsnorm_f32(o_ref[...], gf_ref[...])


def _ffn(x, g, wab, wo, g_final, apply_final_norm, make_sides=()):
    t, d = x.shape
    hidden_steps = D_FF_PAD // FFN_TF
    step_of = lambda i, j: i * hidden_steps + j
    (out,), side_out = _hosted_call(
        functools.partial(_ffn_body, apply_final_norm=apply_final_norm),
        grid=(t // FFN_TM, hidden_steps),
        step_of=step_of,
        in_specs=[
            pl.BlockSpec((FFN_TM, d), lambda i, j: (i, 0)),
            pl.BlockSpec((1, d), lambda i, j: (0, 0)),
            pl.BlockSpec((None, d, FFN_TF), lambda i, j: (0, 0, j)),
            pl.BlockSpec((None, d, FFN_TF), lambda i, j: (1, 0, j)),
            pl.BlockSpec((FFN_TF, d), lambda i, j: (j, 0)),
            pl.BlockSpec((1, d), lambda i, j: (0, 0)),
        ],
        out_specs=[pl.BlockSpec((FFN_TM, d), lambda i, j: (i, 0))],
        out_shape=[jax.ShapeDtypeStruct((t, d), F32)],
        args=(x, g.reshape(1, d), wab, wab, wo, g_final.reshape(1, d)),
        sides=[m(step_of) for m in make_sides],
        semantics=("arbitrary", "arbitrary"),
        scratch_shapes=[pltpu.VMEM((FFN_TM, d), BF16)],
        name="ffn",
    )
    return out, side_out


UV_END = 2 * G_WIDTH
QKVM_WIDTH = SWA_WIDTH + 2 * SWA_KV_WIDTH + MEM_WIDTH
QKVM_END = UV_END + QKVM_WIDTH


def _gelu(x):
    return 0.5 * x * (1.0 + lax.erf(x * math.sqrt(0.5)))


def _uv_body(h_ref, g_ref, wu_ref, wv_ref, lng_ref, lnb_ref, ws_ref, bst_ref, n_ref, oa_ref):
    row = lax.broadcasted_iota(jnp.int32, (CHUNK, CHUNK), 0)
    col = lax.broadcasted_iota(jnp.int32, (CHUNK, CHUNK), 1)
    causal = row >= col
    ws = [jnp.where(causal, ws_ref[g], 0.0).astype(BF16) for g in range(G_GROUPS)]

    for sub in range(UV_TM // UV_SUB):
        sub_rows = slice(sub * UV_SUB, (sub + 1) * UV_SUB)
        n = _rmsnorm_f32(h_ref[sub_rows, :], g_ref[...]).astype(BF16)
        n_ref[sub_rows, :] = n
        u = _gelu(_dot(n, wu_ref[...]))
        v = _gelu(_dot(n, wv_ref[...]))
        mu = jnp.mean(v, axis=-1, keepdims=True)
        vc = v - mu
        var = jnp.mean(vc * vc, axis=-1, keepdims=True)
        vn = (vc * lax.rsqrt(var + EPS) * lng_ref[...] + lnb_ref[...]).astype(BF16)
        for g in range(G_GROUPS):
            bias = bst_ref[:, g:g + 1]
            cs = slice(g * G_GROUP_DIM, (g + 1) * G_GROUP_DIM)
            for c in range(UV_SUB // CHUNK):
                rs = slice(c * CHUNK, (c + 1) * CHUNK)
                mixed = _dot(ws[g], vn[rs, cs]) + bias
                out_rows = slice(sub * UV_SUB + c * CHUNK, sub * UV_SUB + (c + 1) * CHUNK)
                oa_ref[out_rows, cs] = (u[rs, cs] * mixed).astype(BF16)


def _uv(h, g, w_in_b, ln_g, ln_b, w_s, b_s):
    t, d = h.shape
    const = lambda i: (0, 0)
    (n, o_a), _ = _hosted_call(
        _uv_body,
        grid=(t // UV_TM,),
        step_of=lambda i: i,
        in_specs=[
            pl.BlockSpec((UV_TM, d), lambda i: (i, 0)),
            pl.BlockSpec((1, d), const),
            pl.BlockSpec((d, G_WIDTH), lambda i: (0, 0), pipeline_mode=SINGLE),
            pl.BlockSpec((d, G_WIDTH), lambda i: (0, 1), pipeline_mode=SINGLE),
            pl.BlockSpec((1, G_WIDTH), const),
            pl.BlockSpec((1, G_WIDTH), const),
            pl.BlockSpec((G_GROUPS, CHUNK, CHUNK), lambda i: (0, 0, 0)),
            pl.BlockSpec((CHUNK, G_GROUPS), const),
        ],
        out_specs=[
            pl.BlockSpec((UV_TM, d), lambda i: (i, 0)),
            pl.BlockSpec((UV_TM, G_WIDTH), lambda i: (i, 0)),
        ],
        out_shape=[
            jax.ShapeDtypeStruct((t, d), BF16),
            jax.ShapeDtypeStruct((t, G_WIDTH), BF16),
        ],
        args=(h, g.reshape(1, d), w_in_b, w_in_b, ln_g.reshape(1, G_WIDTH), ln_b.reshape(1, G_WIDTH), w_s,
              jnp.transpose(b_s)),
        sides=[],
        semantics=("parallel",),
        name="uv_gmlp",
    )
    return n, o_a


SWA_SCALE = 1.0 / math.sqrt(SWA_HEAD_DIM)
HALF_LANES = LANES // 2
KV_PLACED_WIDTH = SWA_KV_HEADS * LANES
QKVM_BLOCKS = QKVM_WIDTH // W_COLS


def _place_heads(z):
    rows = z.shape[0]
    low_half = lax.broadcasted_iota(jnp.int32, (rows, LANES), 1) < HALF_LANES
    zero = jnp.zeros((rows, LANES), F32)
    low, high = [], []
    for pair in range(SWA_KV_HEADS // 2):
        zg = z[:, pair * LANES:(pair + 1) * LANES]
        swapped = pltpu.roll(zg, HALF_LANES, axis=1)
        low += [jnp.where(low_half, zg, zero), jnp.where(low_half, swapped, zero)]
        high += [jnp.where(low_half, zero, swapped), jnp.where(low_half, zero, zg)]
    return (jnp.concatenate(low, axis=1).astype(BF16), jnp.concatenate(high, axis=1).astype(BF16))


def _qkvm_body(n_ref, wq0_ref, wq1_ref, wkv_ref, wm0_ref, wm1_ref,
               q_ref, klo_ref, khi_ref, vlo_ref, vhi_ref, mq_ref):
    n = n_ref[...]
    q_ref[:, :W_COLS] = (_dot(n, wq0_ref[...]) * SWA_SCALE).astype(BF16)
    q_ref[:, W_COLS:] = (_dot(n, wq1_ref[...]) * SWA_SCALE).astype(BF16)
    kv = _dot(n, wkv_ref[...])
    klo_ref[...], khi_ref[...] = _place_heads(kv[:, :SWA_KV_WIDTH])
    vlo_ref[...], vhi_ref[...] = _place_heads(kv[:, SWA_KV_WIDTH:])
    mq_ref[:, :W_COLS] = _dot(n, wm0_ref[...]).astype(BF16)
    mq_ref[:, W_COLS:] = _dot(n, wm1_ref[...]).astype(BF16)


def _qkvm(n, w_in_b):
    t, d = n.shape
    assert SWA_WIDTH == 2 * W_COLS and 2 * SWA_KV_WIDTH == W_COLS and MEM_WIDTH == 2 * W_COLS
    first = UV_END // W_COLS
    widths = (SWA_WIDTH,) + (KV_PLACED_WIDTH,) * 4 + (MEM_WIDTH,)
    w_specs = [pl.BlockSpec((d, W_COLS), functools.partial(lambda c, i: (0, c), first + c),
                            pipeline_mode=SINGLE) for c in range(QKVM_BLOCKS)]
    outs, _ = _hosted_call(
        _qkvm_body,
        grid=(t // PROJ_TM,),
        step_of=lambda i: i,
        in_specs=[pl.BlockSpec((PROJ_TM, d), lambda i: (i, 0))] + w_specs,
        out_specs=[pl.BlockSpec((PROJ_TM, w_), lambda i: (i, 0)) for w_ in widths],
        out_shape=[jax.ShapeDtypeStruct((t, w_), BF16) for w_ in widths],
        args=(n,) + (w_in_b,) * QKVM_BLOCKS,
        sides=[],
        semantics=("parallel",),
        name="qkvm",
    )
    return outs


GATE_BLOCKS = D_MODEL // W_COLS


def _gates_body(n_ref, *refs):
    w_refs, o_ref = refs[:GATE_BLOCKS], refs[GATE_BLOCKS]
    n = n_ref[...]
    for c, w_ref in enumerate(w_refs):
        z = _dot(n, w_ref[...])
        o_ref[:, c * W_COLS:(c + 1) * W_COLS] = (0.5 * jnp.tanh(0.5 * z) + 0.5).astype(BF16)


def _gates(n, w_in_b, make_sides):
    t, d = n.shape
    tiles = t // GATES_TM
    first = QKVM_END // W_COLS
    step_of = lambda b, i: b * tiles + i
    w_specs = [pl.BlockSpec((d, W_COLS), functools.partial(lambda c, b, i: (0, first + b * GATE_BLOCKS + c), c))
               for c in range(GATE_BLOCKS)]
    (gates,), side_out = _hosted_call(
        _gates_body,
        grid=(N_BRANCH, tiles),
        step_of=step_of,
        in_specs=[pl.BlockSpec((GATES_TM, d), lambda b, i: (i, 0))] + w_specs,
        out_specs=[pl.BlockSpec((None, GATES_TM, d), lambda b, i: (b, i, 0))],
        out_shape=[jax.ShapeDtypeStruct((N_BRANCH, t, d), BF16)],
        args=(n,) + (w_in_b,) * GATE_BLOCKS,
        sides=[m(step_of) for m in make_sides],
        semantics=("arbitrary", "arbitrary"),
        name="gates",
    )
    return gates, side_out


def _mkv_body(m_ref, g_ref, w_ref, mk_ref, mv_ref):
    n = _rmsnorm_f32(m_ref[...], g_ref[...]).astype(BF16)
    z = _dot(n, w_ref[...]).astype(BF16)
    mk_ref[...] = z[:, :MEM_WIDTH]
    mv_ref[...] = z[:, MEM_WIDTH:]


def _mkv(mem, g, w):
    t, d = mem.shape
    return pl.pallas_call(
        _mkv_body,
        grid=(t // MEM_LEN,),
        in_specs=[
            pl.BlockSpec((MEM_LEN, d), lambda i: (i, 0)),
            pl.BlockSpec((1, d), lambda i: (0, 0)),
            pl.BlockSpec((d, 2 * MEM_WIDTH), lambda i: (0, 0), pipeline_mode=SINGLE),
        ],
        out_specs=[pl.BlockSpec((MEM_LEN, MEM_WIDTH), lambda i: (i, 0))] * 2,
        out_shape=[jax.ShapeDtypeStruct((t, MEM_WIDTH), BF16)] * 2,
        compiler_params=_params("parallel"),
        name="mem_kv",
    )(mem, g.reshape(1, d), w)


SWA_SLOPES = tuple(2.0 ** (-8.0 * (h + 1) / SWA_HEADS) for h in range(SWA_HEADS))


def _swa_body(sink_ref, q_ref, klo_ref, khi_ref, vlo_ref, vhi_ref,
              klo_p_ref, khi_p_ref, vlo_p_ref, vhi_p_ref, o_ref):
    w = WINDOW
    t = pl.program_id(1)
    i2 = lax.broadcasted_iota(jnp.int32, (2 * w, w), 0) & (w - 1)
    j2 = lax.broadcasted_iota(jnp.int32, (2 * w, w), 1)
    from_prev = j2 > i2
    dist = ((i2 - j2) & (w - 1)).astype(F32)
    no_prev = j2 > jnp.maximum(i2, jnp.where(t == 0, -1, w))
    low_half = j2 < HALF_LANES
    zero = jnp.zeros((2 * w, w), F32)
    ones_cols = (jnp.where(low_half, 1.0, 0.0).astype(BF16), jnp.where(low_half, 0.0, 1.0).astype(BF16))

    def rows2(top, bottom):
        return jnp.concatenate([jnp.full((w, w), top, F32), jnp.full((w, w), bottom, F32)], axis=0)

    for blk in range(SWA_TQ // w):
        rs = slice(blk * w, (blk + 1) * w)

        def band(cur_ref, prev_ref, cols):
            prev = prev_ref[:, cols] if blk == 0 else cur_ref[(blk - 1) * w:blk * w, cols]
            return jnp.concatenate([prev, cur_ref[rs, cols]], axis=0)

        for kvh in range(SWA_KV_HEADS):
            cols = slice(kvh * LANES, (kvh + 1) * LANES)
            q2 = jnp.concatenate([q_ref[rs, (2 * kvh) * LANES:(2 * kvh + 1) * LANES],
                                  q_ref[rs, (2 * kvh + 1) * LANES:(2 * kvh + 2) * LANES]], axis=0)
            keys = jnp.concatenate([band(klo_ref, klo_p_ref, cols), band(khi_ref, khi_p_ref, cols)], axis=0)
            s = _dot_nt(q2, keys)
            acc = None
            sink_terms = []
            for par, (v_ref, v_p_ref) in enumerate(((vlo_ref, vlo_p_ref), (vhi_ref, vhi_p_ref))):
                heads = (kvh * SWA_REP + par, kvh * SWA_REP + 2 + par)
                base = par * 2 * w
                sc = jnp.where(from_prev, s[:, base:base + w], s[:, base + w:base + 2 * w])
                sc = sc - rows2(SWA_SLOPES[heads[0]], SWA_SLOPES[heads[1]]) * dist
                if blk == 0:
                    sc = jnp.where(no_prev, NEG, sc)
                sink = rows2(sink_ref[heads[0]], sink_ref[heads[1]])
                m = jnp.maximum(jnp.broadcast_to(jnp.max(sc, axis=-1, keepdims=True), (2 * w, w)), sink)
                p = jnp.exp(sc - m)
                pcat = jnp.concatenate([jnp.where(from_prev, p, zero).astype(BF16),
                                        jnp.where(from_prev, zero, p).astype(BF16)], axis=1)
                vext = jnp.concatenate([band(v_ref, v_p_ref, cols), ones_cols[par]], axis=1)
                part = _dot(pcat, vext)
                acc = part if acc is None else acc + part
                sink_terms.append(jnp.exp(sink - m))
            denom = acc[:, w:] + jnp.where(low_half, sink_terms[0], sink_terms[1])
            out = (acc[:, :w] / denom).astype(BF16)
            o_ref[rs, (2 * kvh) * LANES:(2 * kvh + 1) * LANES] = out[:w]
            o_ref[rs, (2 * kvh + 1) * LANES:(2 * kvh + 2) * LANES] = out[w:]


def _swa(q, k_lo, k_hi, v_lo, v_hi, sinks, batch, seq):
    t = q.shape[0]
    steps = seq // SWA_TQ
    blocks_per_step = SWA_TQ // WINDOW
    blocks_per_seq = seq // WINDOW

    def cur(b, s):
        return (b * steps + s, 0)

    def prev(b, s):
        return (b * blocks_per_seq + jnp.maximum(s * blocks_per_step - 1, 0), 0)

    kv_cur = pl.BlockSpec((SWA_TQ, KV_PLACED_WIDTH), cur)
    kv_prev = pl.BlockSpec((WINDOW, KV_PLACED_WIDTH), prev)
    return pl.pallas_call(
        _swa_body,
        grid=(batch, steps),
        in_specs=[pl.BlockSpec(memory_space=pltpu.SMEM), pl.BlockSpec((SWA_TQ, SWA_WIDTH), cur)]
        + [kv_cur] * 4 + [kv_prev] * 4,
        out_specs=pl.BlockSpec((SWA_TQ, SWA_WIDTH), cur),
        out_shape=jax.ShapeDtypeStruct((t, SWA_WIDTH), BF16),
        compiler_params=_params("parallel", "parallel"),
        name="swa",
    )(sinks, q, k_lo, k_hi, v_lo, v_hi, k_lo, k_hi, v_lo, v_hi)


MEM_SCALE = 1.0 / math.sqrt(MEM_HEAD_DIM)


def _memattn_body(q_ref, mk_ref, mv_ref, o_ref):
    for h in range(MEM_HEADS):
        hs = slice(h * MEM_HEAD_DIM, (h + 1) * MEM_HEAD_DIM)
        s = _dot_nt(q_ref[:, hs], mk_ref[:, hs]) * MEM_SCALE
        m = jnp.max(s, axis=-1, keepdims=True)
        p = jnp.exp(s - m)
        probs = (p / jnp.sum(p, axis=-1, keepdims=True)).astype(BF16)
        o_ref[:, hs] = _dot(probs, mv_ref[:, hs]).astype(BF16)


def _memattn(mq, mk, mv, batch, seq):
    t = mq.shape[0]
    steps = seq // MEM_TQ
    return pl.pallas_call(
        _memattn_body,
        grid=(batch, steps),
        in_specs=[
            pl.BlockSpec((MEM_TQ, MEM_WIDTH), lambda b, s: (b * steps + s, 0)),
            pl.BlockSpec((MEM_LEN, MEM_WIDTH), lambda b, s: (b, 0)),
            pl.BlockSpec((MEM_LEN, MEM_WIDTH), lambda b, s: (b, 0)),
        ],
        out_specs=pl.BlockSpec((MEM_TQ, MEM_WIDTH), lambda b, s: (b * steps + s, 0)),
        out_shape=jax.ShapeDtypeStruct((t, MEM_WIDTH), BF16),
        compiler_params=_params("parallel", "parallel"),
        name="mem_attn",
    )(mq, mk, mv)


def _merge_body(h_ref, gt_ref, oa_ref, ob_ref, oc_ref, wbr_ref, wo_ref, o_ref):
    y = gt_ref[0].astype(F32) * _dot(oa_ref[...], wbr_ref[0])
    y += gt_ref[1].astype(F32) * _dot(ob_ref[...], wbr_ref[1])
    y += gt_ref[2].astype(F32) * _dot(oc_ref[...], wbr_ref[2])
    o_ref[...] = h_ref[...] + _dot(y.astype(BF16), wo_ref[...])


def _merge(h, gates, o_a, o_b, o_c, w_branch, w_out, make_sides):
    t, d = h.shape
    row = lambda i: (i, 0)
    (out,), side_out = _hosted_call(
        _merge_body,
        grid=(t // MERGE_TM,),
        step_of=lambda i: i,
        in_specs=[
            pl.BlockSpec((MERGE_TM, d), row),
            pl.BlockSpec((N_BRANCH, MERGE_TM, d), lambda i: (0, i, 0)),
            pl.BlockSpec((MERGE_TM, BRANCH_WIDTH), row),
            pl.BlockSpec((MERGE_TM, BRANCH_WIDTH), row),
            pl.BlockSpec((MERGE_TM, BRANCH_WIDTH), row),
            pl.BlockSpec((N_BRANCH, BRANCH_WIDTH, d), lambda i: (0, 0, 0), pipeline_mode=SINGLE),
            pl.BlockSpec((d, d), lambda i: (0, 0), pipeline_mode=SINGLE),
        ],
        out_specs=[pl.BlockSpec((MERGE_TM, d), row)],
        out_shape=[jax.ShapeDtypeStruct((t, d), F32)],
        args=(h, gates, o_a, o_b, o_c, w_branch, w_out),
        sides=[m(lambda i: i) for m in make_sides],
        semantics=("arbitrary",),
        name="merge",
    )
    return out, side_out


def kernel(x, mem, g_ffn1, w_ffn1_in, w_ffn1_out, g_mix, w_in, gmlp_ln_g, gmlp_ln_b, w_s, b_s, swa_sinks, g_mem, w_mem_kv, w_branch, w_out, g_ffn2, w_ffn2_in, w_ffn2_out, g_final):
    batch, seq, d = x.shape
    depth = w_in.shape[0]
    xt = x.reshape(batch * seq, d)
    memt = mem.reshape(batch * MEM_LEN, d)
    w_branch_rows = w_branch.reshape(depth, N_BRANCH * BRANCH_WIDTH, d)

    def mixer_casts(l):
        return [functools.partial(_side_rows, w_in, l, MIXER_CAST_ROWS),
                functools.partial(_side_rows, w_mem_kv, l, MIXER_CAST_ROWS),
                functools.partial(_side_rows, w_out, l, MIXER_CAST_ROWS),
                functools.partial(_side_rows, w_branch_rows, l, 2 * MIXER_CAST_ROWS)]

    def ffn_casts(w_in_ffn, w_out_ffn, l, rows_in=CAST_ROWS, rows_out=CAST_ROWS):
        return [functools.partial(_side_ffn_in, w_in_ffn, l, rows_in),
                functools.partial(_side_ffn_out, w_out_ffn, l, rows_out)]

    wab1, wo1 = [_cast_alone(m) for m in ffn_casts(w_ffn1_in, w_ffn1_out, 0, ALONE_IN_ROWS, ALONE_OUT_ROWS)]

    for l in range(depth):
        more = l + 1 < depth
        h, (w_in_b, w_mkv, w_o, w_br) = _ffn(xt, g_ffn1[l], wab1, wo1, g_final, False, mixer_casts(l))
        n, o_a = _uv(h, g_mix[l], w_in_b, gmlp_ln_g[l], gmlp_ln_b[l], w_s[l], b_s[l])
        q, k_lo, k_hi, v_lo, v_hi, mq = _qkvm(n, w_in_b)
        gates, (wab2, wo2) = _gates(n, w_in_b, ffn_casts(w_ffn2_in, w_ffn2_out, l))
        mk, mv = _mkv(memt, g_mem[l], w_mkv)
        o_b = _swa(q, k_lo, k_hi, v_lo, v_hi, swa_sinks[l], batch, seq)
        o_c = _memattn(mq, mk, mv, batch, seq)
        h, merge_side = _merge(h, gates, o_a, o_b, o_c, w_br.reshape(N_BRANCH, BRANCH_WIDTH, d), w_o,
                               ffn_casts(w_ffn1_in, w_ffn1_out, l + 1) if more else [])
        xt, _ = _ffn(h, g_ffn2[l], wab2, wo2, g_final, not more)
        if more:
            wab1, wo1 = merge_side
    return xt.reshape(batch, seq, d)
```

```python
import functools
import math
from typing import Callable, NamedTuple

import jax
import jax.numpy as jnp
from jax import lax
from jax.experimental import pallas as pl
from jax.experimental.pallas import tpu as pltpu

D_MODEL = 2048
MEM_LEN = 256
CHUNK = 128
WINDOW = 128
G_GROUPS = 4
G_WIDTH = 1024
G_GROUP_DIM = G_WIDTH // G_GROUPS
SWA_HEADS = 16
SWA_KV_HEADS = 4
SWA_HEAD_DIM = 64
SWA_REP = SWA_HEADS // SWA_KV_HEADS
SWA_WIDTH = SWA_HEADS * SWA_HEAD_DIM
SWA_KV_WIDTH = SWA_KV_HEADS * SWA_HEAD_DIM
MEM_HEADS = 4
MEM_HEAD_DIM = 256
MEM_WIDTH = MEM_HEADS * MEM_HEAD_DIM
N_BRANCH = 3
BRANCH_WIDTH = 1024
D_FF = 5504
EPS = 1e-6
NEG = -1e30

F32 = jnp.float32
BF16 = jnp.bfloat16

LANES = 128
VMEM_LIMIT_BYTES = 60 * 1024 * 1024

FFN_TM = 1024
FFN_TF = 512
UV_TM = 1024
PROJ_TM = 1024
GATES_TM = 1024
MERGE_TM = 256
SWA_TQ = 1024
MEM_TQ = 2048
W_COLS = 512
CAST_ROWS = 128
MIXER_CAST_ROWS = 16
ALONE_IN_ROWS = 256
ALONE_OUT_ROWS = D_FF // 8

SINGLE = pl.Buffered(1)


def _params(*semantics):
    return pltpu.CompilerParams(dimension_semantics=semantics, vmem_limit_bytes=VMEM_LIMIT_BYTES)


def _rmsnorm_f32(xf, g):
    return xf * lax.rsqrt(jnp.mean(xf * xf, axis=-1, keepdims=True) + EPS) * g


def _dot(a, b):
    return jnp.dot(a, b, preferred_element_type=F32)


def _dot_nt(a, b):
    return lax.dot_general(a, b, (((1,), (1,)), ((), ())), preferred_element_type=F32)


class _SideCast(NamedTuple):
    srcs: tuple
    in_specs: tuple
    out_spec: pl.BlockSpec
    out_shape: jax.ShapeDtypeStruct
    write: Callable
    n_blocks: int


def _side_rows(w, l, rows_blk, step_of):
    _, rows, cols = w.shape
    n_blocks = rows // rows_blk
    assert rows % rows_blk == 0

    def blk(*g):
        return jnp.minimum(step_of(*g), n_blocks - 1)

    def write(step, in_refs, out_ref):
        @pl.when(step < n_blocks)
        def _():
            out_ref[...] = in_refs[0][...].astype(BF16)

    return _SideCast(
        (w,),
        (pl.BlockSpec((None, rows_blk, cols), lambda *g: (l, blk(*g), 0)),),
        pl.BlockSpec((rows_blk, cols), lambda *g: (blk(*g), 0)),
        jax.ShapeDtypeStruct((rows, cols), BF16), write, n_blocks)


def _side_ffn_in(w, l, rows_blk, step_of):
    _, d, _ = w.shape
    assert d % rows_blk == 0 and D_FF % LANES == 0
    n_blocks = d // rows_blk

    def blk(*g):
        return jnp.minimum(step_of(*g), n_blocks - 1)

    def write(step, in_refs, out_ref):
        @pl.when(step < n_blocks)
        def _():
            for half, in_ref in enumerate(in_refs):
                for k in range(D_FF // LANES):
                    dst = (2 * k + half) * LANES
                    out_ref[:, dst:dst + LANES] = in_ref[:, k * LANES:(k + 1) * LANES].astype(BF16)

    half_spec = lambda half: pl.BlockSpec((None, rows_blk, D_FF), lambda *g: (l, blk(*g), half))
    return _SideCast(
        (w, w),
        (half_spec(0), half_spec(1)),
        pl.BlockSpec((rows_blk, 2 * D_FF), lambda *g: (blk(*g), 0)),
        jax.ShapeDtypeStruct((d, 2 * D_FF), BF16), write, n_blocks)


def _hosted_call(body, *, grid, step_of, in_specs, out_specs, out_shape, args, sides, semantics, name,
                 scratch_shapes=()):
    n_in, n_out, k = len(in_specs), len(out_specs), len(sides)
    side_in_counts = [len(s.in_specs) for s in sides]
    n_side_in = sum(side_in_counts)
    n_steps = math.prod(grid)
    assert all(s.n_blocks <= n_steps for s in sides)

    def hosted(*refs):
        ins, side_ins = refs[:n_in], refs[n_in:n_in + n_side_in]
        outs = refs[n_in + n_side_in:n_in + n_side_in + n_out]
        side_outs = refs[n_in + n_side_in + n_out:n_in + n_side_in + n_out + k]
        scratch = refs[n_in + n_side_in + n_out + k:]
        if k:
            step = step_of(*[pl.program_id(a) for a in range(len(grid))])
            first = 0
            for side, count, o_ref in zip(sides, side_in_counts, side_outs):
                side.write(step, side_ins[first:first + count], o_ref)
                first += count
        body(*ins, *outs, *scratch)

    res = pl.pallas_call(
        hosted,
        grid=grid,
        in_specs=list(in_specs) + [spec for s in sides for spec in s.in_specs],
        out_specs=list(out_specs) + [s.out_spec for s in sides],
        out_shape=list(out_shape) + [s.out_shape for s in sides],
        scratch_shapes=list(scratch_shapes),
        compiler_params=_params(*semantics),
        name=name,
    )(*args, *[src for s in sides for src in s.srcs])
    return res[:n_out], res[n_out:]


def _cast_alone(make_side):
    side = make_side(lambda i: i)
    _, (out,) = _hosted_call(lambda: None, grid=(side.n_blocks,), step_of=lambda i: i, in_specs=[],
                             out_specs=[], out_shape=[], args=[], sides=[side],
                             semantics=("arbitrary",), name="cast")
    return out


FFN_STEPS = pl.cdiv(D_FF, FFN_TF)
FFN_LAST_TF = D_FF - (FFN_STEPS - 1) * FFN_TF


def _ffn_step(n, w_ref, wo_ref, o_ref, units):
    pairs = units // LANES
    even = pairs - pairs % 2
    z = [_dot(n, w_ref[:, :2 * even * LANES])] if even else []
    if pairs % 2:
        half_rows = n.shape[0] // 2
        w_last = w_ref[:, 2 * even * LANES:2 * pairs * LANES]
        z.append(jnp.concatenate([_dot(n[:half_rows], w_last), _dot(n[half_rows:], w_last)], axis=0))
    z = jnp.concatenate(z, axis=1)
    hid = []
    for k in range(pairs):
        a = z[:, (2 * k) * LANES:(2 * k + 1) * LANES]
        b = z[:, (2 * k + 1) * LANES:(2 * k + 2) * LANES]
        half_a = 0.5 * a
        hid.append((0.5 * ((half_a * jnp.tanh(half_a) + half_a) * b)).astype(BF16))
    hid = jnp.concatenate(hid, axis=1)
    for c in range(o_ref.shape[1] // FFN_TF):
        cs = slice(c * FFN_TF, (c + 1) * FFN_TF)
        o_ref[:, cs] += _dot(hid, wo_ref[:units, cs])


def _ffn_body(x_ref, g_ref, w_ref, wo_ref, gf_ref, o_ref, n_ref, *, apply_final_norm):
    j = pl.program_id(1)

    @pl.when(j == 0)
    def _():
        x = x_ref[...]
        n_ref[...] = _rmsnorm_f32(x, g_ref[...]).astype(BF16)
        o_ref[...] = x

    @pl.when(j < FFN_STEPS - 1)
    def _():
        _ffn_step(n_ref[...], w_ref, wo_ref, o_ref, FFN_TF)

    @pl.when(j == FFN_STEPS - 1)
    def _():
        _ffn_step(n_ref[...], w_ref, wo_ref, o_ref, FFN_LAST_TF)
        if apply_final_norm:
            o_ref[...] = _rmsnorm_f32(o_ref[...], gf_ref[...])


def _ffn(x, g, w_in_b, w_out_b, g_final, apply_final_norm, make_sides=()):
    t, d = x.shape
    step_of = lambda i, j: i * FFN_STEPS + j
    (out,), side_out = _hosted_call(
        functools.partial(_ffn_body, apply_final_norm=apply_final_norm),
        grid=(t // FFN_TM, FFN_STEPS),
        step_of=step_of,
        in_specs=[
            pl.BlockSpec((FFN_TM, d), lambda i, j: (i, 0)),
            pl.BlockSpec((1, d), lambda i, j: (0, 0)),
            pl.BlockSpec((d, 2 * FFN_TF), lambda i, j: (0, j)),
            pl.BlockSpec((FFN_TF, d), lambda i, j: (j, 0)),
            pl.BlockSpec((1, d), lambda i, j: (0, 0)),
        ],
        out_specs=[pl.BlockSpec((FFN_TM, d), lambda i, j: (i, 0))],
        out_shape=[jax.ShapeDtypeStruct((t, d), F32)],
        args=(x, g.reshape(1, d), w_in_b, w_out_b, g_final.reshape(1, d)),
        sides=[m(step_of) for m in make_sides],
        semantics=("arbitrary", "arbitrary"),
        scratch_shapes=[pltpu.VMEM((FFN_TM, d), BF16)],
        name="ffn",
    )
    return out, side_out


UV_END = 2 * G_WIDTH
QKVM_WIDTH = SWA_WIDTH + 2 * SWA_KV_WIDTH + MEM_WIDTH
QKVM_END = UV_END + QKVM_WIDTH


def _gelu(x):
    return 0.5 * x * (1.0 + lax.erf(x * math.sqrt(0.5)))


def _uv_body(h_ref, g_ref, wu_ref, wv_ref, lng_ref, lnb_ref, ws_ref, bst_ref, n_ref, oa_ref):
    row = lax.broadcasted_iota(jnp.int32, (CHUNK, CHUNK), 0)
    col = lax.broadcasted_iota(jnp.int32, (CHUNK, CHUNK), 1)
    causal = row >= col

    n = _rmsnorm_f32(h_ref[...], g_ref[...]).astype(BF16)
    n_ref[...] = n
    u = _gelu(_dot(n, wu_ref[...]))
    v = _gelu(_dot(n, wv_ref[...]))
    mu = jnp.mean(v, axis=-1, keepdims=True)
    vc = v - mu
    var = jnp.mean(vc * vc, axis=-1, keepdims=True)
    vn = (vc * lax.rsqrt(var + EPS) * lng_ref[...] + lnb_ref[...]).astype(BF16)
    for g in range(G_GROUPS):
        ws = jnp.where(causal, ws_ref[g], 0.0).astype(BF16)
        bias = bst_ref[:, g:g + 1]
        cs = slice(g * G_GROUP_DIM, (g + 1) * G_GROUP_DIM)
        for c in range(UV_TM // CHUNK):
            rs = slice(c * CHUNK, (c + 1) * CHUNK)
            mixed = _dot(ws, vn[rs, cs]) + bias
            oa_ref[rs, cs] = (u[rs, cs] * mixed).astype(BF16)


def _uv(h, g, w_in_b, ln_g, ln_b, w_s, b_s):
    t, d = h.shape
    const = lambda i: (0, 0)
    (n, o_a), _ = _hosted_call(
        _uv_body,
        grid=(t // UV_TM,),
        step_of=lambda i: i,
        in_specs=[
            pl.BlockSpec((UV_TM, d), lambda i: (i, 0)),
            pl.BlockSpec((1, d), const),
            pl.BlockSpec((d, G_WIDTH), lambda i: (0, 0), pipeline_mode=SINGLE),
            pl.BlockSpec((d, G_WIDTH), lambda i: (0, 1), pipeline_mode=SINGLE),
            pl.BlockSpec((1, G_WIDTH), const),
            pl.BlockSpec((1, G_WIDTH), const),
            pl.BlockSpec((G_GROUPS, CHUNK, CHUNK), lambda i: (0, 0, 0)),
            pl.BlockSpec((CHUNK, G_GROUPS), const),
        ],
        out_specs=[
            pl.BlockSpec((UV_TM, d), lambda i: (i, 0)),
            pl.BlockSpec((UV_TM, G_WIDTH), lambda i: (i, 0)),
        ],
        out_shape=[
            jax.ShapeDtypeStruct((t, d), BF16),
            jax.ShapeDtypeStruct((t, G_WIDTH), BF16),
        ],
        args=(h, g.reshape(1, d), w_in_b, w_in_b, ln_g.reshape(1, G_WIDTH), ln_b.reshape(1, G_WIDTH), w_s,
              jnp.transpose(b_s)),
        sides=[],
        semantics=("parallel",),
        name="uv_gmlp",
    )
    return n, o_a


SWA_SCALE = 1.0 / math.sqrt(SWA_HEAD_DIM)
HALF_LANES = LANES // 2
KV_PLACED_WIDTH = SWA_KV_HEADS * LANES
QKVM_BLOCKS = QKVM_WIDTH // W_COLS


def _place_heads(z):
    rows = z.shape[0]
    low_half = lax.broadcasted_iota(jnp.int32, (rows, LANES), 1) < HALF_LANES
    zero = jnp.zeros((rows, LANES), F32)
    low, high = [], []
    for pair in range(SWA_KV_HEADS // 2):
        zg = z[:, pair * LANES:(pair + 1) * LANES]
        swapped = pltpu.roll(zg, HALF_LANES, axis=1)
        low += [jnp.where(low_half, zg, zero), jnp.where(low_half, swapped, zero)]
        high += [jnp.where(low_half, zero, swapped), jnp.where(low_half, zero, zg)]
    return (jnp.concatenate(low, axis=1).astype(BF16), jnp.concatenate(high, axis=1).astype(BF16))


def _qkvm_body(n_ref, wq0_ref, wq1_ref, wkv_ref, wm0_ref, wm1_ref,
               q_ref, klo_ref, khi_ref, vlo_ref, vhi_ref, mq_ref):
    n = n_ref[...]
    q_ref[:, :W_COLS] = (_dot(n, wq0_ref[...]) * SWA_SCALE).astype(BF16)
    q_ref[:, W_COLS:] = (_dot(n, wq1_ref[...]) * SWA_SCALE).astype(BF16)
    kv = _dot(n, wkv_ref[...])
    klo_ref[...], khi_ref[...] = _place_heads(kv[:, :SWA_KV_WIDTH])
    vlo_ref[...], vhi_ref[...] = _place_heads(kv[:, SWA_KV_WIDTH:])
    mq_ref[:, :W_COLS] = _dot(n, wm0_ref[...]).astype(BF16)
    mq_ref[:, W_COLS:] = _dot(n, wm1_ref[...]).astype(BF16)


def _qkvm(n, w_in_b):
    t, d = n.shape
    assert SWA_WIDTH == 2 * W_COLS and 2 * SWA_KV_WIDTH == W_COLS and MEM_WIDTH == 2 * W_COLS
    first = UV_END // W_COLS
    widths = (SWA_WIDTH,) + (KV_PLACED_WIDTH,) * 4 + (MEM_WIDTH,)
    w_specs = [pl.BlockSpec((d, W_COLS), functools.partial(lambda c, i: (0, c), first + c),
                            pipeline_mode=SINGLE) for c in range(QKVM_BLOCKS)]
    outs, _ = _hosted_call(
        _qkvm_body,
        grid=(t // PROJ_TM,),
        step_of=lambda i: i,
        in_specs=[pl.BlockSpec((PROJ_TM, d), lambda i: (i, 0))] + w_specs,
        out_specs=[pl.BlockSpec((PROJ_TM, w_), lambda i: (i, 0)) for w_ in widths],
        out_shape=[jax.ShapeDtypeStruct((t, w_), BF16) for w_ in widths],
        args=(n,) + (w_in_b,) * QKVM_BLOCKS,
        sides=[],
        semantics=("parallel",),
        name="qkvm",
    )
    return outs


GATE_BLOCKS = D_MODEL // W_COLS


def _gates_body(n_ref, *refs):
    w_refs, o_ref = refs[:GATE_BLOCKS], refs[GATE_BLOCKS]
    n = n_ref[...]
    for c, w_ref in enumerate(w_refs):
        z = _dot(n, w_ref[...])
        o_ref[:, c * W_COLS:(c + 1) * W_COLS] = (0.5 * jnp.tanh(0.5 * z) + 0.5).astype(BF16)


def _gates(n, w_in_b, make_sides):
    t, d = n.shape
    tiles = t // GATES_TM
    first = QKVM_END // W_COLS
    step_of = lambda b, i: b * tiles + i
    w_specs = [pl.BlockSpec((d, W_COLS), functools.partial(lambda c, b, i: (0, first + b * GATE_BLOCKS + c), c))
               for c in range(GATE_BLOCKS)]
    (gates,), side_out = _hosted_call(
        _gates_body,
        grid=(N_BRANCH, tiles),
        step_of=step_of,
        in_specs=[pl.BlockSpec((GATES_TM, d), lambda b, i: (i, 0))] + w_specs,
        out_specs=[pl.BlockSpec((None, GATES_TM, d), lambda b, i: (b, i, 0))],
        out_shape=[jax.ShapeDtypeStruct((N_BRANCH, t, d), BF16)],
        args=(n,) + (w_in_b,) * GATE_BLOCKS,
        sides=[m(step_of) for m in make_sides],
        semantics=("arbitrary", "arbitrary"),
        name="gates",
    )
    return gates, side_out


def _mkv_body(m_ref, g_ref, w_ref, mk_ref, mv_ref):
    n = _rmsnorm_f32(m_ref[...], g_ref[...]).astype(BF16)
    z = _dot(n, w_ref[...]).astype(BF16)
    mk_ref[...] = z[:, :MEM_WIDTH]
    mv_ref[...] = z[:, MEM_WIDTH:]


def _mkv(mem, g, w):
    t, d = mem.shape
    return pl.pallas_call(
        _mkv_body,
        grid=(t // MEM_LEN,),
        in_specs=[
            pl.BlockSpec((MEM_LEN, d), lambda i: (i, 0)),
            pl.BlockSpec((1, d), lambda i: (0, 0)),
            pl.BlockSpec((d, 2 * MEM_WIDTH), lambda i: (0, 0), pipeline_mode=SINGLE),
        ],
        out_specs=[pl.BlockSpec((MEM_LEN, MEM_WIDTH), lambda i: (i, 0))] * 2,
        out_shape=[jax.ShapeDtypeStruct((t, MEM_WIDTH), BF16)] * 2,
        compiler_params=_params("parallel"),
        name="mem_kv",
    )(mem, g.reshape(1, d), w)


SWA_SLOPES = tuple(2.0 ** (-8.0 * (h + 1) / SWA_HEADS) for h in range(SWA_HEADS))


def _swa_body(sink_ref, q_ref, klo_ref, khi_ref, vlo_ref, vhi_ref,
              klo_p_ref, khi_p_ref, vlo_p_ref, vhi_p_ref, o_ref):
    w = WINDOW
    t = pl.program_id(1)
    i2 = lax.broadcasted_iota(jnp.int32, (2 * w, w), 0) & (w - 1)
    j2 = lax.broadcasted_iota(jnp.int32, (2 * w, w), 1)
    from_prev = j2 > i2
    dist = ((i2 - j2) & (w - 1)).astype(F32)
    no_prev = j2 > jnp.maximum(i2, jnp.where(t == 0, -1, w))
    low_half = j2 < HALF_LANES
    zero = jnp.zeros((2 * w, w), F32)
    ones_cols = (jnp.where(low_half, 1.0, 0.0).astype(BF16), jnp.where(low_half, 0.0, 1.0).astype(BF16))

    def rows2(top, bottom):
        return jnp.concatenate([jnp.full((w, w), top, F32), jnp.full((w, w), bottom, F32)], axis=0)

    for blk in range(SWA_TQ // w):
        rs = slice(blk * w, (blk + 1) * w)

        def band(cur_ref, prev_ref, cols):
            prev = prev_ref[:, cols] if blk == 0 else cur_ref[(blk - 1) * w:blk * w, cols]
            return jnp.concatenate([prev, cur_ref[rs, cols]], axis=0)

        for kvh in range(SWA_KV_HEADS):
            cols = slice(kvh * LANES, (kvh + 1) * LANES)
            q2 = jnp.concatenate([q_ref[rs, (2 * kvh) * LANES:(2 * kvh + 1) * LANES],
                                  q_ref[rs, (2 * kvh + 1) * LANES:(2 * kvh + 2) * LANES]], axis=0)
            keys = jnp.concatenate([band(klo_ref, klo_p_ref, cols), band(khi_ref, khi_p_ref, cols)], axis=0)
            s = _dot_nt(q2, keys)
            acc = None
            sink_terms = []
            for par, (v_ref, v_p_ref) in enumerate(((vlo_ref, vlo_p_ref), (vhi_ref, vhi_p_ref))):
                heads = (kvh * SWA_REP + par, kvh * SWA_REP + 2 + par)
                base = par * 2 * w
                sc = jnp.where(from_prev, s[:, base:base + w], s[:, base + w:base + 2 * w])
                sc = sc - rows2(SWA_SLOPES[heads[0]], SWA_SLOPES[heads[1]]) * dist
                if blk == 0:
                    sc = jnp.where(no_prev, NEG, sc)
                sink = rows2(sink_ref[heads[0]], sink_ref[heads[1]])
                m = jnp.maximum(jnp.broadcast_to(jnp.max(sc, axis=-1, keepdims=True), (2 * w, w)), sink)
                p = jnp.exp(sc - m)
                pcat = jnp.concatenate([jnp.where(from_prev, p, zero).astype(BF16),
                                        jnp.where(from_prev, zero, p).astype(BF16)], axis=1)
                vext = jnp.concatenate([band(v_ref, v_p_ref, cols), ones_cols[par]], axis=1)
                part = _dot(pcat, vext)
                acc = part if acc is None else acc + part
                sink_terms.append(jnp.exp(sink - m))
            denom = acc[:, w:] + jnp.where(low_half, sink_terms[0], sink_terms[1])
            out = (acc[:, :w] / denom).astype(BF16)
            o_ref[rs, (2 * kvh) * LANES:(2 * kvh + 1) * LANES] = out[:w]
            o_ref[rs, (2 * kvh + 1) * LANES:(2 * kvh + 2) * LANES] = out[w:]


def _swa(q, k_lo, k_hi, v_lo, v_hi, sinks, batch, seq):
    t = q.shape[0]
    steps = seq // SWA_TQ
    blocks_per_step = SWA_TQ // WINDOW
    blocks_per_seq = seq // WINDOW

    def cur(b, s):
        return (b * steps + s, 0)

    def prev(b, s):
        return (b * blocks_per_seq + jnp.maximum(s * blocks_per_step - 1, 0), 0)

    kv_cur = pl.BlockSpec((SWA_TQ, KV_PLACED_WIDTH), cur)
    kv_prev = pl.BlockSpec((WINDOW, KV_PLACED_WIDTH), prev)
    return pl.pallas_call(
        _swa_body,
        grid=(batch, steps),
        in_specs=[pl.BlockSpec(memory_space=pltpu.SMEM), pl.BlockSpec((SWA_TQ, SWA_WIDTH), cur)]
        + [kv_cur] * 4 + [kv_prev] * 4,
        out_specs=pl.BlockSpec((SWA_TQ, SWA_WIDTH), cur),
        out_shape=jax.ShapeDtypeStruct((t, SWA_WIDTH), BF16),
        compiler_params=_params("parallel", "parallel"),
        name="swa",
    )(sinks, q, k_lo, k_hi, v_lo, v_hi, k_lo, k_hi, v_lo, v_hi)


MEM_SCALE = 1.0 / math.sqrt(MEM_HEAD_DIM)


def _memattn_body(q_ref, mk_ref, mv_ref, o_ref):
    for h in range(MEM_HEADS):
        hs = slice(h * MEM_HEAD_DIM, (h + 1) * MEM_HEAD_DIM)
        s = _dot_nt(q_ref[:, hs], mk_ref[:, hs]) * MEM_SCALE
        m = jnp.max(s, axis=-1, keepdims=True)
        p = jnp.exp(s - m)
        probs = (p / jnp.sum(p, axis=-1, keepdims=True)).astype(BF16)
        o_ref[:, hs] = _dot(probs, mv_ref[:, hs]).astype(BF16)


def _memattn(mq, mk, mv, batch, seq):
    t = mq.shape[0]
    steps = seq // MEM_TQ
    return pl.pallas_call(
        _memattn_body,
        grid=(batch, steps),
        in_specs=[
            pl.BlockSpec((MEM_TQ, MEM_WIDTH), lambda b, s: (b * steps + s, 0)),
            pl.BlockSpec((MEM_LEN, MEM_WIDTH), lambda b, s: (b, 0)),
            pl.BlockSpec((MEM_LEN, MEM_WIDTH), lambda b, s: (b, 0)),
        ],
        out_specs=pl.BlockSpec((MEM_TQ, MEM_WIDTH), lambda b, s: (b * steps + s, 0)),
        out_shape=jax.ShapeDtypeStruct((t, MEM_WIDTH), BF16),
        compiler_params=_params("parallel", "parallel"),
        name="mem_attn",
    )(mq, mk, mv)


def _merge_body(h_ref, gt_ref, oa_ref, ob_ref, oc_ref, wbr_ref, wo_ref, o_ref):
    y = gt_ref[0].astype(F32) * _dot(oa_ref[...], wbr_ref[0])
    y += gt_ref[1].astype(F32) * _dot(ob_ref[...], wbr_ref[1])
    y += gt_ref[2].astype(F32) * _dot(oc_ref[...], wbr_ref[2])
    o_ref[...] = h_ref[...] + _dot(y.astype(BF16), wo_ref[...])


def _merge(h, gates, o_a, o_b, o_c, w_branch, w_out, make_sides):
    t, d = h.shape
    row = lambda i: (i, 0)
    (out,), side_out = _hosted_call(
        _merge_body,
        grid=(t // MERGE_TM,),
        step_of=lambda i: i,
        in_specs=[
            pl.BlockSpec((MERGE_TM, d), row),
            pl.BlockSpec((N_BRANCH, MERGE_TM, d), lambda i: (0, i, 0)),
            pl.BlockSpec((MERGE_TM, BRANCH_WIDTH), row),
            pl.BlockSpec((MERGE_TM, BRANCH_WIDTH), row),
            pl.BlockSpec((MERGE_TM, BRANCH_WIDTH), row),
            pl.BlockSpec((N_BRANCH, BRANCH_WIDTH, d), lambda i: (0, 0, 0), pipeline_mode=SINGLE),
            pl.BlockSpec((d, d), lambda i: (0, 0), pipeline_mode=SINGLE),
        ],
        out_specs=[pl.BlockSpec((MERGE_TM, d), row)],
        out_shape=[jax.ShapeDtypeStruct((t, d), F32)],
        args=(h, gates, o_a, o_b, o_c, w_branch, w_out),
        sides=[m(lambda i: i) for m in make_sides],
        semantics=("arbitrary",),
        name="merge",
    )
    return out, side_out


def kernel(x, mem, g_ffn1, w_ffn1_in, w_ffn1_out, g_mix, w_in, gmlp_ln_g, gmlp_ln_b, w_s, b_s, swa_sinks, g_mem, w_mem_kv, w_branch, w_out, g_ffn2, w_ffn2_in, w_ffn2_out, g_final):
    batch, seq, d = x.shape
    depth = w_in.shape[0]
    xt = x.reshape(batch * seq, d)
    memt = mem.reshape(batch * MEM_LEN, d)
    w_branch_rows = w_branch.reshape(depth, N_BRANCH * BRANCH_WIDTH, d)

    def mixer_casts(l):
        return [functools.partial(_side_rows, w_in, l, MIXER_CAST_ROWS),
                functools.partial(_side_rows, w_mem_kv, l, MIXER_CAST_ROWS),
                functools.partial(_side_rows, w_out, l, MIXER_CAST_ROWS),
                functools.partial(_side_rows, w_branch_rows, l, 2 * MIXER_CAST_ROWS)]

    def ffn_casts(w_in_ffn, w_out_ffn, l, rows_in=CAST_ROWS // 2, rows_out=CAST_ROWS):
        return [functools.partial(_side_ffn_in, w_in_ffn, l, rows_in),
                functools.partial(_side_rows, w_out_ffn, l, rows_out)]

    wab1, wo1 = [_cast_alone(m) for m in ffn_casts(w_ffn1_in, w_ffn1_out, 0, ALONE_IN_ROWS, ALONE_OUT_ROWS)]

    for l in range(depth):
        more = l + 1 < depth
        h, (w_in_b, w_mkv, w_o, w_br) = _ffn(xt, g_ffn1[l], wab1, wo1, g_final, False, mixer_casts(l))
        n, o_a = _uv(h, g_mix[l], w_in_b, gmlp_ln_g[l], gmlp_ln_b[l], w_s[l], b_s[l])
        q, k_lo, k_hi, v_lo, v_hi, mq = _qkvm(n, w_in_b)
        gates, (wab2, wo2) = _gates(n, w_in_b, ffn_casts(w_ffn2_in, w_ffn2_out, l))
        mk, mv = _mkv(memt, g_mem[l], w_mkv)
        o_b = _swa(q, k_lo, k_hi, v_lo, v_hi, swa_sinks[l], batch, seq)
        o_c = _memattn(mq, mk, mv, batch, seq)
        h, merge_side = _merge(h, gates, o_a, o_b, o_c, w_br.reshape(N_BRANCH, BRANCH_WIDTH, d), w_o,
                               ffn_casts(w_ffn1_in, w_ffn1_out, l + 1) if more else [])
        xt, _ = _ffn(h, g_ffn2[l], wab2, wo2, g_final, not more)
        if more:
            wab1, wo1 = merge_side
    return xt.reshape(batch, seq, d)
```

```python
import functools
import math
from typing import Callable, NamedTuple

import jax
import jax.numpy as jnp
from jax import lax
from jax.experimental import pallas as pl
from jax.experimental.pallas import tpu as pltpu

D_MODEL = 2048
MEM_LEN = 256
CHUNK = 128
WINDOW = 128
G_GROUPS = 4
G_WIDTH = 1024
G_GROUP_DIM = G_WIDTH // G_GROUPS
SWA_HEADS = 16
SWA_KV_HEADS = 4
SWA_HEAD_DIM = 64
SWA_REP = SWA_HEADS // SWA_KV_HEADS
SWA_WIDTH = SWA_HEADS * SWA_HEAD_DIM
SWA_KV_WIDTH = SWA_KV_HEADS * SWA_HEAD_DIM
MEM_HEADS = 4
MEM_HEAD_DIM = 256
MEM_WIDTH = MEM_HEADS * MEM_HEAD_DIM
N_BRANCH = 3
BRANCH_WIDTH = 1024
D_FF = 5504
EPS = 1e-6
NEG = -1e30

F32 = jnp.float32
BF16 = jnp.bfloat16

LANES = 128
VMEM_LIMIT_BYTES = 60 * 1024 * 1024

FFN_TM = 1024
FFN_TF = 512
UV_TM = 1024
PROJ_TM = 1024
GATES_TM = 1024
MERGE_TM = 256
SWA_TQ = 1024
MEM_TQ = 2048
W_COLS = 512
CAST_ROWS = 128
MIXER_CAST_ROWS = 16
ALONE_IN_ROWS = 256
ALONE_OUT_ROWS = D_FF // 8

SINGLE = pl.Buffered(1)


def _params(*semantics):
    return pltpu.CompilerParams(dimension_semantics=semantics, vmem_limit_bytes=VMEM_LIMIT_BYTES)


def _rmsnorm_f32(xf, g):
    return xf * lax.rsqrt(jnp.mean(xf * xf, axis=-1, keepdims=True) + EPS) * g


def _dot(a, b):
    return jnp.dot(a, b, preferred_element_type=F32)


def _dot_nt(a, b):
    return lax.dot_general(a, b, (((1,), (1,)), ((), ())), preferred_element_type=F32)


class _SideCast(NamedTuple):
    srcs: tuple
    in_specs: tuple
    out_spec: pl.BlockSpec
    out_shape: jax.ShapeDtypeStruct
    write: Callable
    n_blocks: int


def _side_rows(w, l, rows_blk, step_of):
    _, rows, cols = w.shape
    n_blocks = rows // rows_blk
    assert rows % rows_blk == 0

    def blk(*g):
        return jnp.minimum(step_of(*g), n_blocks - 1)

    def write(step, in_refs, out_ref):
        @pl.when(step < n_blocks)
        def _():
            out_ref[...] = in_refs[0][...].astype(BF16)

    return _SideCast(
        (w,),
        (pl.BlockSpec((None, rows_blk, cols), lambda *g: (l, blk(*g), 0)),),
        pl.BlockSpec((rows_blk, cols), lambda *g: (blk(*g), 0)),
        jax.ShapeDtypeStruct((rows, cols), BF16), write, n_blocks)


def _side_ffn_in(w, l, rows_blk, step_of):
    _, d, _ = w.shape
    assert d % rows_blk == 0 and D_FF % LANES == 0
    n_blocks = d // rows_blk

    def blk(*g):
        return jnp.minimum(step_of(*g), n_blocks - 1)

    def write(step, in_refs, out_ref):
        @pl.when(step < n_blocks)
        def _():
            for half, in_ref in enumerate(in_refs):
                for k in range(D_FF // LANES):
                    dst = (2 * k + half) * LANES
                    out_ref[:, dst:dst + LANES] = in_ref[:, k * LANES:(k + 1) * LANES].astype(BF16)

    half_spec = lambda half: pl.BlockSpec((None, rows_blk, D_FF), lambda *g: (l, blk(*g), half))
    return _SideCast(
        (w, w),
        (half_spec(0), half_spec(1)),
        pl.BlockSpec((rows_blk, 2 * D_FF), lambda *g: (blk(*g), 0)),
        jax.ShapeDtypeStruct((d, 2 * D_FF), BF16), write, n_blocks)


def _hosted_call(body, *, grid, step_of, in_specs, out_specs, out_shape, args, sides, semantics, name,
                 scratch_shapes=()):
    n_in, n_out, k = len(in_specs), len(out_specs), len(sides)
    side_in_counts = [len(s.in_specs) for s in sides]
    n_side_in = sum(side_in_counts)
    n_steps = math.prod(grid)
    assert all(s.n_blocks <= n_steps for s in sides)

    def hosted(*refs):
        ins, side_ins = refs[:n_in], refs[n_in:n_in + n_side_in]
        outs = refs[n_in + n_side_in:n_in + n_side_in + n_out]
        side_outs = refs[n_in + n_side_in + n_out:n_in + n_side_in + n_out + k]
        scratch = refs[n_in + n_side_in + n_out + k:]
        if k:
            step = step_of(*[pl.program_id(a) for a in range(len(grid))])
            first = 0
            for side, count, o_ref in zip(sides, side_in_counts, side_outs):
                side.write(step, side_ins[first:first + count], o_ref)
                first += count
        body(*ins, *outs, *scratch)

    res = pl.pallas_call(
        hosted,
        grid=grid,
        in_specs=list(in_specs) + [spec for s in sides for spec in s.in_specs],
        out_specs=list(out_specs) + [s.out_spec for s in sides],
        out_shape=list(out_shape) + [s.out_shape for s in sides],
        scratch_shapes=list(scratch_shapes),
        compiler_params=_params(*semantics),
        name=name,
    )(*args, *[src for s in sides for src in s.srcs])
    return res[:n_out], res[n_out:]


def _cast_alone(make_side):
    side = make_side(lambda i: i)
    _, (out,) = _hosted_call(lambda: None, grid=(side.n_blocks,), step_of=lambda i: i, in_specs=[],
                             out_specs=[], out_shape=[], args=[], sides=[side],
                             semantics=("arbitrary",), name="cast")
    return out


FFN_STEPS = pl.cdiv(D_FF, FFN_TF)
FFN_LAST_TF = D_FF - (FFN_STEPS - 1) * FFN_TF


def _ffn_step(n, w_ref, wo_ref, o_ref, units, acc_ref=None):
    acc_ref = o_ref if acc_ref is None else acc_ref
    pairs = units // LANES
    even = pairs - pairs % 2
    z = [_dot(n, w_ref[:, :2 * even * LANES])] if even else []
    if pairs % 2:
        half_rows = n.shape[0] // 2
        w_last = w_ref[:, 2 * even * LANES:2 * pairs * LANES]
        z.append(jnp.concatenate([_dot(n[:half_rows], w_last), _dot(n[half_rows:], w_last)], axis=0))
    z = jnp.concatenate(z, axis=1)
    hid = []
    for k in range(pairs):
        a = z[:, (2 * k) * LANES:(2 * k + 1) * LANES]
        b = z[:, (2 * k + 1) * LANES:(2 * k + 2) * LANES]
        half_a = 0.5 * a
        hid.append((0.5 * ((half_a * jnp.tanh(half_a) + half_a) * b)).astype(BF16))
    hid = jnp.concatenate(hid, axis=1)
    for c in range(o_ref.shape[1] // FFN_TF):
        cs = slice(c * FFN_TF, (c + 1) * FFN_TF)
        o_ref[:, cs] = acc_ref[:, cs] + _dot(hid, wo_ref[:units, cs])


def _ffn_body(x_ref, g_ref, w_ref, wo_ref, gf_ref, o_ref, n_ref, *, apply_final_norm):
    j = pl.program_id(1)
    assert FFN_STEPS > 2

    @pl.when(j == 0)
    def _():
        n = _rmsnorm_f32(x_ref[...], g_ref[...]).astype(BF16)
        n_ref[...] = n
        _ffn_step(n, w_ref, wo_ref, o_ref, FFN_TF, acc_ref=x_ref)

    @pl.when((j > 0) & (j < FFN_STEPS - 1))
    def _():
        _ffn_step(n_ref[...], w_ref, wo_ref, o_ref, FFN_TF)

    @pl.when(j == FFN_STEPS - 1)
    def _():
        _ffn_step(n_ref[...], w_ref, wo_ref, o_ref, FFN_LAST_TF)
        if apply_final_norm:
            o_ref[...] = _rmsnorm_f32(o_ref[...], gf_ref[...])


def _ffn(x, g, w_in_b, w_out_b, g_final, apply_final_norm, make_sides=()):
    t, d = x.shape
    step_of = lambda i, j: i * FFN_STEPS + j
    (out,), side_out = _hosted_call(
        functools.partial(_ffn_body, apply_final_norm=apply_final_norm),
        grid=(t // FFN_TM, FFN_STEPS),
        step_of=step_of,
        in_specs=[
            pl.BlockSpec((FFN_TM, d), lambda i, j: (i, 0)),
            pl.BlockSpec((1, d), lambda i, j: (0, 0)),
            pl.BlockSpec((d, 2 * FFN_TF), lambda i, j: (0, j)),
            pl.BlockSpec((FFN_TF, d), lambda i, j: (j, 0)),
            pl.BlockSpec((1, d), lambda i, j: (0, 0)),
        ],
        out_specs=[pl.BlockSpec((FFN_TM, d), lambda i, j: (i, 0))],
        out_shape=[jax.ShapeDtypeStruct((t, d), F32)],
        args=(x, g.reshape(1, d), w_in_b, w_out_b, g_final.reshape(1, d)),
        sides=[m(step_of) for m in make_sides],
        semantics=("arbitrary", "arbitrary"),
        scratch_shapes=[pltpu.VMEM((FFN_TM, d), BF16)],
        name="ffn",
    )
    return out, side_out


UV_END = 2 * G_WIDTH
QKVM_WIDTH = SWA_WIDTH + 2 * SWA_KV_WIDTH + MEM_WIDTH
QKVM_END = UV_END + QKVM_WIDTH


def _gelu(x):
    return 0.5 * x * (1.0 + lax.erf(x * math.sqrt(0.5)))


def _uv_body(h_ref, g_ref, wu_ref, wv_ref, lng_ref, lnb_ref, ws_ref, bst_ref, n_ref, oa_ref):
    row = lax.broadcasted_iota(jnp.int32, (CHUNK, CHUNK), 0)
    col = lax.broadcasted_iota(jnp.int32, (CHUNK, CHUNK), 1)
    causal = row >= col

    n = _rmsnorm_f32(h_ref[...], g_ref[...]).astype(BF16)
    n_ref[...] = n
    u = _gelu(_dot(n, wu_ref[...]))
    v = _gelu(_dot(n, wv_ref[...]))
    mu = jnp.mean(v, axis=-1, keepdims=True)
    vc = v - mu
    var = jnp.mean(vc * vc, axis=-1, keepdims=True)
    vn = (vc * lax.rsqrt(var + EPS) * lng_ref[...] + lnb_ref[...]).astype(BF16)
    for g in range(G_GROUPS):
        ws = jnp.where(causal, ws_ref[g], 0.0).astype(BF16)
        bias = bst_ref[:, g:g + 1]
        cs = slice(g * G_GROUP_DIM, (g + 1) * G_GROUP_DIM)
        for c in range(UV_TM // CHUNK):
            rs = slice(c * CHUNK, (c + 1) * CHUNK)
            mixed = _dot(ws, vn[rs, cs]) + bias
            oa_ref[rs, cs] = (u[rs, cs] * mixed).astype(BF16)


def _uv(h, g, w_in_b, ln_g, ln_b, w_s, b_s):
    t, d = h.shape
    const = lambda i: (0, 0)
    (n, o_a), _ = _hosted_call(
        _uv_body,
        grid=(t // UV_TM,),
        step_of=lambda i: i,
        in_specs=[
            pl.BlockSpec((UV_TM, d), lambda i: (i, 0)),
            pl.BlockSpec((1, d), const),
            pl.BlockSpec((d, G_WIDTH), lambda i: (0, 0), pipeline_mode=SINGLE),
            pl.BlockSpec((d, G_WIDTH), lambda i: (0, 1), pipeline_mode=SINGLE),
            pl.BlockSpec((1, G_WIDTH), const),
            pl.BlockSpec((1, G_WIDTH), const),
            pl.BlockSpec((G_GROUPS, CHUNK, CHUNK), lambda i: (0, 0, 0)),
            pl.BlockSpec((CHUNK, G_GROUPS), const),
        ],
        out_specs=[
            pl.BlockSpec((UV_TM, d), lambda i: (i, 0)),
            pl.BlockSpec((UV_TM, G_WIDTH), lambda i: (i, 0)),
        ],
        out_shape=[
            jax.ShapeDtypeStruct((t, d), BF16),
            jax.ShapeDtypeStruct((t, G_WIDTH), BF16),
        ],
        args=(h, g.reshape(1, d), w_in_b, w_in_b, ln_g.reshape(1, G_WIDTH), ln_b.reshape(1, G_WIDTH), w_s,
              jnp.transpose(b_s)),
        sides=[],
        semantics=("parallel",),
        name="uv_gmlp",
    )
    return n, o_a


SWA_SCALE = 1.0 / math.sqrt(SWA_HEAD_DIM)
HALF_LANES = LANES // 2
KV_PLACED_WIDTH = SWA_KV_HEADS * LANES
QKVM_BLOCKS = QKVM_WIDTH // W_COLS


def _place_heads(z):
    rows = z.shape[0]
    low_half = lax.broadcasted_iota(jnp.int32, (rows, LANES), 1) < HALF_LANES
    zero = jnp.zeros((rows, LANES), F32)
    low, high = [], []
    for pair in range(SWA_KV_HEADS // 2):
        zg = z[:, pair * LANES:(pair + 1) * LANES]
        swapped = pltpu.roll(zg, HALF_LANES, axis=1)
        low += [jnp.where(low_half, zg, zero), jnp.where(low_half, swapped, zero)]
        high += [jnp.where(low_half, zero, swapped), jnp.where(low_half, zero, zg)]
    return (jnp.concatenate(low, axis=1).astype(BF16), jnp.concatenate(high, axis=1).astype(BF16))


def _qkvm_body(n_ref, wq0_ref, wq1_ref, wkv_ref, wm0_ref, wm1_ref,
               q_ref, klo_ref, khi_ref, vlo_ref, vhi_ref, mq_ref):
    n = n_ref[...]
    q_ref[:, :W_COLS] = (_dot(n, wq0_ref[...]) * SWA_SCALE).astype(BF16)
    q_ref[:, W_COLS:] = (_dot(n, wq1_ref[...]) * SWA_SCALE).astype(BF16)
    kv = _dot(n, wkv_ref[...])
    klo_ref[...], khi_ref[...] = _place_heads(kv[:, :SWA_KV_WIDTH])
    vlo_ref[...], vhi_ref[...] = _place_heads(kv[:, SWA_KV_WIDTH:])
    mq_ref[:, :W_COLS] = _dot(n, wm0_ref[...]).astype(BF16)
    mq_ref[:, W_COLS:] = _dot(n, wm1_ref[...]).astype(BF16)


def _qkvm(n, w_in_b):
    t, d = n.shape
    assert SWA_WIDTH == 2 * W_COLS and 2 * SWA_KV_WIDTH == W_COLS and MEM_WIDTH == 2 * W_COLS
    first = UV_END // W_COLS
    widths = (SWA_WIDTH,) + (KV_PLACED_WIDTH,) * 4 + (MEM_WIDTH,)
    w_specs = [pl.BlockSpec((d, W_COLS), functools.partial(lambda c, i: (0, c), first + c),
                            pipeline_mode=SINGLE) for c in range(QKVM_BLOCKS)]
    outs, _ = _hosted_call(
        _qkvm_body,
        grid=(t // PROJ_TM,),
        step_of=lambda i: i,
        in_specs=[pl.BlockSpec((PROJ_TM, d), lambda i: (i, 0))] + w_specs,
        out_specs=[pl.BlockSpec((PROJ_TM, w_), lambda i: (i, 0)) for w_ in widths],
        out_shape=[jax.ShapeDtypeStruct((t, w_), BF16) for w_ in widths],
        args=(n,) + (w_in_b,) * QKVM_BLOCKS,
        sides=[],
        semantics=("parallel",),
        name="qkvm",
    )
    return outs


GATE_BLOCKS = D_MODEL // W_COLS


def _gates_body(n_ref, *refs):
    w_refs, o_ref = refs[:GATE_BLOCKS], refs[GATE_BLOCKS]
    n = n_ref[...]
    for c, w_ref in enumerate(w_refs):
        z = _dot(n, w_ref[...])
        o_ref[:, c * W_COLS:(c + 1) * W_COLS] = (0.5 * jnp.tanh(0.5 * z) + 0.5).astype(BF16)


def _gates(n, w_in_b, make_sides):
    t, d = n.shape
    tiles = t // GATES_TM
    first = QKVM_END // W_COLS
    step_of = lambda b, i: b * tiles + i
    w_specs = [pl.BlockSpec((d, W_COLS), functools.partial(lambda c, b, i: (0, first + b * GATE_BLOCKS + c), c))
               for c in range(GATE_BLOCKS)]
    (gates,), side_out = _hosted_call(
        _gates_body,
        grid=(N_BRANCH, tiles),
        step_of=step_of,
        in_specs=[pl.BlockSpec((GATES_TM, d), lambda b, i: (i, 0))] + w_specs,
        out_specs=[pl.BlockSpec((None, GATES_TM, d), lambda b, i: (b, i, 0))],
        out_shape=[jax.ShapeDtypeStruct((N_BRANCH, t, d), BF16)],
        args=(n,) + (w_in_b,) * GATE_BLOCKS,
        sides=[m(step_of) for m in make_sides],
        semantics=("arbitrary", "arbitrary"),
        name="gates",
    )
    return gates, side_out


def _mkv_body(m_ref, g_ref, w_ref, mk_ref, mv_ref):
    n = _rmsnorm_f32(m_ref[...], g_ref[...]).astype(BF16)
    z = _dot(n, w_ref[...]).astype(BF16)
    mk_ref[...] = z[:, :MEM_WIDTH]
    mv_ref[...] = z[:, MEM_WIDTH:]


def _mkv(mem, g, w):
    t, d = mem.shape
    return pl.pallas_call(
        _mkv_body,
        grid=(t // MEM_LEN,),
        in_specs=[
            pl.BlockSpec((MEM_LEN, d), lambda i: (i, 0)),
            pl.BlockSpec((1, d), lambda i: (0, 0)),
            pl.BlockSpec((d, 2 * MEM_WIDTH), lambda i: (0, 0), pipeline_mode=SINGLE),
        ],
        out_specs=[pl.BlockSpec((MEM_LEN, MEM_WIDTH), lambda i: (i, 0))] * 2,
        out_shape=[jax.ShapeDtypeStruct((t, MEM_WIDTH), BF16)] * 2,
        compiler_params=_params("parallel"),
        name="mem_kv",
    )(mem, g.reshape(1, d), w)


SWA_SLOPES = tuple(2.0 ** (-8.0 * (h + 1) / SWA_HEADS) for h in range(SWA_HEADS))


def _swa_body(sink_ref, q_ref, klo_ref, khi_ref, vlo_ref, vhi_ref,
              klo_p_ref, khi_p_ref, vlo_p_ref, vhi_p_ref, o_ref):
    w = WINDOW
    t = pl.program_id(1)
    i2 = lax.broadcasted_iota(jnp.int32, (2 * w, w), 0) & (w - 1)
    j2 = lax.broadcasted_iota(jnp.int32, (2 * w, w), 1)
    from_prev = j2 > i2
    dist = ((i2 - j2) & (w - 1)).astype(F32)
    no_prev = j2 > jnp.maximum(i2, jnp.where(t == 0, -1, w))
    low_half = j2 < HALF_LANES
    zero = jnp.zeros((2 * w, w), F32)
    ones_cols = (jnp.where(low_half, 1.0, 0.0).astype(BF16), jnp.where(low_half, 0.0, 1.0).astype(BF16))

    def rows2(top, bottom):
        return jnp.concatenate([jnp.full((w, w), top, F32), jnp.full((w, w), bottom, F32)], axis=0)

    for blk in range(SWA_TQ // w):
        rs = slice(blk * w, (blk + 1) * w)

        def band(cur_ref, prev_ref, cols):
            prev = prev_ref[:, cols] if blk == 0 else cur_ref[(blk - 1) * w:blk * w, cols]
            return jnp.concatenate([prev, cur_ref[rs, cols]], axis=0)

        for kvh in range(SWA_KV_HEADS):
            cols = slice(kvh * LANES, (kvh + 1) * LANES)
            q2 = jnp.concatenate([q_ref[rs, (2 * kvh) * LANES:(2 * kvh + 1) * LANES],
                                  q_ref[rs, (2 * kvh + 1) * LANES:(2 * kvh + 2) * LANES]], axis=0)
            keys = jnp.concatenate([band(klo_ref, klo_p_ref, cols), band(khi_ref, khi_p_ref, cols)], axis=0)
            s = _dot_nt(q2, keys)
            acc = None
            sink_terms = []
            for par, (v_ref, v_p_ref) in enumerate(((vlo_ref, vlo_p_ref), (vhi_ref, vhi_p_ref))):
                heads = (kvh * SWA_REP + par, kvh * SWA_REP + 2 + par)
                base = par * 2 * w
                sc = jnp.where(from_prev, s[:, base:base + w], s[:, base + w:base + 2 * w])
                sc = sc - rows2(SWA_SLOPES[heads[0]], SWA_SLOPES[heads[1]]) * dist
                if blk == 0:
                    sc = jnp.where(no_prev, NEG, sc)
                sink = rows2(sink_ref[heads[0]], sink_ref[heads[1]])
                m = jnp.maximum(jnp.broadcast_to(jnp.max(sc, axis=-1, keepdims=True), (2 * w, w)), sink)
                p = jnp.exp(sc - m)
                pcat = jnp.concatenate([jnp.where(from_prev, p, zero).astype(BF16),
                                        jnp.where(from_prev, zero, p).astype(BF16)], axis=1)
                vext = jnp.concatenate([band(v_ref, v_p_ref, cols), ones_cols[par]], axis=1)
                part = _dot(pcat, vext)
                acc = part if acc is None else acc + part
                sink_terms.append(jnp.exp(sink - m))
            denom = acc[:, w:] + jnp.where(low_half, sink_terms[0], sink_terms[1])
            out = (acc[:, :w] / denom).astype(BF16)
            o_ref[rs, (2 * kvh) * LANES:(2 * kvh + 1) * LANES] = out[:w]
            o_ref[rs, (2 * kvh + 1) * LANES:(2 * kvh + 2) * LANES] = out[w:]


def _swa(q, k_lo, k_hi, v_lo, v_hi, sinks, batch, seq):
    t = q.shape[0]
    steps = seq // SWA_TQ
    blocks_per_step = SWA_TQ // WINDOW
    blocks_per_seq = seq // WINDOW

    def cur(b, s):
        return (b * steps + s, 0)

    def prev(b, s):
        return (b * blocks_per_seq + jnp.maximum(s * blocks_per_step - 1, 0), 0)

    kv_cur = pl.BlockSpec((SWA_TQ, KV_PLACED_WIDTH), cur)
    kv_prev = pl.BlockSpec((WINDOW, KV_PLACED_WIDTH), prev)
    return pl.pallas_call(
        _swa_body,
        grid=(batch, steps),
        in_specs=[pl.BlockSpec(memory_space=pltpu.SMEM), pl.BlockSpec((SWA_TQ, SWA_WIDTH), cur)]
        + [kv_cur] * 4 + [kv_prev] * 4,
        out_specs=pl.BlockSpec((SWA_TQ, SWA_WIDTH), cur),
        out_shape=jax.ShapeDtypeStruct((t, SWA_WIDTH), BF16),
        compiler_params=_params("parallel", "parallel"),
        name="swa",
    )(sinks, q, k_lo, k_hi, v_lo, v_hi, k_lo, k_hi, v_lo, v_hi)


MEM_SCALE = 1.0 / math.sqrt(MEM_HEAD_DIM)


def _memattn_body(q_ref, mk_ref, mv_ref, o_ref):
    for h in range(MEM_HEADS):
        hs = slice(h * MEM_HEAD_DIM, (h + 1) * MEM_HEAD_DIM)
        s = _dot_nt(q_ref[:, hs], mk_ref[:, hs]) * MEM_SCALE
        m = jnp.max(s, axis=-1, keepdims=True)
        p = jnp.exp(s - m)
        probs = (p / jnp.sum(p, axis=-1, keepdims=True)).astype(BF16)
        o_ref[:, hs] = _dot(probs, mv_ref[:, hs]).astype(BF16)


def _memattn(mq, mk, mv, batch, seq):
    t = mq.shape[0]
    steps = seq // MEM_TQ
    return pl.pallas_call(
        _memattn_body,
        grid=(batch, steps),
        in_specs=[
            pl.BlockSpec((MEM_TQ, MEM_WIDTH), lambda b, s: (b * steps + s, 0)),
            pl.BlockSpec((MEM_LEN, MEM_WIDTH), lambda b, s: (b, 0)),
            pl.BlockSpec((MEM_LEN, MEM_WIDTH), lambda b, s: (b, 0)),
        ],
        out_specs=pl.BlockSpec((MEM_TQ, MEM_WIDTH), lambda b, s: (b * steps + s, 0)),
        out_shape=jax.ShapeDtypeStruct((t, MEM_WIDTH), BF16),
        compiler_params=_params("parallel", "parallel"),
        name="mem_attn",
    )(mq, mk, mv)


def _merge_body(h_ref, gt_ref, oa_ref, ob_ref, oc_ref, wbr_ref, wo_ref, o_ref):
    y = gt_ref[0].astype(F32) * _dot(oa_ref[...], wbr_ref[0])
    y += gt_ref[1].astype(F32) * _dot(ob_ref[...], wbr_ref[1])
    y += gt_ref[2].astype(F32) * _dot(oc_ref[...], wbr_ref[2])
    o_ref[...] = h_ref[...] + _dot(y.astype(BF16), wo_ref[...])


def _merge(h, gates, o_a, o_b, o_c, w_branch, w_out, make_sides):
    t, d = h.shape
    row = lambda i: (i, 0)
    (out,), side_out = _hosted_call(
        _merge_body,
        grid=(t // MERGE_TM,),
        step_of=lambda i: i,
        in_specs=[
            pl.BlockSpec((MERGE_TM, d), row),
            pl.BlockSpec((N_BRANCH, MERGE_TM, d), lambda i: (0, i, 0)),
            pl.BlockSpec((MERGE_TM, BRANCH_WIDTH), row),
            pl.BlockSpec((MERGE_TM, BRANCH_WIDTH), row),
            pl.BlockSpec((MERGE_TM, BRANCH_WIDTH), row),
            pl.BlockSpec((N_BRANCH, BRANCH_WIDTH, d), lambda i: (0, 0, 0), pipeline_mode=SINGLE),
            pl.BlockSpec((d, d), lambda i: (0, 0), pipeline_mode=SINGLE),
        ],
        out_specs=[pl.BlockSpec((MERGE_TM, d), row)],
        out_shape=[jax.ShapeDtypeStruct((t, d), F32)],
        args=(h, gates, o_a, o_b, o_c, w_branch, w_out),
        sides=[m(lambda i: i) for m in make_sides],
        semantics=("arbitrary",),
        name="merge",
    )
    return out, side_out


def kernel(x, mem, g_ffn1, w_ffn1_in, w_ffn1_out, g_mix, w_in, gmlp_ln_g, gmlp_ln_b, w_s, b_s, swa_sinks, g_mem, w_mem_kv, w_branch, w_out, g_ffn2, w_ffn2_in, w_ffn2_out, g_final):
    batch, seq, d = x.shape
    depth = w_in.shape[0]
    xt = x.reshape(batch * seq, d)
    memt = mem.reshape(batch * MEM_LEN, d)
    w_branch_rows = w_branch.reshape(depth, N_BRANCH * BRANCH_WIDTH, d)

    def mixer_casts(l):
        return [functools.partial(_side_rows, w_in, l, MIXER_CAST_ROWS),
                functools.partial(_side_rows, w_mem_kv, l, MIXER_CAST_ROWS),
                functools.partial(_side_rows, w_out, l, MIXER_CAST_ROWS),
                functools.partial(_side_rows, w_branch_rows, l, 2 * MIXER_CAST_ROWS)]

    def ffn_casts(w_in_ffn, w_out_ffn, l, rows_in=CAST_ROWS // 2, rows_out=CAST_ROWS):
        return [functools.partial(_side_ffn_in, w_in_ffn, l, rows_in),
                functools.partial(_side_rows, w_out_ffn, l, rows_out)]

    wab1, wo1 = [_cast_alone(m) for m in ffn_casts(w_ffn1_in, w_ffn1_out, 0, ALONE_IN_ROWS, ALONE_OUT_ROWS)]

    for l in range(depth):
        more = l + 1 < depth
        h, (w_in_b, w_mkv, w_o, w_br) = _ffn(xt, g_ffn1[l], wab1, wo1, g_final, False, mixer_casts(l))
        n, o_a = _uv(h, g_mix[l], w_in_b, gmlp_ln_g[l], gmlp_ln_b[l], w_s[l], b_s[l])
        q, k_lo, k_hi, v_lo, v_hi, mq = _qkvm(n, w_in_b)
        gates, (wab2, wo2) = _gates(n, w_in_b, ffn_casts(w_ffn2_in, w_ffn2_out, l))
        mk, mv = _mkv(memt, g_mem[l], w_mkv)
        o_b = _swa(q, k_lo, k_hi, v_lo, v_hi, swa_sinks[l], batch, seq)
        o_c = _memattn(mq, mk, mv, batch, seq)
        h, merge_side = _merge(h, gates, o_a, o_b, o_c, w_br.reshape(N_BRANCH, BRANCH_WIDTH, d), w_o,
                               ffn_casts(w_ffn1_in, w_ffn1_out, l + 1) if more else [])
        xt, _ = _ffn(h, g_ffn2[l], wab2, wo2, g_final, not more)
        if more:
            wab1, wo1 = merge_side
    return xt.reshape(batch, seq, d)
```

```python
import functools
import math
from typing import Callable, NamedTuple

import jax
import jax.numpy as jnp
from jax import lax
from jax.experimental import pallas as pl
from jax.experimental.pallas import tpu as pltpu

D_MODEL = 2048
MEM_LEN = 256
CHUNK = 128
WINDOW = 128
G_GROUPS = 4
G_WIDTH = 1024
G_GROUP_DIM = G_WIDTH // G_GROUPS
SWA_HEADS = 16
SWA_KV_HEADS = 4
SWA_HEAD_DIM = 64
SWA_REP = SWA_HEADS // SWA_KV_HEADS
SWA_WIDTH = SWA_HEADS * SWA_HEAD_DIM
SWA_KV_WIDTH = SWA_KV_HEADS * SWA_HEAD_DIM
MEM_HEADS = 4
MEM_HEAD_DIM = 256
MEM_WIDTH = MEM_HEADS * MEM_HEAD_DIM
N_BRANCH = 3
BRANCH_WIDTH = 1024
D_FF = 5504
EPS = 1e-6
NEG = -1e30

F32 = jnp.float32
BF16 = jnp.bfloat16

LANES = 128
VMEM_LIMIT_BYTES = 60 * 1024 * 1024

FFN_TM = 1024
FFN_TF = 512
UV_TM = 1024
PROJ_TM = 1024
GATES_TM = 1024
MERGE_TM = 256
SWA_TQ = 1024
MEM_TQ = 2048
W_COLS = 512
CAST_ROWS = 128
MIXER_CAST_ROWS = 16
ALONE_IN_ROWS = 256
ALONE_OUT_ROWS = D_FF // 8

SINGLE = pl.Buffered(1)


def _params(*semantics):
    return pltpu.CompilerParams(dimension_semantics=semantics, vmem_limit_bytes=VMEM_LIMIT_BYTES)


def _rmsnorm_f32(xf, g):
    return xf * lax.rsqrt(jnp.mean(xf * xf, axis=-1, keepdims=True) + EPS) * g


def _dot(a, b):
    return jnp.dot(a, b, preferred_element_type=F32)


def _dot_nt(a, b):
    return lax.dot_general(a, b, (((1,), (1,)), ((), ())), preferred_element_type=F32)


class _SideCast(NamedTuple):
    srcs: tuple
    in_specs: tuple
    out_spec: pl.BlockSpec
    out_shape: jax.ShapeDtypeStruct
    write: Callable
    n_blocks: int


def _side_rows(w, l, rows_blk, step_of):
    _, rows, cols = w.shape
    n_blocks = rows // rows_blk
    assert rows % rows_blk == 0

    def blk(*g):
        return jnp.minimum(step_of(*g), n_blocks - 1)

    def write(step, in_refs, out_ref):
        @pl.when(step < n_blocks)
        def _():
            out_ref[...] = in_refs[0][...].astype(BF16)

    return _SideCast(
        (w,),
        (pl.BlockSpec((None, rows_blk, cols), lambda *g: (l, blk(*g), 0)),),
        pl.BlockSpec((rows_blk, cols), lambda *g: (blk(*g), 0)),
        jax.ShapeDtypeStruct((rows, cols), BF16), write, n_blocks)


def _side_ffn_in(w, l, rows_blk, step_of):
    _, d, _ = w.shape
    assert d % rows_blk == 0 and D_FF % LANES == 0
    n_blocks = d // rows_blk

    def blk(*g):
        return jnp.minimum(step_of(*g), n_blocks - 1)

    def write(step, in_refs, out_ref):
        @pl.when(step < n_blocks)
        def _():
            for half, in_ref in enumerate(in_refs):
                for k in range(D_FF // LANES):
                    dst = (2 * k + half) * LANES
                    out_ref[:, dst:dst + LANES] = in_ref[:, k * LANES:(k + 1) * LANES].astype(BF16)

    half_spec = lambda half: pl.BlockSpec((None, rows_blk, D_FF), lambda *g: (l, blk(*g), half))
    return _SideCast(
        (w, w),
        (half_spec(0), half_spec(1)),
        pl.BlockSpec((rows_blk, 2 * D_FF), lambda *g: (blk(*g), 0)),
        jax.ShapeDtypeStruct((d, 2 * D_FF), BF16), write, n_blocks)


def _hosted_call(body, *, grid, step_of, in_specs, out_specs, out_shape, args, sides, semantics, name,
                 scratch_shapes=()):
    n_in, n_out, k = len(in_specs), len(out_specs), len(sides)
    side_in_counts = [len(s.in_specs) for s in sides]
    n_side_in = sum(side_in_counts)
    n_steps = math.prod(grid)
    assert all(s.n_blocks <= n_steps for s in sides)

    def hosted(*refs):
        ins, side_ins = refs[:n_in], refs[n_in:n_in + n_side_in]
        outs = refs[n_in + n_side_in:n_in + n_side_in + n_out]
        side_outs = refs[n_in + n_side_in + n_out:n_in + n_side_in + n_out + k]
        scratch = refs[n_in + n_side_in + n_out + k:]
        if k:
            step = step_of(*[pl.program_id(a) for a in range(len(grid))])
            first = 0
            for side, count, o_ref in zip(sides, side_in_counts, side_outs):
                side.write(step, side_ins[first:first + count], o_ref)
                first += count
        body(*ins, *outs, *scratch)

    res = pl.pallas_call(
        hosted,
        grid=grid,
        in_specs=list(in_specs) + [spec for s in sides for spec in s.in_specs],
        out_specs=list(out_specs) + [s.out_spec for s in sides],
        out_shape=list(out_shape) + [s.out_shape for s in sides],
        scratch_shapes=list(scratch_shapes),
        compiler_params=_params(*semantics),
        name=name,
    )(*args, *[src for s in sides for src in s.srcs])
    return res[:n_out], res[n_out:]


def _cast_alone(make_side):
    side = make_side(lambda i: i)
    _, (out,) = _hosted_call(lambda: None, grid=(side.n_blocks,), step_of=lambda i: i, in_specs=[],
                             out_specs=[], out_shape=[], args=[], sides=[side],
                             semantics=("arbitrary",), name="cast")
    return out


FFN_STEPS = pl.cdiv(D_FF, FFN_TF)
FFN_LAST_TF = D_FF - (FFN_STEPS - 1) * FFN_TF


def _ffn_step(n, w_ref, wo_ref, o_ref, units, acc_ref=None):
    acc_ref = o_ref if acc_ref is None else acc_ref
    pairs = units // LANES
    even = pairs - pairs % 2
    z = [_dot(n, w_ref[:, :2 * even * LANES])] if even else []
    if pairs % 2:
        half_rows = n.shape[0] // 2
        w_last = w_ref[:, 2 * even * LANES:2 * pairs * LANES]
        z.append(jnp.concatenate([_dot(n[:half_rows], w_last), _dot(n[half_rows:], w_last)], axis=0))
    z = jnp.concatenate(z, axis=1)
    hid = []
    for k in range(pairs):
        a = z[:, (2 * k) * LANES:(2 * k + 1) * LANES]
        b = z[:, (2 * k + 1) * LANES:(2 * k + 2) * LANES]
        half_a = 0.5 * a
        hid.append((0.5 * ((half_a * jnp.tanh(half_a) + half_a) * b)).astype(BF16))
    hid = jnp.concatenate(hid, axis=1)
    for c in range(o_ref.shape[1] // FFN_TF):
        cs = slice(c * FFN_TF, (c + 1) * FFN_TF)
        o_ref[:, cs] = acc_ref[:, cs] + _dot(hid, wo_ref[:units, cs])


def _ffn_body(x_ref, g_ref, w_ref, wo_ref, gf_ref, o_ref, n_ref, *, apply_final_norm):
    j = pl.program_id(1)
    assert FFN_STEPS > 2

    @pl.when(j == 0)
    def _():
        n = _rmsnorm_f32(x_ref[...], g_ref[...]).astype(BF16)
        n_ref[...] = n
        _ffn_step(n, w_ref, wo_ref, o_ref, FFN_TF, acc_ref=x_ref)

    @pl.when((j > 0) & (j < FFN_STEPS - 1))
    def _():
        _ffn_step(n_ref[...], w_ref, wo_ref, o_ref, FFN_TF)

    @pl.when(j == FFN_STEPS - 1)
    def _():
        _ffn_step(n_ref[...], w_ref, wo_ref, o_ref, FFN_LAST_TF)
        if apply_final_norm:
            o_ref[...] = _rmsnorm_f32(o_ref[...], gf_ref[...])


def _ffn(x, g, w_in_b, w_out_b, g_final, apply_final_norm, make_sides=()):
    t, d = x.shape
    step_of = lambda i, j: i * FFN_STEPS + j
    (out,), side_out = _hosted_call(
        functools.partial(_ffn_body, apply_final_norm=apply_final_norm),
        grid=(t // FFN_TM, FFN_STEPS),
        step_of=step_of,
        in_specs=[
            pl.BlockSpec((FFN_TM, d), lambda i, j: (i, 0)),
            pl.BlockSpec((1, d), lambda i, j: (0, 0)),
            pl.BlockSpec((d, 2 * FFN_TF), lambda i, j: (0, j)),
            pl.BlockSpec((FFN_TF, d), lambda i, j: (j, 0)),
            pl.BlockSpec((1, d), lambda i, j: (0, 0)),
        ],
        out_specs=[pl.BlockSpec((FFN_TM, d), lambda i, j: (i, 0))],
        out_shape=[jax.ShapeDtypeStruct((t, d), F32)],
        args=(x, g.reshape(1, d), w_in_b, w_out_b, g_final.reshape(1, d)),
        sides=[m(step_of) for m in make_sides],
        semantics=("arbitrary", "arbitrary"),
        scratch_shapes=[pltpu.VMEM((FFN_TM, d), BF16)],
        name="ffn",
    )
    return out, side_out


UV_END = 2 * G_WIDTH
QKVM_WIDTH = SWA_WIDTH + 2 * SWA_KV_WIDTH + MEM_WIDTH
QKVM_END = UV_END + QKVM_WIDTH


def _gelu(x):
    return 0.5 * x * (1.0 + lax.erf(x * math.sqrt(0.5)))


def _uv_body(h_ref, g_ref, wu_ref, wv_ref, lng_ref, lnb_ref, ws_ref, bst_ref, n_ref, oa_ref):
    row = lax.broadcasted_iota(jnp.int32, (CHUNK, CHUNK), 0)
    col = lax.broadcasted_iota(jnp.int32, (CHUNK, CHUNK), 1)
    causal = row >= col

    n = _rmsnorm_f32(h_ref[...], g_ref[...]).astype(BF16)
    n_ref[...] = n
    v = _gelu(_dot(n, wv_ref[...]))
    u = _gelu(_dot(n, wu_ref[...]))
    mu = jnp.mean(v, axis=-1, keepdims=True)
    vc = v - mu
    var = jnp.mean(vc * vc, axis=-1, keepdims=True)
    vn = (vc * lax.rsqrt(var + EPS) * lng_ref[...] + lnb_ref[...]).astype(BF16)
    for g in range(G_GROUPS):
        ws = jnp.where(causal, ws_ref[g], 0.0).astype(BF16)
        bias = bst_ref[:, g:g + 1]
        cs = slice(g * G_GROUP_DIM, (g + 1) * G_GROUP_DIM)
        for c in range(UV_TM // CHUNK):
            rs = slice(c * CHUNK, (c + 1) * CHUNK)
            mixed = _dot(ws, vn[rs, cs]) + bias
            oa_ref[rs, cs] = (u[rs, cs] * mixed).astype(BF16)


def _uv(h, g, w_in_b, ln_g, ln_b, w_s, b_s):
    t, d = h.shape
    const = lambda i: (0, 0)
    (n, o_a), _ = _hosted_call(
        _uv_body,
        grid=(t // UV_TM,),
        step_of=lambda i: i,
        in_specs=[
            pl.BlockSpec((UV_TM, d), lambda i: (i, 0)),
            pl.BlockSpec((1, d), const),
            pl.BlockSpec((d, G_WIDTH), lambda i: (0, 0), pipeline_mode=SINGLE),
            pl.BlockSpec((d, G_WIDTH), lambda i: (0, 1), pipeline_mode=SINGLE),
            pl.BlockSpec((1, G_WIDTH), const),
            pl.BlockSpec((1, G_WIDTH), const),
            pl.BlockSpec((G_GROUPS, CHUNK, CHUNK), lambda i: (0, 0, 0)),
            pl.BlockSpec((CHUNK, G_GROUPS), const),
        ],
        out_specs=[
            pl.BlockSpec((UV_TM, d), lambda i: (i, 0)),
            pl.BlockSpec((UV_TM, G_WIDTH), lambda i: (i, 0)),
        ],
        out_shape=[
            jax.ShapeDtypeStruct((t, d), BF16),
            jax.ShapeDtypeStruct((t, G_WIDTH), BF16),
        ],
        args=(h, g.reshape(1, d), w_in_b, w_in_b, ln_g.reshape(1, G_WIDTH), ln_b.reshape(1, G_WIDTH), w_s,
              jnp.transpose(b_s)),
        sides=[],
        semantics=("parallel",),
        name="uv_gmlp",
    )
    return n, o_a


SWA_SCALE = 1.0 / math.sqrt(SWA_HEAD_DIM)
HALF_LANES = LANES // 2
KV_PLACED_WIDTH = SWA_KV_HEADS * LANES
QKVM_BLOCKS = QKVM_WIDTH // W_COLS


def _place_heads(z):
    rows = z.shape[0]
    low_half = lax.broadcasted_iota(jnp.int32, (rows, LANES), 1) < HALF_LANES
    zero = jnp.zeros((rows, LANES), F32)
    low, high = [], []
    for pair in range(SWA_KV_HEADS // 2):
        zg = z[:, pair * LANES:(pair + 1) * LANES]
        swapped = pltpu.roll(zg, HALF_LANES, axis=1)
        low += [jnp.where(low_half, zg, zero), jnp.where(low_half, swapped, zero)]
        high += [jnp.where(low_half, zero, swapped), jnp.where(low_half, zero, zg)]
    return (jnp.concatenate(low, axis=1).astype(BF16), jnp.concatenate(high, axis=1).astype(BF16))


def _qkvm_body(n_ref, wq0_ref, wq1_ref, wkv_ref, wm0_ref, wm1_ref,
               q_ref, klo_ref, khi_ref, vlo_ref, vhi_ref, mq_ref):
    n = n_ref[...]
    q_ref[:, :W_COLS] = (_dot(n, wq0_ref[...]) * SWA_SCALE).astype(BF16)
    q_ref[:, W_COLS:] = (_dot(n, wq1_ref[...]) * SWA_SCALE).astype(BF16)
    kv = _dot(n, wkv_ref[...])
    klo_ref[...], khi_ref[...] = _place_heads(kv[:, :SWA_KV_WIDTH])
    vlo_ref[...], vhi_ref[...] = _place_heads(kv[:, SWA_KV_WIDTH:])
    mq_ref[:, :W_COLS] = _dot(n, wm0_ref[...]).astype(BF16)
    mq_ref[:, W_COLS:] = _dot(n, wm1_ref[...]).astype(BF16)


def _qkvm(n, w_in_b):
    t, d = n.shape
    assert SWA_WIDTH == 2 * W_COLS and 2 * SWA_KV_WIDTH == W_COLS and MEM_WIDTH == 2 * W_COLS
    first = UV_END // W_COLS
    widths = (SWA_WIDTH,) + (KV_PLACED_WIDTH,) * 4 + (MEM_WIDTH,)
    w_specs = [pl.BlockSpec((d, W_COLS), functools.partial(lambda c, i: (0, c), first + c),
                            pipeline_mode=SINGLE) for c in range(QKVM_BLOCKS)]
    outs, _ = _hosted_call(
        _qkvm_body,
        grid=(t // PROJ_TM,),
        step_of=lambda i: i,
        in_specs=[pl.BlockSpec((PROJ_TM, d), lambda i: (i, 0))] + w_specs,
        out_specs=[pl.BlockSpec((PROJ_TM, w_), lambda i: (i, 0)) for w_ in widths],
        out_shape=[jax.ShapeDtypeStruct((t, w_), BF16) for w_ in widths],
        args=(n,) + (w_in_b,) * QKVM_BLOCKS,
        sides=[],
        semantics=("parallel",),
        name="qkvm",
    )
    return outs


GATE_BLOCKS = D_MODEL // W_COLS


def _gates_body(n_ref, *refs):
    w_refs, o_ref = refs[:GATE_BLOCKS], refs[GATE_BLOCKS]
    n = n_ref[...]
    for c, w_ref in enumerate(w_refs):
        z = _dot(n, w_ref[...])
        o_ref[:, c * W_COLS:(c + 1) * W_COLS] = (0.5 * jnp.tanh(0.5 * z) + 0.5).astype(BF16)


def _gates(n, w_in_b, make_sides):
    t, d = n.shape
    tiles = t // GATES_TM
    first = QKVM_END // W_COLS
    step_of = lambda b, i: b * tiles + i
    w_specs = [pl.BlockSpec((d, W_COLS), functools.partial(lambda c, b, i: (0, first + b * GATE_BLOCKS + c), c))
               for c in range(GATE_BLOCKS)]
    (gates,), side_out = _hosted_call(
        _gates_body,
        grid=(N_BRANCH, tiles),
        step_of=step_of,
        in_specs=[pl.BlockSpec((GATES_TM, d), lambda b, i: (i, 0))] + w_specs,
        out_specs=[pl.BlockSpec((None, GATES_TM, d), lambda b, i: (b, i, 0))],
        out_shape=[jax.ShapeDtypeStruct((N_BRANCH, t, d), BF16)],
        args=(n,) + (w_in_b,) * GATE_BLOCKS,
        sides=[m(step_of) for m in make_sides],
        semantics=("arbitrary", "arbitrary"),
        name="gates",
    )
    return gates, side_out


def _mkv_body(m_ref, g_ref, w_ref, mk_ref, mv_ref):
    n = _rmsnorm_f32(m_ref[...], g_ref[...]).astype(BF16)
    z = _dot(n, w_ref[...]).astype(BF16)
    mk_ref[...] = z[:, :MEM_WIDTH]
    mv_ref[...] = z[:, MEM_WIDTH:]


def _mkv(mem, g, w):
    t, d = mem.shape
    return pl.pallas_call(
        _mkv_body,
        grid=(t // MEM_LEN,),
        in_specs=[
            pl.BlockSpec((MEM_LEN, d), lambda i: (i, 0)),
            pl.BlockSpec((1, d), lambda i: (0, 0)),
            pl.BlockSpec((d, 2 * MEM_WIDTH), lambda i: (0, 0), pipeline_mode=SINGLE),
        ],
        out_specs=[pl.BlockSpec((MEM_LEN, MEM_WIDTH), lambda i: (i, 0))] * 2,
        out_shape=[jax.ShapeDtypeStruct((t, MEM_WIDTH), BF16)] * 2,
        compiler_params=_params("parallel"),
        name="mem_kv",
    )(mem, g.reshape(1, d), w)


SWA_SLOPES = tuple(2.0 ** (-8.0 * (h + 1) / SWA_HEADS) for h in range(SWA_HEADS))


def _swa_body(sink_ref, q_ref, klo_ref, khi_ref, vlo_ref, vhi_ref,
              klo_p_ref, khi_p_ref, vlo_p_ref, vhi_p_ref, o_ref):
    w = WINDOW
    t = pl.program_id(1)
    i2 = lax.broadcasted_iota(jnp.int32, (2 * w, w), 0) & (w - 1)
    j2 = lax.broadcasted_iota(jnp.int32, (2 * w, w), 1)
    from_prev = j2 > i2
    dist = ((i2 - j2) & (w - 1)).astype(F32)
    no_prev = j2 > jnp.maximum(i2, jnp.where(t == 0, -1, w))
    low_half = j2 < HALF_LANES
    zero = jnp.zeros((2 * w, w), F32)
    ones_cols = (jnp.where(low_half, 1.0, 0.0).astype(BF16), jnp.where(low_half, 0.0, 1.0).astype(BF16))

    def rows2(top, bottom):
        return jnp.concatenate([jnp.full((w, w), top, F32), jnp.full((w, w), bottom, F32)], axis=0)

    for blk in range(SWA_TQ // w):
        rs = slice(blk * w, (blk + 1) * w)

        def band(cur_ref, prev_ref, cols):
            prev = prev_ref[:, cols] if blk == 0 else cur_ref[(blk - 1) * w:blk * w, cols]
            return jnp.concatenate([prev, cur_ref[rs, cols]], axis=0)

        for kvh in range(SWA_KV_HEADS):
            cols = slice(kvh * LANES, (kvh + 1) * LANES)
            q2 = jnp.concatenate([q_ref[rs, (2 * kvh) * LANES:(2 * kvh + 1) * LANES],
                                  q_ref[rs, (2 * kvh + 1) * LANES:(2 * kvh + 2) * LANES]], axis=0)
            keys = jnp.concatenate([band(klo_ref, klo_p_ref, cols), band(khi_ref, khi_p_ref, cols)], axis=0)
            s = _dot_nt(q2, keys)
            acc = None
            sink_terms = []
            for par, (v_ref, v_p_ref) in enumerate(((vlo_ref, vlo_p_ref), (vhi_ref, vhi_p_ref))):
                heads = (kvh * SWA_REP + par, kvh * SWA_REP + 2 + par)
                base = par * 2 * w
                sc = jnp.where(from_prev, s[:, base:base + w], s[:, base + w:base + 2 * w])
                sc = sc - rows2(SWA_SLOPES[heads[0]], SWA_SLOPES[heads[1]]) * dist
                if blk == 0:
                    sc = jnp.where(no_prev, NEG, sc)
                sink = rows2(sink_ref[heads[0]], sink_ref[heads[1]])
                m = jnp.maximum(jnp.broadcast_to(jnp.max(sc, axis=-1, keepdims=True), (2 * w, w)), sink)
                p = jnp.exp(sc - m)
                pcat = jnp.concatenate([jnp.where(from_prev, p, zero).astype(BF16),
                                        jnp.where(from_prev, zero, p).astype(BF16)], axis=1)
                vext = jnp.concatenate([band(v_ref, v_p_ref, cols), ones_cols[par]], axis=1)
                part = _dot(pcat, vext)
                acc = part if acc is None else acc + part
                sink_terms.append(jnp.exp(sink - m))
            denom = acc[:, w:] + jnp.where(low_half, sink_terms[0], sink_terms[1])
            out = (acc[:, :w] / denom).astype(BF16)
            o_ref[rs, (2 * kvh) * LANES:(2 * kvh + 1) * LANES] = out[:w]
            o_ref[rs, (2 * kvh + 1) * LANES:(2 * kvh + 2) * LANES] = out[w:]


def _swa(q, k_lo, k_hi, v_lo, v_hi, sinks, batch, seq):
    t = q.shape[0]
    steps = seq // SWA_TQ
    blocks_per_step = SWA_TQ // WINDOW
    blocks_per_seq = seq // WINDOW

    def cur(b, s):
        return (b * steps + s, 0)

    def prev(b, s):
        return (b * blocks_per_seq + jnp.maximum(s * blocks_per_step - 1, 0), 0)

    kv_cur = pl.BlockSpec((SWA_TQ, KV_PLACED_WIDTH), cur)
    kv_prev = pl.BlockSpec((WINDOW, KV_PLACED_WIDTH), prev)
    return pl.pallas_call(
        _swa_body,
        grid=(batch, steps),
        in_specs=[pl.BlockSpec(memory_space=pltpu.SMEM), pl.BlockSpec((SWA_TQ, SWA_WIDTH), cur)]
        + [kv_cur] * 4 + [kv_prev] * 4,
        out_specs=pl.BlockSpec((SWA_TQ, SWA_WIDTH), cur),
        out_shape=jax.ShapeDtypeStruct((t, SWA_WIDTH), BF16),
        compiler_params=_params("parallel", "parallel"),
        name="swa",
    )(sinks, q, k_lo, k_hi, v_lo, v_hi, k_lo, k_hi, v_lo, v_hi)


MEM_SCALE = 1.0 / math.sqrt(MEM_HEAD_DIM)


def _memattn_body(q_ref, mk_ref, mv_ref, o_ref):
    for h in range(MEM_HEADS):
        hs = slice(h * MEM_HEAD_DIM, (h + 1) * MEM_HEAD_DIM)
        s = _dot_nt(q_ref[:, hs], mk_ref[:, hs]) * MEM_SCALE
        m = jnp.max(s, axis=-1, keepdims=True)
        p = jnp.exp(s - m)
        probs = (p / jnp.sum(p, axis=-1, keepdims=True)).astype(BF16)
        o_ref[:, hs] = _dot(probs, mv_ref[:, hs]).astype(BF16)


def _memattn(mq, mk, mv, batch, seq):
    t = mq.shape[0]
    steps = seq // MEM_TQ
    return pl.pallas_call(
        _memattn_body,
        grid=(batch, steps),
        in_specs=[
            pl.BlockSpec((MEM_TQ, MEM_WIDTH), lambda b, s: (b * steps + s, 0)),
            pl.BlockSpec((MEM_LEN, MEM_WIDTH), lambda b, s: (b, 0)),
            pl.BlockSpec((MEM_LEN, MEM_WIDTH), lambda b, s: (b, 0)),
        ],
        out_specs=pl.BlockSpec((MEM_TQ, MEM_WIDTH), lambda b, s: (b * steps + s, 0)),
        out_shape=jax.ShapeDtypeStruct((t, MEM_WIDTH), BF16),
        compiler_params=_params("parallel", "parallel"),
        name="mem_attn",
    )(mq, mk, mv)


def _merge_body(h_ref, gt_ref, oa_ref, ob_ref, oc_ref, wbr_ref, wo_ref, o_ref):
    y = gt_ref[0].astype(F32) * _dot(oa_ref[...], wbr_ref[0])
    y += gt_ref[1].astype(F32) * _dot(ob_ref[...], wbr_ref[1])
    y += gt_ref[2].astype(F32) * _dot(oc_ref[...], wbr_ref[2])
    o_ref[...] = h_ref[...] + _dot(y.astype(BF16), wo_ref[...])


def _merge(h, gates, o_a, o_b, o_c, w_branch, w_out, make_sides):
    t, d = h.shape
    row = lambda i: (i, 0)
    (out,), side_out = _hosted_call(
        _merge_body,
        grid=(t // MERGE_TM,),
        step_of=lambda i: i,
        in_specs=[
            pl.BlockSpec((MERGE_TM, d), row),
            pl.BlockSpec((N_BRANCH, MERGE_TM, d), lambda i: (0, i, 0)),
            pl.BlockSpec((MERGE_TM, BRANCH_WIDTH), row),
            pl.BlockSpec((MERGE_TM, BRANCH_WIDTH), row),
            pl.BlockSpec((MERGE_TM, BRANCH_WIDTH), row),
            pl.BlockSpec((N_BRANCH, BRANCH_WIDTH, d), lambda i: (0, 0, 0), pipeline_mode=SINGLE),
            pl.BlockSpec((d, d), lambda i: (0, 0), pipeline_mode=SINGLE),
        ],
        out_specs=[pl.BlockSpec((MERGE_TM, d), row)],
        out_shape=[jax.ShapeDtypeStruct((t, d), F32)],
        args=(h, gates, o_a, o_b, o_c, w_branch, w_out),
        sides=[m(lambda i: i) for m in make_sides],
        semantics=("arbitrary",),
        name="merge",
    )
    return out, side_out


def kernel(x, mem, g_ffn1, w_ffn1_in, w_ffn1_out, g_mix, w_in, gmlp_ln_g, gmlp_ln_b, w_s, b_s, swa_sinks, g_mem, w_mem_kv, w_branch, w_out, g_ffn2, w_ffn2_in, w_ffn2_out, g_final):
    batch, seq, d = x.shape
    depth = w_in.shape[0]
    xt = x.reshape(batch * seq, d)
    memt = mem.reshape(batch * MEM_LEN, d)
    w_branch_rows = w_branch.reshape(depth, N_BRANCH * BRANCH_WIDTH, d)

    def mixer_casts(l):
        return [functools.partial(_side_rows, w_in, l, MIXER_CAST_ROWS),
                functools.partial(_side_rows, w_mem_kv, l, MIXER_CAST_ROWS),
                functools.partial(_side_rows, w_out, l, MIXER_CAST_ROWS),
                functools.partial(_side_rows, w_branch_rows, l, 2 * MIXER_CAST_ROWS)]

    def ffn_casts(w_in_ffn, w_out_ffn, l, rows_in=CAST_ROWS // 2, rows_out=CAST_ROWS):
        return [functools.partial(_side_ffn_in, w_in_ffn, l, rows_in),
                functools.partial(_side_rows, w_out_ffn, l, rows_out)]

    wab1, wo1 = [_cast_alone(m) for m in ffn_casts(w_ffn1_in, w_ffn1_out, 0, ALONE_IN_ROWS, ALONE_OUT_ROWS)]

    for l in range(depth):
        more = l + 1 < depth
        h, (w_in_b, w_mkv, w_o, w_br) = _ffn(xt, g_ffn1[l], wab1, wo1, g_final, False, mixer_casts(l))
        n, o_a = _uv(h, g_mix[l], w_in_b, gmlp_ln_g[l], gmlp_ln_b[l], w_s[l], b_s[l])
        q, k_lo, k_hi, v_lo, v_hi, mq = _qkvm(n, w_in_b)
        gates, (wab2, wo2) = _gates(n, w_in_b, ffn_casts(w_ffn2_in, w_ffn2_out, l))
        mk, mv = _mkv(memt, g_mem[l], w_mkv)
        o_b = _swa(q, k_lo, k_hi, v_lo, v_hi, swa_sinks[l], batch, seq)
        o_c = _memattn(mq, mk, mv, batch, seq)
        h, merge_side = _merge(h, gates, o_a, o_b, o_c, w_br.reshape(N_BRANCH, BRANCH_WIDTH, d), w_o,
                               ffn_casts(w_ffn1_in, w_ffn1_out, l + 1) if more else [])
        xt, _ = _ffn(h, g_ffn2[l], wab2, wo2, g_final, not more)
        if more:
            wab1, wo1 = merge_side
    return xt.reshape(batch, seq, d)
```

```python
import functools
import math
from typing import Callable, NamedTuple

import jax
import jax.numpy as jnp
from jax import lax
from jax.experimental import pallas as pl
from jax.experimental.pallas import tpu as pltpu

D_MODEL = 2048
MEM_LEN = 256
CHUNK = 128
WINDOW = 128
G_GROUPS = 4
G_WIDTH = 1024
G_GROUP_DIM = G_WIDTH // G_GROUPS
SWA_HEADS = 16
SWA_KV_HEADS = 4
SWA_HEAD_DIM = 64
SWA_REP = SWA_HEADS // SWA_KV_HEADS
SWA_WIDTH = SWA_HEADS * SWA_HEAD_DIM
SWA_KV_WIDTH = SWA_KV_HEADS * SWA_HEAD_DIM
MEM_HEADS = 4
MEM_HEAD_DIM = 256
MEM_WIDTH = MEM_HEADS * MEM_HEAD_DIM
N_BRANCH = 3
BRANCH_WIDTH = 1024
D_FF = 5504
EPS = 1e-6
NEG = -1e30

F32 = jnp.float32
BF16 = jnp.bfloat16

LANES = 128
VMEM_LIMIT_BYTES = 60 * 1024 * 1024

FFN_TM = 1024
FFN_TF = 512
UV_TM = 1024
PROJ_TM = 1024
GATES_TM = 1024
MERGE_TM = 256
SWA_TQ = 1024
MEM_TQ = 2048
W_COLS = 512
CAST_ROWS = 128
MIXER_CAST_ROWS = 16
ALONE_IN_ROWS = 256
ALONE_OUT_ROWS = D_FF // 8

SINGLE = pl.Buffered(1)


def _params(*semantics):
    return pltpu.CompilerParams(dimension_semantics=semantics, vmem_limit_bytes=VMEM_LIMIT_BYTES)


def _rmsnorm_f32(xf, g):
    return xf * lax.rsqrt(jnp.mean(xf * xf, axis=-1, keepdims=True) + EPS) * g


def _dot(a, b):
    return jnp.dot(a, b, preferred_element_type=F32)


def _dot_nt(a, b):
    return lax.dot_general(a, b, (((1,), (1,)), ((), ())), preferred_element_type=F32)


class _SideCast(NamedTuple):
    srcs: tuple
    in_specs: tuple
    out_spec: pl.BlockSpec
    out_shape: jax.ShapeDtypeStruct
    write: Callable
    n_blocks: int


def _side_rows(w, l, rows_blk, step_of):
    _, rows, cols = w.shape
    n_blocks = rows // rows_blk
    assert rows % rows_blk == 0

    def blk(*g):
        return jnp.minimum(step_of(*g), n_blocks - 1)

    def write(step, in_refs, out_ref):
        @pl.when(step < n_blocks)
        def _():
            out_ref[...] = in_refs[0][...].astype(BF16)

    return _SideCast(
        (w,),
        (pl.BlockSpec((None, rows_blk, cols), lambda *g: (l, blk(*g), 0)),),
        pl.BlockSpec((rows_blk, cols), lambda *g: (blk(*g), 0)),
        jax.ShapeDtypeStruct((rows, cols), BF16), write, n_blocks)


def _side_ffn_in(w, l, rows_blk, step_of):
    _, d, _ = w.shape
    assert d % rows_blk == 0 and D_FF % LANES == 0
    n_blocks = d // rows_blk

    def blk(*g):
        return jnp.minimum(step_of(*g), n_blocks - 1)

    def write(step, in_refs, out_ref):
        @pl.when(step < n_blocks)
        def _():
            for half, in_ref in enumerate(in_refs):
                for k in range(D_FF // LANES):
                    dst = (2 * k + half) * LANES
                    out_ref[:, dst:dst + LANES] = in_ref[:, k * LANES:(k + 1) * LANES].astype(BF16)

    half_spec = lambda half: pl.BlockSpec((None, rows_blk, D_FF), lambda *g: (l, blk(*g), half))
    return _SideCast(
        (w, w),
        (half_spec(0), half_spec(1)),
        pl.BlockSpec((rows_blk, 2 * D_FF), lambda *g: (blk(*g), 0)),
        jax.ShapeDtypeStruct((d, 2 * D_FF), BF16), write, n_blocks)


def _hosted_call(body, *, grid, step_of, in_specs, out_specs, out_shape, args, sides, semantics, name,
                 scratch_shapes=()):
    n_in, n_out, k = len(in_specs), len(out_specs), len(sides)
    side_in_counts = [len(s.in_specs) for s in sides]
    n_side_in = sum(side_in_counts)
    n_steps = math.prod(grid)
    assert all(s.n_blocks <= n_steps for s in sides)

    def hosted(*refs):
        ins, side_ins = refs[:n_in], refs[n_in:n_in + n_side_in]
        outs = refs[n_in + n_side_in:n_in + n_side_in + n_out]
        side_outs = refs[n_in + n_side_in + n_out:n_in + n_side_in + n_out + k]
        scratch = refs[n_in + n_side_in + n_out + k:]
        if k:
            step = step_of(*[pl.program_id(a) for a in range(len(grid))])
            first = 0
            for side, count, o_ref in zip(sides, side_in_counts, side_outs):
                side.write(step, side_ins[first:first + count], o_ref)
                first += count
        body(*ins, *outs, *scratch)

    res = pl.pallas_call(
        hosted,
        grid=grid,
        in_specs=list(in_specs) + [spec for s in sides for spec in s.in_specs],
        out_specs=list(out_specs) + [s.out_spec for s in sides],
        out_shape=list(out_shape) + [s.out_shape for s in sides],
        scratch_shapes=list(scratch_shapes),
        compiler_params=_params(*semantics),
        name=name,
    )(*args, *[src for s in sides for src in s.srcs])
    return res[:n_out], res[n_out:]


def _cast_alone(make_side):
    side = make_side(lambda i: i)
    _, (out,) = _hosted_call(lambda: None, grid=(side.n_blocks,), step_of=lambda i: i, in_specs=[],
                             out_specs=[], out_shape=[], args=[], sides=[side],
                             semantics=("arbitrary",), name="cast")
    return out


FFN_STEPS = pl.cdiv(D_FF, FFN_TF)
FFN_LAST_TF = D_FF - (FFN_STEPS - 1) * FFN_TF


def _ffn_step(n, w_ref, wo_ref, o_ref, units, acc_ref=None):
    acc_ref = o_ref if acc_ref is None else acc_ref
    pairs = units // LANES
    even = pairs - pairs % 2
    z = [_dot(n, w_ref[:, :2 * even * LANES])] if even else []
    if pairs % 2:
        half_rows = n.shape[0] // 2
        w_last = w_ref[:, 2 * even * LANES:2 * pairs * LANES]
        z.append(jnp.concatenate([_dot(n[:half_rows], w_last), _dot(n[half_rows:], w_last)], axis=0))
    z = jnp.concatenate(z, axis=1)
    hid = []
    for k in range(pairs):
        a = z[:, (2 * k) * LANES:(2 * k + 1) * LANES]
        b = z[:, (2 * k + 1) * LANES:(2 * k + 2) * LANES]
        half_a = 0.5 * a
        hid.append((0.5 * ((half_a * jnp.tanh(half_a) + half_a) * b)).astype(BF16))
    hid = jnp.concatenate(hid, axis=1)
    for c in range(o_ref.shape[1] // FFN_TF):
        cs = slice(c * FFN_TF, (c + 1) * FFN_TF)
        o_ref[:, cs] = acc_ref[:, cs] + _dot(hid, wo_ref[:units, cs])


def _ffn_body(x_hbm, g_ref, w_ref, wo_ref, gf_ref, o_ref, n_ref, x_ref, x_sem, *, apply_final_norm):
    i, j = pl.program_id(0), pl.program_id(1)
    assert FFN_STEPS > 2

    def x_copy(tile):
        return pltpu.make_async_copy(x_hbm.at[pl.ds(tile * FFN_TM, FFN_TM), :], x_ref, x_sem)

    @pl.when((i == 0) & (j == 0))
    def _():
        x_copy(0).start()

    @pl.when(j == 0)
    def _():
        x_copy(i).wait()
        n = _rmsnorm_f32(x_ref[...], g_ref[...]).astype(BF16)
        n_ref[...] = n
        _ffn_step(n, w_ref, wo_ref, o_ref, FFN_TF, acc_ref=x_ref)

    @pl.when((j == 1) & (i + 1 < pl.num_programs(0)))
    def _():
        x_copy(i + 1).start()

    @pl.when((j > 0) & (j < FFN_STEPS - 1))
    def _():
        _ffn_step(n_ref[...], w_ref, wo_ref, o_ref, FFN_TF)

    @pl.when(j == FFN_STEPS - 1)
    def _():
        _ffn_step(n_ref[...], w_ref, wo_ref, o_ref, FFN_LAST_TF)
        if apply_final_norm:
            o_ref[...] = _rmsnorm_f32(o_ref[...], gf_ref[...])


def _ffn(x, g, w_in_b, w_out_b, g_final, apply_final_norm, make_sides=()):
    t, d = x.shape
    step_of = lambda i, j: i * FFN_STEPS + j
    (out,), side_out = _hosted_call(
        functools.partial(_ffn_body, apply_final_norm=apply_final_norm),
        grid=(t // FFN_TM, FFN_STEPS),
        step_of=step_of,
        in_specs=[
            pl.BlockSpec(memory_space=pl.ANY),
            pl.BlockSpec((1, d), lambda i, j: (0, 0)),
            pl.BlockSpec((d, 2 * FFN_TF), lambda i, j: (0, j)),
            pl.BlockSpec((FFN_TF, d), lambda i, j: (j, 0)),
            pl.BlockSpec((1, d), lambda i, j: (0, 0)),
        ],
        out_specs=[pl.BlockSpec((FFN_TM, d), lambda i, j: (i, 0))],
        out_shape=[jax.ShapeDtypeStruct((t, d), F32)],
        args=(x, g.reshape(1, d), w_in_b, w_out_b, g_final.reshape(1, d)),
        sides=[m(step_of) for m in make_sides],
        semantics=("arbitrary", "arbitrary"),
        scratch_shapes=[pltpu.VMEM((FFN_TM, d), BF16), pltpu.VMEM((FFN_TM, d), F32),
                        pltpu.SemaphoreType.DMA(())],
        name="ffn",
    )
    return out, side_out


UV_END = 2 * G_WIDTH
QKVM_WIDTH = SWA_WIDTH + 2 * SWA_KV_WIDTH + MEM_WIDTH
QKVM_END = UV_END + QKVM_WIDTH


def _gelu(x):
    return 0.5 * x * (1.0 + lax.erf(x * math.sqrt(0.5)))


def _uv_body(h_ref, g_ref, wu_ref, wv_ref, lng_ref, lnb_ref, ws_ref, bst_ref, n_ref, oa_ref):
    row = lax.broadcasted_iota(jnp.int32, (CHUNK, CHUNK), 0)
    col = lax.broadcasted_iota(jnp.int32, (CHUNK, CHUNK), 1)
    causal = row >= col

    n = _rmsnorm_f32(h_ref[...], g_ref[...]).astype(BF16)
    n_ref[...] = n
    v = _gelu(_dot(n, wv_ref[...]))
    u = _gelu(_dot(n, wu_ref[...]))
    mu = jnp.mean(v, axis=-1, keepdims=True)
    vc = v - mu
    var = jnp.mean(vc * vc, axis=-1, keepdims=True)
    vn = (vc * lax.rsqrt(var + EPS) * lng_ref[...] + lnb_ref[...]).astype(BF16)
    for g in range(G_GROUPS):
        ws = jnp.where(causal, ws_ref[g], 0.0).astype(BF16)
        bias = bst_ref[:, g:g + 1]
        cs = slice(g * G_GROUP_DIM, (g + 1) * G_GROUP_DIM)
        for c in range(UV_TM // CHUNK):
            rs = slice(c * CHUNK, (c + 1) * CHUNK)
            mixed = _dot(ws, vn[rs, cs]) + bias
            oa_ref[rs, cs] = (u[rs, cs] * mixed).astype(BF16)


def _uv(h, g, w_in_b, ln_g, ln_b, w_s, b_s):
    t, d = h.shape
    const = lambda i: (0, 0)
    (n, o_a), _ = _hosted_call(
        _uv_body,
        grid=(t // UV_TM,),
        step_of=lambda i: i,
        in_specs=[
            pl.BlockSpec((UV_TM, d), lambda i: (i, 0)),
            pl.BlockSpec((1, d), const),
            pl.BlockSpec((d, G_WIDTH), lambda i: (0, 0), pipeline_mode=SINGLE),
            pl.BlockSpec((d, G_WIDTH), lambda i: (0, 1), pipeline_mode=SINGLE),
            pl.BlockSpec((1, G_WIDTH), const),
            pl.BlockSpec((1, G_WIDTH), const),
            pl.BlockSpec((G_GROUPS, CHUNK, CHUNK), lambda i: (0, 0, 0)),
            pl.BlockSpec((CHUNK, G_GROUPS), const),
        ],
        out_specs=[
            pl.BlockSpec((UV_TM, d), lambda i: (i, 0)),
            pl.BlockSpec((UV_TM, G_WIDTH), lambda i: (i, 0)),
        ],
        out_shape=[
            jax.ShapeDtypeStruct((t, d), BF16),
            jax.ShapeDtypeStruct((t, G_WIDTH), BF16),
        ],
        args=(h, g.reshape(1, d), w_in_b, w_in_b, ln_g.reshape(1, G_WIDTH), ln_b.reshape(1, G_WIDTH), w_s,
              jnp.transpose(b_s)),
        sides=[],
        semantics=("parallel",),
        name="uv_gmlp",
    )
    return n, o_a


SWA_SCALE = 1.0 / math.sqrt(SWA_HEAD_DIM)
HALF_LANES = LANES // 2
KV_PLACED_WIDTH = SWA_KV_HEADS * LANES
QKVM_BLOCKS = QKVM_WIDTH // W_COLS


def _place_heads(z):
    rows = z.shape[0]
    low_half = lax.broadcasted_iota(jnp.int32, (rows, LANES), 1) < HALF_LANES
    zero = jnp.zeros((rows, LANES), F32)
    low, high = [], []
    for pair in range(SWA_KV_HEADS // 2):
        zg = z[:, pair * LANES:(pair + 1) * LANES]
        swapped = pltpu.roll(zg, HALF_LANES, axis=1)
        low += [jnp.where(low_half, zg, zero), jnp.where(low_half, swapped, zero)]
        high += [jnp.where(low_half, zero, swapped), jnp.where(low_half, zero, zg)]
    return (jnp.concatenate(low, axis=1).astype(BF16), jnp.concatenate(high, axis=1).astype(BF16))


def _qkvm_body(n_ref, wq0_ref, wq1_ref, wkv_ref, wm0_ref, wm1_ref,
               q_ref, klo_ref, khi_ref, vlo_ref, vhi_ref, mq_ref):
    n = n_ref[...]
    q_ref[:, :W_COLS] = (_dot(n, wq0_ref[...]) * SWA_SCALE).astype(BF16)
    q_ref[:, W_COLS:] = (_dot(n, wq1_ref[...]) * SWA_SCALE).astype(BF16)
    kv = _dot(n, wkv_ref[...])
    klo_ref[...], khi_ref[...] = _place_heads(kv[:, :SWA_KV_WIDTH])
    vlo_ref[...], vhi_ref[...] = _place_heads(kv[:, SWA_KV_WIDTH:])
    mq_ref[:, :W_COLS] = _dot(n, wm0_ref[...]).astype(BF16)
    mq_ref[:, W_COLS:] = _dot(n, wm1_ref[...]).astype(BF16)


def _qkvm(n, w_in_b):
    t, d = n.shape
    assert SWA_WIDTH == 2 * W_COLS and 2 * SWA_KV_WIDTH == W_COLS and MEM_WIDTH == 2 * W_COLS
    first = UV_END // W_COLS
    widths = (SWA_WIDTH,) + (KV_PLACED_WIDTH,) * 4 + (MEM_WIDTH,)
    w_specs = [pl.BlockSpec((d, W_COLS), functools.partial(lambda c, i: (0, c), first + c),
                            pipeline_mode=SINGLE) for c in range(QKVM_BLOCKS)]
    outs, _ = _hosted_call(
        _qkvm_body,
        grid=(t // PROJ_TM,),
        step_of=lambda i: i,
        in_specs=[pl.BlockSpec((PROJ_TM, d), lambda i: (i, 0))] + w_specs,
        out_specs=[pl.BlockSpec((PROJ_TM, w_), lambda i: (i, 0)) for w_ in widths],
        out_shape=[jax.ShapeDtypeStruct((t, w_), BF16) for w_ in widths],
        args=(n,) + (w_in_b,) * QKVM_BLOCKS,
        sides=[],
        semantics=("parallel",),
        name="qkvm",
    )
    return outs


GATE_BLOCKS = D_MODEL // W_COLS


def _gates_body(n_ref, *refs):
    w_refs, o_ref = refs[:GATE_BLOCKS], refs[GATE_BLOCKS]
    n = n_ref[...]
    for c, w_ref in enumerate(w_refs):
        z = _dot(n, w_ref[...])
        o_ref[:, c * W_COLS:(c + 1) * W_COLS] = (0.5 * jnp.tanh(0.5 * z) + 0.5).astype(BF16)


def _gates(n, w_in_b, make_sides):
    t, d = n.shape
    tiles = t // GATES_TM
    first = QKVM_END // W_COLS
    step_of = lambda b, i: b * tiles + i
    w_specs = [pl.BlockSpec((d, W_COLS), functools.partial(lambda c, b, i: (0, first + b * GATE_BLOCKS + c), c))
               for c in range(GATE_BLOCKS)]
    (gates,), side_out = _hosted_call(
        _gates_body,
        grid=(N_BRANCH, tiles),
        step_of=step_of,
        in_specs=[pl.BlockSpec((GATES_TM, d), lambda b, i: (i, 0))] + w_specs,
        out_specs=[pl.BlockSpec((None, GATES_TM, d), lambda b, i: (b, i, 0))],
        out_shape=[jax.ShapeDtypeStruct((N_BRANCH, t, d), BF16)],
        args=(n,) + (w_in_b,) * GATE_BLOCKS,
        sides=[m(step_of) for m in make_sides],
        semantics=("arbitrary", "arbitrary"),
        name="gates",
    )
    return gates, side_out


def _mkv_body(m_ref, g_ref, w_ref, mk_ref, mv_ref):
    n = _rmsnorm_f32(m_ref[...], g_ref[...]).astype(BF16)
    z = _dot(n, w_ref[...]).astype(BF16)
    mk_ref[...] = z[:, :MEM_WIDTH]
    mv_ref[...] = z[:, MEM_WIDTH:]


def _mkv(mem, g, w):
    t, d = mem.shape
    return pl.pallas_call(
        _mkv_body,
        grid=(t // MEM_LEN,),
        in_specs=[
            pl.BlockSpec((MEM_LEN, d), lambda i: (i, 0)),
            pl.BlockSpec((1, d), lambda i: (0, 0)),
            pl.BlockSpec((d, 2 * MEM_WIDTH), lambda i: (0, 0), pipeline_mode=SINGLE),
        ],
        out_specs=[pl.BlockSpec((MEM_LEN, MEM_WIDTH), lambda i: (i, 0))] * 2,
        out_shape=[jax.ShapeDtypeStruct((t, MEM_WIDTH), BF16)] * 2,
        compiler_params=_params("parallel"),
        name="mem_kv",
    )(mem, g.reshape(1, d), w)


SWA_SLOPES = tuple(2.0 ** (-8.0 * (h + 1) / SWA_HEADS) for h in range(SWA_HEADS))


def _swa_body(sink_ref, q_ref, klo_ref, khi_ref, vlo_ref, vhi_ref,
              klo_p_ref, khi_p_ref, vlo_p_ref, vhi_p_ref, o_ref):
    w = WINDOW
    t = pl.program_id(1)
    i2 = lax.broadcasted_iota(jnp.int32, (2 * w, w), 0) & (w - 1)
    j2 = lax.broadcasted_iota(jnp.int32, (2 * w, w), 1)
    from_prev = j2 > i2
    dist = ((i2 - j2) & (w - 1)).astype(F32)
    no_prev = j2 > jnp.maximum(i2, jnp.where(t == 0, -1, w))
    low_half = j2 < HALF_LANES
    zero = jnp.zeros((2 * w, w), F32)
    ones_cols = (jnp.where(low_half, 1.0, 0.0).astype(BF16), jnp.where(low_half, 0.0, 1.0).astype(BF16))

    def rows2(top, bottom):
        return jnp.concatenate([jnp.full((w, w), top, F32), jnp.full((w, w), bottom, F32)], axis=0)

    for blk in range(SWA_TQ // w):
        rs = slice(blk * w, (blk + 1) * w)

        def band(cur_ref, prev_ref, cols):
            prev = prev_ref[:, cols] if blk == 0 else cur_ref[(blk - 1) * w:blk * w, cols]
            return jnp.concatenate([prev, cur_ref[rs, cols]], axis=0)

        for kvh in range(SWA_KV_HEADS):
            cols = slice(kvh * LANES, (kvh + 1) * LANES)
            q2 = jnp.concatenate([q_ref[rs, (2 * kvh) * LANES:(2 * kvh + 1) * LANES],
                                  q_ref[rs, (2 * kvh + 1) * LANES:(2 * kvh + 2) * LANES]], axis=0)
            keys = jnp.concatenate([band(klo_ref, klo_p_ref, cols), band(khi_ref, khi_p_ref, cols)], axis=0)
            s = _dot_nt(q2, keys)
            acc = None
            sink_terms = []
            for par, (v_ref, v_p_ref) in enumerate(((vlo_ref, vlo_p_ref), (vhi_ref, vhi_p_ref))):
                heads = (kvh * SWA_REP + par, kvh * SWA_REP + 2 + par)
                base = par * 2 * w
                sc = jnp.where(from_prev, s[:, base:base + w], s[:, base + w:base + 2 * w])
                sc = sc - rows2(SWA_SLOPES[heads[0]], SWA_SLOPES[heads[1]]) * dist
                if blk == 0:
                    sc = jnp.where(no_prev, NEG, sc)
                sink = rows2(sink_ref[heads[0]], sink_ref[heads[1]])
                m = jnp.maximum(jnp.broadcast_to(jnp.max(sc, axis=-1, keepdims=True), (2 * w, w)), sink)
                p = jnp.exp(sc - m)
                pcat = jnp.concatenate([jnp.where(from_prev, p, zero).astype(BF16),
                                        jnp.where(from_prev, zero, p).astype(BF16)], axis=1)
                vext = jnp.concatenate([band(v_ref, v_p_ref, cols), ones_cols[par]], axis=1)
                part = _dot(pcat, vext)
                acc = part if acc is None else acc + part
                sink_terms.append(jnp.exp(sink - m))
            denom = acc[:, w:] + jnp.where(low_half, sink_terms[0], sink_terms[1])
            out = (acc[:, :w] / denom).astype(BF16)
            o_ref[rs, (2 * kvh) * LANES:(2 * kvh + 1) * LANES] = out[:w]
            o_ref[rs, (2 * kvh + 1) * LANES:(2 * kvh + 2) * LANES] = out[w:]


def _swa(q, k_lo, k_hi, v_lo, v_hi, sinks, batch, seq):
    t = q.shape[0]
    steps = seq // SWA_TQ
    blocks_per_step = SWA_TQ // WINDOW
    blocks_per_seq = seq // WINDOW

    def cur(b, s):
        return (b * steps + s, 0)

    def prev(b, s):
        return (b * blocks_per_seq + jnp.maximum(s * blocks_per_step - 1, 0), 0)

    kv_cur = pl.BlockSpec((SWA_TQ, KV_PLACED_WIDTH), cur)
    kv_prev = pl.BlockSpec((WINDOW, KV_PLACED_WIDTH), prev)
    return pl.pallas_call(
        _swa_body,
        grid=(batch, steps),
        in_specs=[pl.BlockSpec(memory_space=pltpu.SMEM), pl.BlockSpec((SWA_TQ, SWA_WIDTH), cur)]
        + [kv_cur] * 4 + [kv_prev] * 4,
        out_specs=pl.BlockSpec((SWA_TQ, SWA_WIDTH), cur),
        out_shape=jax.ShapeDtypeStruct((t, SWA_WIDTH), BF16),
        compiler_params=_params("parallel", "parallel"),
        name="swa",
    )(sinks, q, k_lo, k_hi, v_lo, v_hi, k_lo, k_hi, v_lo, v_hi)


MEM_SCALE = 1.0 / math.sqrt(MEM_HEAD_DIM)


def _memattn_body(q_ref, mk_ref, mv_ref, o_ref):
    for h in range(MEM_HEADS):
        hs = slice(h * MEM_HEAD_DIM, (h + 1) * MEM_HEAD_DIM)
        s = _dot_nt(q_ref[:, hs], mk_ref[:, hs]) * MEM_SCALE
        m = jnp.max(s, axis=-1, keepdims=True)
        p = jnp.exp(s - m)
        probs = (p / jnp.sum(p, axis=-1, keepdims=True)).astype(BF16)
        o_ref[:, hs] = _dot(probs, mv_ref[:, hs]).astype(BF16)


def _memattn(mq, mk, mv, batch, seq):
    t = mq.shape[0]
    steps = seq // MEM_TQ
    return pl.pallas_call(
        _memattn_body,
        grid=(batch, steps),
        in_specs=[
            pl.BlockSpec((MEM_TQ, MEM_WIDTH), lambda b, s: (b * steps + s, 0)),
            pl.BlockSpec((MEM_LEN, MEM_WIDTH), lambda b, s: (b, 0)),
            pl.BlockSpec((MEM_LEN, MEM_WIDTH), lambda b, s: (b, 0)),
        ],
        out_specs=pl.BlockSpec((MEM_TQ, MEM_WIDTH), lambda b, s: (b * steps + s, 0)),
        out_shape=jax.ShapeDtypeStruct((t, MEM_WIDTH), BF16),
        compiler_params=_params("parallel", "parallel"),
        name="mem_attn",
    )(mq, mk, mv)


def _merge_body(h_ref, gt_ref, oa_ref, ob_ref, oc_ref, wbr_ref, wo_ref, o_ref):
    y = gt_ref[0].astype(F32) * _dot(oa_ref[...], wbr_ref[0])
    y += gt_ref[1].astype(F32) * _dot(ob_ref[...], wbr_ref[1])
    y += gt_ref[2].astype(F32) * _dot(oc_ref[...], wbr_ref[2])
    o_ref[...] = h_ref[...] + _dot(y.astype(BF16), wo_ref[...])


def _merge(h, gates, o_a, o_b, o_c, w_branch, w_out, make_sides):
    t, d = h.shape
    row = lambda i: (i, 0)
    (out,), side_out = _hosted_call(
        _merge_body,
        grid=(t // MERGE_TM,),
        step_of=lambda i: i,
        in_specs=[
            pl.BlockSpec((MERGE_TM, d), row),
            pl.BlockSpec((N_BRANCH, MERGE_TM, d), lambda i: (0, i, 0)),
            pl.BlockSpec((MERGE_TM, BRANCH_WIDTH), row),
            pl.BlockSpec((MERGE_TM, BRANCH_WIDTH), row),
            pl.BlockSpec((MERGE_TM, BRANCH_WIDTH), row),
            pl.BlockSpec((N_BRANCH, BRANCH_WIDTH, d), lambda i: (0, 0, 0), pipeline_mode=SINGLE),
            pl.BlockSpec((d, d), lambda i: (0, 0), pipeline_mode=SINGLE),
        ],
        out_specs=[pl.BlockSpec((MERGE_TM, d), row)],
        out_shape=[jax.ShapeDtypeStruct((t, d), F32)],
        args=(h, gates, o_a, o_b, o_c, w_branch, w_out),
        sides=[m(lambda i: i) for m in make_sides],
        semantics=("arbitrary",),
        name="merge",
    )
    return out, side_out


def kernel(x, mem, g_ffn1, w_ffn1_in, w_ffn1_out, g_mix, w_in, gmlp_ln_g, gmlp_ln_b, w_s, b_s, swa_sinks, g_mem, w_mem_kv, w_branch, w_out, g_ffn2, w_ffn2_in, w_ffn2_out, g_final):
    batch, seq, d = x.shape
    depth = w_in.shape[0]
    xt = x.reshape(batch * seq, d)
    memt = mem.reshape(batch * MEM_LEN, d)
    w_branch_rows = w_branch.reshape(depth, N_BRANCH * BRANCH_WIDTH, d)

    def mixer_casts(l):
        return [functools.partial(_side_rows, w_in, l, MIXER_CAST_ROWS),
                functools.partial(_side_rows, w_mem_kv, l, MIXER_CAST_ROWS),
                functools.partial(_side_rows, w_out, l, MIXER_CAST_ROWS),
                functools.partial(_side_rows, w_branch_rows, l, 2 * MIXER_CAST_ROWS)]

    def ffn_casts(w_in_ffn, w_out_ffn, l, rows_in=CAST_ROWS // 2, rows_out=CAST_ROWS):
        return [functools.partial(_side_ffn_in, w_in_ffn, l, rows_in),
                functools.partial(_side_rows, w_out_ffn, l, rows_out)]

    wab1, wo1 = [_cast_alone(m) for m in ffn_casts(w_ffn1_in, w_ffn1_out, 0, ALONE_IN_ROWS, ALONE_OUT_ROWS)]

    for l in range(depth):
        more = l + 1 < depth
        h, (w_in_b, w_mkv, w_o, w_br) = _ffn(xt, g_ffn1[l], wab1, wo1, g_final, False, mixer_casts(l))
        n, o_a = _uv(h, g_mix[l], w_in_b, gmlp_ln_g[l], gmlp_ln_b[l], w_s[l], b_s[l])
        q, k_lo, k_hi, v_lo, v_hi, mq = _qkvm(n, w_in_b)
        gates, (wab2, wo2) = _gates(n, w_in_b, ffn_casts(w_ffn2_in, w_ffn2_out, l))
        mk, mv = _mkv(memt, g_mem[l], w_mkv)
        o_b = _swa(q, k_lo, k_hi, v_lo, v_hi, swa_sinks[l], batch, seq)
        o_c = _memattn(mq, mk, mv, batch, seq)
        h, merge_side = _merge(h, gates, o_a, o_b, o_c, w_br.reshape(N_BRANCH, BRANCH_WIDTH, d), w_o,
                               ffn_casts(w_ffn1_in, w_ffn1_out, l + 1) if more else [])
        xt, _ = _ffn(h, g_ffn2[l], wab2, wo2, g_final, not more)
        if more:
            wab1, wo1 = merge_side
    return xt.reshape(batch, seq, d)
```

```python
import functools
import math
from typing import Callable, NamedTuple

import jax
import jax.numpy as jnp
from jax import lax
from jax.experimental import pallas as pl
from jax.experimental.pallas import tpu as pltpu

D_MODEL = 2048
MEM_LEN = 256
CHUNK = 128
WINDOW = 128
G_GROUPS = 4
G_WIDTH = 1024
G_GROUP_DIM = G_WIDTH // G_GROUPS
SWA_HEADS = 16
SWA_KV_HEADS = 4
SWA_HEAD_DIM = 64
SWA_REP = SWA_HEADS // SWA_KV_HEADS
SWA_WIDTH = SWA_HEADS * SWA_HEAD_DIM
SWA_KV_WIDTH = SWA_KV_HEADS * SWA_HEAD_DIM
MEM_HEADS = 4
MEM_HEAD_DIM = 256
MEM_WIDTH = MEM_HEADS * MEM_HEAD_DIM
N_BRANCH = 3
BRANCH_WIDTH = 1024
D_FF = 5504
EPS = 1e-6
NEG = -1e30

F32 = jnp.float32
BF16 = jnp.bfloat16

LANES = 128
VMEM_LIMIT_BYTES = 60 * 1024 * 1024

FFN_TM = 1024
FFN_TF = 1024
FFN_TF_HOST = 512
FFN_TN = 512
UV_TM = 1024
PROJ_TM = 1024
GATES_TM = 1024
MERGE_TM = 256
SWA_TQ = 1024
MEM_TQ = 2048
W_COLS = 512
CAST_ROWS = 128
MIXER_CAST_ROWS = 16
ALONE_IN_ROWS = 256
ALONE_OUT_ROWS = D_FF // 8

SINGLE = pl.Buffered(1)


def _params(*semantics):
    return pltpu.CompilerParams(dimension_semantics=semantics, vmem_limit_bytes=VMEM_LIMIT_BYTES)


def _rmsnorm_f32(xf, g):
    return xf * lax.rsqrt(jnp.mean(xf * xf, axis=-1, keepdims=True) + EPS) * g


def _dot(a, b):
    return jnp.dot(a, b, preferred_element_type=F32)


def _dot_nt(a, b):
    return lax.dot_general(a, b, (((1,), (1,)), ((), ())), preferred_element_type=F32)


class _SideCast(NamedTuple):
    srcs: tuple
    in_specs: tuple
    out_spec: pl.BlockSpec
    out_shape: jax.ShapeDtypeStruct
    write: Callable
    n_blocks: int


def _side_rows(w, l, rows_blk, step_of):
    _, rows, cols = w.shape
    n_blocks = rows // rows_blk
    assert rows % rows_blk == 0

    def blk(*g):
        return jnp.minimum(step_of(*g), n_blocks - 1)

    def write(step, in_refs, out_ref):
        @pl.when(step < n_blocks)
        def _():
            out_ref[...] = in_refs[0][...].astype(BF16)

    return _SideCast(
        (w,),
        (pl.BlockSpec((None, rows_blk, cols), lambda *g: (l, blk(*g), 0)),),
        pl.BlockSpec((rows_blk, cols), lambda *g: (blk(*g), 0)),
        jax.ShapeDtypeStruct((rows, cols), BF16), write, n_blocks)


def _side_ffn_in(w, l, rows_blk, step_of):
    _, d, _ = w.shape
    assert d % rows_blk == 0 and D_FF % LANES == 0
    n_blocks = d // rows_blk

    def blk(*g):
        return jnp.minimum(step_of(*g), n_blocks - 1)

    def write(step, in_refs, out_ref):
        @pl.when(step < n_blocks)
        def _():
            for half, in_ref in enumerate(in_refs):
                for k in range(D_FF // LANES):
                    dst = (2 * k + half) * LANES
                    out_ref[:, dst:dst + LANES] = in_ref[:, k * LANES:(k + 1) * LANES].astype(BF16)

    half_spec = lambda half: pl.BlockSpec((None, rows_blk, D_FF), lambda *g: (l, blk(*g), half))
    return _SideCast(
        (w, w),
        (half_spec(0), half_spec(1)),
        pl.BlockSpec((rows_blk, 2 * D_FF), lambda *g: (blk(*g), 0)),
        jax.ShapeDtypeStruct((d, 2 * D_FF), BF16), write, n_blocks)


def _hosted_call(body, *, grid, step_of, in_specs, out_specs, out_shape, args, sides, semantics, name,
                 scratch_shapes=()):
    n_in, n_out, k = len(in_specs), len(out_specs), len(sides)
    side_in_counts = [len(s.in_specs) for s in sides]
    n_side_in = sum(side_in_counts)
    n_steps = math.prod(grid)
    assert all(s.n_blocks <= n_steps for s in sides)

    def hosted(*refs):
        ins, side_ins = refs[:n_in], refs[n_in:n_in + n_side_in]
        outs = refs[n_in + n_side_in:n_in + n_side_in + n_out]
        side_outs = refs[n_in + n_side_in + n_out:n_in + n_side_in + n_out + k]
        scratch = refs[n_in + n_side_in + n_out + k:]
        if k:
            step = step_of(*[pl.program_id(a) for a in range(len(grid))])
            first = 0
            for side, count, o_ref in zip(sides, side_in_counts, side_outs):
                side.write(step, side_ins[first:first + count], o_ref)
                first += count
        body(*ins, *outs, *scratch)

    res = pl.pallas_call(
        hosted,
        grid=grid,
        in_specs=list(in_specs) + [spec for s in sides for spec in s.in_specs],
        out_specs=list(out_specs) + [s.out_spec for s in sides],
        out_shape=list(out_shape) + [s.out_shape for s in sides],
        scratch_shapes=list(scratch_shapes),
        compiler_params=_params(*semantics),
        name=name,
    )(*args, *[src for s in sides for src in s.srcs])
    return res[:n_out], res[n_out:]


def _cast_alone(make_side):
    side = make_side(lambda i: i)
    _, (out,) = _hosted_call(lambda: None, grid=(side.n_blocks,), step_of=lambda i: i, in_specs=[],
                             out_specs=[], out_shape=[], args=[], sides=[side],
                             semantics=("arbitrary",), name="cast")
    return out


def _ffn_step(n, w_ref, wo_ref, o_ref, units, acc_ref=None):
    acc_ref = o_ref if acc_ref is None else acc_ref
    pairs = units // LANES
    even = pairs - pairs % 2
    z = [_dot(n, w_ref[:, :2 * even * LANES])] if even else []
    if pairs % 2:
        half_rows = n.shape[0] // 2
        w_last = w_ref[:, 2 * even * LANES:2 * pairs * LANES]
        z.append(jnp.concatenate([_dot(n[:half_rows], w_last), _dot(n[half_rows:], w_last)], axis=0))
    z = jnp.concatenate(z, axis=1)
    hid = []
    for k in range(pairs):
        a = z[:, (2 * k) * LANES:(2 * k + 1) * LANES]
        b = z[:, (2 * k + 1) * LANES:(2 * k + 2) * LANES]
        half_a = 0.5 * a
        hid.append((0.5 * ((half_a * jnp.tanh(half_a) + half_a) * b)).astype(BF16))
    hid = jnp.concatenate(hid, axis=1)
    for c in range(o_ref.shape[1] // FFN_TN):
        cs = slice(c * FFN_TN, (c + 1) * FFN_TN)
        o_ref[:, cs] = acc_ref[:, cs] + _dot(hid, wo_ref[:units, cs])


def _ffn_body(x_hbm, g_ref, w_ref, wo_ref, gf_ref, o_ref, n_ref, x_ref, x_sem, *, apply_final_norm, tf):
    i, j = pl.program_id(0), pl.program_id(1)
    steps = pl.cdiv(D_FF, tf)
    last_tf = D_FF - (steps - 1) * tf
    assert steps > 2

    def x_copy(tile):
        return pltpu.make_async_copy(x_hbm.at[pl.ds(tile * FFN_TM, FFN_TM), :], x_ref, x_sem)

    @pl.when((i == 0) & (j == 0))
    def _():
        x_copy(0).start()

    @pl.when(j == 0)
    def _():
        x_copy(i).wait()
        n = _rmsnorm_f32(x_ref[...], g_ref[...]).astype(BF16)
        n_ref[...] = n
        _ffn_step(n, w_ref, wo_ref, o_ref, tf, acc_ref=x_ref)

    @pl.when((j == 1) & (i + 1 < pl.num_programs(0)))
    def _():
        x_copy(i + 1).start()

    @pl.when((j > 0) & (j < steps - 1))
    def _():
        _ffn_step(n_ref[...], w_ref, wo_ref, o_ref, tf)

    @pl.when(j == steps - 1)
    def _():
        _ffn_step(n_ref[...], w_ref, wo_ref, o_ref, last_tf)
        if apply_final_norm:
            o_ref[...] = _rmsnorm_f32(o_ref[...], gf_ref[...])


def _ffn(x, g, w_in_b, w_out_b, g_final, apply_final_norm, make_sides=()):
    t, d = x.shape
    tf = FFN_TF_HOST if make_sides else FFN_TF
    steps = pl.cdiv(D_FF, tf)
    step_of = lambda i, j: i * steps + j
    (out,), side_out = _hosted_call(
        functools.partial(_ffn_body, apply_final_norm=apply_final_norm, tf=tf),
        grid=(t // FFN_TM, steps),
        step_of=step_of,
        in_specs=[
            pl.BlockSpec(memory_space=pl.ANY),
            pl.BlockSpec((1, d), lambda i, j: (0, 0)),
            pl.BlockSpec((d, 2 * tf), lambda i, j: (0, j)),
            pl.BlockSpec((tf, d), lambda i, j: (j, 0)),
            pl.BlockSpec((1, d), lambda i, j: (0, 0)),
        ],
        out_specs=[pl.BlockSpec((FFN_TM, d), lambda i, j: (i, 0))],
        out_shape=[jax.ShapeDtypeStruct((t, d), F32)],
        args=(x, g.reshape(1, d), w_in_b, w_out_b, g_final.reshape(1, d)),
        sides=[m(step_of) for m in make_sides],
        semantics=("arbitrary", "arbitrary"),
        scratch_shapes=[pltpu.VMEM((FFN_TM, d), BF16), pltpu.VMEM((FFN_TM, d), F32),
                        pltpu.SemaphoreType.DMA(())],
        name="ffn",
    )
    return out, side_out


UV_END = 2 * G_WIDTH
QKVM_WIDTH = SWA_WIDTH + 2 * SWA_KV_WIDTH + MEM_WIDTH
QKVM_END = UV_END + QKVM_WIDTH


def _gelu(x):
    return 0.5 * x * (1.0 + lax.erf(x * math.sqrt(0.5)))


def _uv_body(h_ref, g_ref, wu_ref, wv_ref, lng_ref, lnb_ref, ws_ref, bst_ref, n_ref, oa_ref):
    row = lax.broadcasted_iota(jnp.int32, (CHUNK, CHUNK), 0)
    col = lax.broadcasted_iota(jnp.int32, (CHUNK, CHUNK), 1)
    causal = row >= col

    n = _rmsnorm_f32(h_ref[...], g_ref[...]).astype(BF16)
    n_ref[...] = n
    v = _gelu(_dot(n, wv_ref[...]))
    u = _gelu(_dot(n, wu_ref[...]))
    mu = jnp.mean(v, axis=-1, keepdims=True)
    vc = v - mu
    var = jnp.mean(vc * vc, axis=-1, keepdims=True)
    vn = (vc * lax.rsqrt(var + EPS) * lng_ref[...] + lnb_ref[...]).astype(BF16)
    for g in range(G_GROUPS):
        ws = jnp.where(causal, ws_ref[g], 0.0).astype(BF16)
        bias = bst_ref[:, g:g + 1]
        cs = slice(g * G_GROUP_DIM, (g + 1) * G_GROUP_DIM)
        for c in range(UV_TM // CHUNK):
            rs = slice(c * CHUNK, (c + 1) * CHUNK)
            mixed = _dot(ws, vn[rs, cs]) + bias
            oa_ref[rs, cs] = (u[rs, cs] * mixed).astype(BF16)


def _uv(h, g, w_in_b, ln_g, ln_b, w_s, b_s):
    t, d = h.shape
    const = lambda i: (0, 0)
    (n, o_a), _ = _hosted_call(
        _uv_body,
        grid=(t // UV_TM,),
        step_of=lambda i: i,
        in_specs=[
            pl.BlockSpec((UV_TM, d), lambda i: (i, 0)),
            pl.BlockSpec((1, d), const),
            pl.BlockSpec((d, G_WIDTH), lambda i: (0, 0), pipeline_mode=SINGLE),
            pl.BlockSpec((d, G_WIDTH), lambda i: (0, 1), pipeline_mode=SINGLE),
            pl.BlockSpec((1, G_WIDTH), const),
            pl.BlockSpec((1, G_WIDTH), const),
            pl.BlockSpec((G_GROUPS, CHUNK, CHUNK), lambda i: (0, 0, 0)),
            pl.BlockSpec((CHUNK, G_GROUPS), const),
        ],
        out_specs=[
            pl.BlockSpec((UV_TM, d), lambda i: (i, 0)),
            pl.BlockSpec((UV_TM, G_WIDTH), lambda i: (i, 0)),
        ],
        out_shape=[
            jax.ShapeDtypeStruct((t, d), BF16),
            jax.ShapeDtypeStruct((t, G_WIDTH), BF16),
        ],
        args=(h, g.reshape(1, d), w_in_b, w_in_b, ln_g.reshape(1, G_WIDTH), ln_b.reshape(1, G_WIDTH), w_s,
              jnp.transpose(b_s)),
        sides=[],
        semantics=("parallel",),
        name="uv_gmlp",
    )
    return n, o_a


SWA_SCALE = 1.0 / math.sqrt(SWA_HEAD_DIM)
HALF_LANES = LANES // 2
KV_PLACED_WIDTH = SWA_KV_HEADS * LANES
QKVM_BLOCKS = QKVM_WIDTH // W_COLS


def _place_heads(z):
    rows = z.shape[0]
    low_half = lax.broadcasted_iota(jnp.int32, (rows, LANES), 1) < HALF_LANES
    zero = jnp.zeros((rows, LANES), F32)
    low, high = [], []
    for pair in range(SWA_KV_HEADS // 2):
        zg = z[:, pair * LANES:(pair + 1) * LANES]
        swapped = pltpu.roll(zg, HALF_LANES, axis=1)
        low += [jnp.where(low_half, zg, zero), jnp.where(low_half, swapped, zero)]
        high += [jnp.where(low_half, zero, swapped), jnp.where(low_half, zero, zg)]
    return (jnp.concatenate(low, axis=1).astype(BF16), jnp.concatenate(high, axis=1).astype(BF16))


def _qkvm_body(n_ref, wq0_ref, wq1_ref, wkv_ref, wm0_ref, wm1_ref,
               q_ref, klo_ref, khi_ref, vlo_ref, vhi_ref, mq_ref):
    n = n_ref[...]
    q_ref[:, :W_COLS] = (_dot(n, wq0_ref[...]) * SWA_SCALE).astype(BF16)
    q_ref[:, W_COLS:] = (_dot(n, wq1_ref[...]) * SWA_SCALE).astype(BF16)
    kv = _dot(n, wkv_ref[...])
    klo_ref[...], khi_ref[...] = _place_heads(kv[:, :SWA_KV_WIDTH])
    vlo_ref[...], vhi_ref[...] = _place_heads(kv[:, SWA_KV_WIDTH:])
    mq_ref[:, :W_COLS] = _dot(n, wm0_ref[...]).astype(BF16)
    mq_ref[:, W_COLS:] = _dot(n, wm1_ref[...]).astype(BF16)


def _qkvm(n, w_in_b):
    t, d = n.shape
    assert SWA_WIDTH == 2 * W_COLS and 2 * SWA_KV_WIDTH == W_COLS and MEM_WIDTH == 2 * W_COLS
    first = UV_END // W_COLS
    widths = (SWA_WIDTH,) + (KV_PLACED_WIDTH,) * 4 + (MEM_WIDTH,)
    w_specs = [pl.BlockSpec((d, W_COLS), functools.partial(lambda c, i: (0, c), first + c),
                            pipeline_mode=SINGLE) for c in range(QKVM_BLOCKS)]
    outs, _ = _hosted_call(
        _qkvm_body,
        grid=(t // PROJ_TM,),
        step_of=lambda i: i,
        in_specs=[pl.BlockSpec((PROJ_TM, d), lambda i: (i, 0))] + w_specs,
        out_specs=[pl.BlockSpec((PROJ_TM, w_), lambda i: (i, 0)) for w_ in widths],
        out_shape=[jax.ShapeDtypeStruct((t, w_), BF16) for w_ in widths],
        args=(n,) + (w_in_b,) * QKVM_BLOCKS,
        sides=[],
        semantics=("parallel",),
        name="qkvm",
    )
    return outs


GATE_BLOCKS = D_MODEL // W_COLS


def _gates_body(n_ref, *refs):
    w_refs, o_ref = refs[:GATE_BLOCKS], refs[GATE_BLOCKS]
    n = n_ref[...]
    for c, w_ref in enumerate(w_refs):
        z = _dot(n, w_ref[...])
        o_ref[:, c * W_COLS:(c + 1) * W_COLS] = (0.5 * jnp.tanh(0.5 * z) + 0.5).astype(BF16)


def _gates(n, w_in_b, make_sides):
    t, d = n.shape
    tiles = t // GATES_TM
    first = QKVM_END // W_COLS
    step_of = lambda b, i: b * tiles + i
    w_specs = [pl.BlockSpec((d, W_COLS), functools.partial(lambda c, b, i: (0, first + b * GATE_BLOCKS + c), c))
               for c in range(GATE_BLOCKS)]
    (gates,), side_out = _hosted_call(
        _gates_body,
        grid=(N_BRANCH, tiles),
        step_of=step_of,
        in_specs=[pl.BlockSpec((GATES_TM, d), lambda b, i: (i, 0))] + w_specs,
        out_specs=[pl.BlockSpec((None, GATES_TM, d), lambda b, i: (b, i, 0))],
        out_shape=[jax.ShapeDtypeStruct((N_BRANCH, t, d), BF16)],
        args=(n,) + (w_in_b,) * GATE_BLOCKS,
        sides=[m(step_of) for m in make_sides],
        semantics=("arbitrary", "arbitrary"),
        name="gates",
    )
    return gates, side_out


def _mkv_body(m_ref, g_ref, w_ref, mk_ref, mv_ref):
    n = _rmsnorm_f32(m_ref[...], g_ref[...]).astype(BF16)
    z = _dot(n, w_ref[...]).astype(BF16)
    mk_ref[...] = z[:, :MEM_WIDTH]
    mv_ref[...] = z[:, MEM_WIDTH:]


def _mkv(mem, g, w):
    t, d = mem.shape
    return pl.pallas_call(
        _mkv_body,
        grid=(t // MEM_LEN,),
        in_specs=[
            pl.BlockSpec((MEM_LEN, d), lambda i: (i, 0)),
            pl.BlockSpec((1, d), lambda i: (0, 0)),
            pl.BlockSpec((d, 2 * MEM_WIDTH), lambda i: (0, 0), pipeline_mode=SINGLE),
        ],
        out_specs=[pl.BlockSpec((MEM_LEN, MEM_WIDTH), lambda i: (i, 0))] * 2,
        out_shape=[jax.ShapeDtypeStruct((t, MEM_WIDTH), BF16)] * 2,
        compiler_params=_params("parallel"),
        name="mem_kv",
    )(mem, g.reshape(1, d), w)


SWA_SLOPES = tuple(2.0 ** (-8.0 * (h + 1) / SWA_HEADS) for h in range(SWA_HEADS))


def _swa_body(sink_ref, q_ref, klo_ref, khi_ref, vlo_ref, vhi_ref,
              klo_p_ref, khi_p_ref, vlo_p_ref, vhi_p_ref, o_ref):
    w = WINDOW
    t = pl.program_id(1)
    i2 = lax.broadcasted_iota(jnp.int32, (2 * w, w), 0) & (w - 1)
    j2 = lax.broadcasted_iota(jnp.int32, (2 * w, w), 1)
    from_prev = j2 > i2
    dist = ((i2 - j2) & (w - 1)).astype(F32)
    no_prev = j2 > jnp.maximum(i2, jnp.where(t == 0, -1, w))
    low_half = j2 < HALF_LANES
    zero = jnp.zeros((2 * w, w), F32)
    ones_cols = (jnp.where(low_half, 1.0, 0.0).astype(BF16), jnp.where(low_half, 0.0, 1.0).astype(BF16))

    def rows2(top, bottom):
        return jnp.concatenate([jnp.full((w, w), top, F32), jnp.full((w, w), bottom, F32)], axis=0)

    for blk in range(SWA_TQ // w):
        rs = slice(blk * w, (blk + 1) * w)

        def band(cur_ref, prev_ref, cols):
            prev = prev_ref[:, cols] if blk == 0 else cur_ref[(blk - 1) * w:blk * w, cols]
            return jnp.concatenate([prev, cur_ref[rs, cols]], axis=0)

        for kvh in range(SWA_KV_HEADS):
            cols = slice(kvh * LANES, (kvh + 1) * LANES)
            q2 = jnp.concatenate([q_ref[rs, (2 * kvh) * LANES:(2 * kvh + 1) * LANES],
                                  q_ref[rs, (2 * kvh + 1) * LANES:(2 * kvh + 2) * LANES]], axis=0)
            keys = jnp.concatenate([band(klo_ref, klo_p_ref, cols), band(khi_ref, khi_p_ref, cols)], axis=0)
            s = _dot_nt(q2, keys)
            acc = None
            sink_terms = []
            for par, (v_ref, v_p_ref) in enumerate(((vlo_ref, vlo_p_ref), (vhi_ref, vhi_p_ref))):
                heads = (kvh * SWA_REP + par, kvh * SWA_REP + 2 + par)
                base = par * 2 * w
                sc = jnp.where(from_prev, s[:, base:base + w], s[:, base + w:base + 2 * w])
                sc = sc - rows2(SWA_SLOPES[heads[0]], SWA_SLOPES[heads[1]]) * dist
                if blk == 0:
                    sc = jnp.where(no_prev, NEG, sc)
                sink = rows2(sink_ref[heads[0]], sink_ref[heads[1]])
                m = jnp.maximum(jnp.broadcast_to(jnp.max(sc, axis=-1, keepdims=True), (2 * w, w)), sink)
                p = jnp.exp(sc - m)
                pcat = jnp.concatenate([jnp.where(from_prev, p, zero).astype(BF16),
                                        jnp.where(from_prev, zero, p).astype(BF16)], axis=1)
                vext = jnp.concatenate([band(v_ref, v_p_ref, cols), ones_cols[par]], axis=1)
                part = _dot(pcat, vext)
                acc = part if acc is None else acc + part
                sink_terms.append(jnp.exp(sink - m))
            denom = acc[:, w:] + jnp.where(low_half, sink_terms[0], sink_terms[1])
            out = (acc[:, :w] / denom).astype(BF16)
            o_ref[rs, (2 * kvh) * LANES:(2 * kvh + 1) * LANES] = out[:w]
            o_ref[rs, (2 * kvh + 1) * LANES:(2 * kvh + 2) * LANES] = out[w:]


def _swa(q, k_lo, k_hi, v_lo, v_hi, sinks, batch, seq):
    t = q.shape[0]
    steps = seq // SWA_TQ
    blocks_per_step = SWA_TQ // WINDOW
    blocks_per_seq = seq // WINDOW

    def cur(b, s):
        return (b * steps + s, 0)

    def prev(b, s):
        return (b * blocks_per_seq + jnp.maximum(s * blocks_per_step - 1, 0), 0)

    kv_cur = pl.BlockSpec((SWA_TQ, KV_PLACED_WIDTH), cur)
    kv_prev = pl.BlockSpec((WINDOW, KV_PLACED_WIDTH), prev)
    return pl.pallas_call(
        _swa_body,
        grid=(batch, steps),
        in_specs=[pl.BlockSpec(memory_space=pltpu.SMEM), pl.BlockSpec((SWA_TQ, SWA_WIDTH), cur)]
        + [kv_cur] * 4 + [kv_prev] * 4,
        out_specs=pl.BlockSpec((SWA_TQ, SWA_WIDTH), cur),
        out_shape=jax.ShapeDtypeStruct((t, SWA_WIDTH), BF16),
        compiler_params=_params("parallel", "parallel"),
        name="swa",
    )(sinks, q, k_lo, k_hi, v_lo, v_hi, k_lo, k_hi, v_lo, v_hi)


MEM_SCALE = 1.0 / math.sqrt(MEM_HEAD_DIM)


def _memattn_body(q_ref, mk_ref, mv_ref, o_ref):
    for h in range(MEM_HEADS):
        hs = slice(h * MEM_HEAD_DIM, (h + 1) * MEM_HEAD_DIM)
        s = _dot_nt(q_ref[:, hs], mk_ref[:, hs]) * MEM_SCALE
        m = jnp.max(s, axis=-1, keepdims=True)
        p = jnp.exp(s - m)
        probs = (p / jnp.sum(p, axis=-1, keepdims=True)).astype(BF16)
        o_ref[:, hs] = _dot(probs, mv_ref[:, hs]).astype(BF16)


def _memattn(mq, mk, mv, batch, seq):
    t = mq.shape[0]
    steps = seq // MEM_TQ
    return pl.pallas_call(
        _memattn_body,
        grid=(batch, steps),
        in_specs=[
            pl.BlockSpec((MEM_TQ, MEM_WIDTH), lambda b, s: (b * steps + s, 0)),
            pl.BlockSpec((MEM_LEN, MEM_WIDTH), lambda b, s: (b, 0)),
            pl.BlockSpec((MEM_LEN, MEM_WIDTH), lambda b, s: (b, 0)),
        ],
        out_specs=pl.BlockSpec((MEM_TQ, MEM_WIDTH), lambda b, s: (b * steps + s, 0)),
        out_shape=jax.ShapeDtypeStruct((t, MEM_WIDTH), BF16),
        compiler_params=_params("parallel", "parallel"),
        name="mem_attn",
    )(mq, mk, mv)


def _merge_body(h_ref, gt_ref, oa_ref, ob_ref, oc_ref, wbr_ref, wo_ref, o_ref):
    y = gt_ref[0].astype(F32) * _dot(oa_ref[...], wbr_ref[0])
    y += gt_ref[1].astype(F32) * _dot(ob_ref[...], wbr_ref[1])
    y += gt_ref[2].astype(F32) * _dot(oc_ref[...], wbr_ref[2])
    o_ref[...] = h_ref[...] + _dot(y.astype(BF16), wo_ref[...])


def _merge(h, gates, o_a, o_b, o_c, w_branch, w_out, make_sides):
    t, d = h.shape
    row = lambda i: (i, 0)
    (out,), side_out = _hosted_call(
        _merge_body,
        grid=(t // MERGE_TM,),
        step_of=lambda i: i,
        in_specs=[
            pl.BlockSpec((MERGE_TM, d), row),
            pl.BlockSpec((N_BRANCH, MERGE_TM, d), lambda i: (0, i, 0)),
            pl.BlockSpec((MERGE_TM, BRANCH_WIDTH), row),
            pl.BlockSpec((MERGE_TM, BRANCH_WIDTH), row),
            pl.BlockSpec((MERGE_TM, BRANCH_WIDTH), row),
            pl.BlockSpec((N_BRANCH, BRANCH_WIDTH, d), lambda i: (0, 0, 0), pipeline_mode=SINGLE),
            pl.BlockSpec((d, d), lambda i: (0, 0), pipeline_mode=SINGLE),
        ],
        out_specs=[pl.BlockSpec((MERGE_TM, d), row)],
        out_shape=[jax.ShapeDtypeStruct((t, d), F32)],
        args=(h, gates, o_a, o_b, o_c, w_branch, w_out),
        sides=[m(lambda i: i) for m in make_sides],
        semantics=("arbitrary",),
        name="merge",
    )
    return out, side_out


def kernel(x, mem, g_ffn1, w_ffn1_in, w_ffn1_out, g_mix, w_in, gmlp_ln_g, gmlp_ln_b, w_s, b_s, swa_sinks, g_mem, w_mem_kv, w_branch, w_out, g_ffn2, w_ffn2_in, w_ffn2_out, g_final):
    batch, seq, d = x.shape
    depth = w_in.shape[0]
    xt = x.reshape(batch * seq, d)
    memt = mem.reshape(batch * MEM_LEN, d)
    w_branch_rows = w_branch.reshape(depth, N_BRANCH * BRANCH_WIDTH, d)

    def mixer_casts(l):
        return [functools.partial(_side_rows, w_in, l, MIXER_CAST_ROWS),
                functools.partial(_side_rows, w_mem_kv, l, MIXER_CAST_ROWS),
                functools.partial(_side_rows, w_out, l, MIXER_CAST_ROWS),
                functools.partial(_side_rows, w_branch_rows, l, 2 * MIXER_CAST_ROWS)]

    def ffn_casts(w_in_ffn, w_out_ffn, l, rows_in=CAST_ROWS // 2, rows_out=CAST_ROWS):
        return [functools.partial(_side_ffn_in, w_in_ffn, l, rows_in),
                functools.partial(_side_rows, w_out_ffn, l, rows_out)]

    wab1, wo1 = [_cast_alone(m) for m in ffn_casts(w_ffn1_in, w_ffn1_out, 0, ALONE_IN_ROWS, ALONE_OUT_ROWS)]

    for l in range(depth):
        more = l + 1 < depth
        h, (w_in_b, w_mkv, w_o, w_br) = _ffn(xt, g_ffn1[l], wab1, wo1, g_final, False, mixer_casts(l))
        n, o_a = _uv(h, g_mix[l], w_in_b, gmlp_ln_g[l], gmlp_ln_b[l], w_s[l], b_s[l])
        q, k_lo, k_hi, v_lo, v_hi, mq = _qkvm(n, w_in_b)
        gates, (wab2, wo2) = _gates(n, w_in_b, ffn_casts(w_ffn2_in, w_ffn2_out, l))
        mk, mv = _mkv(memt, g_mem[l], w_mkv)
        o_b = _swa(q, k_lo, k_hi, v_lo, v_hi, swa_sinks[l], batch, seq)
        o_c = _memattn(mq, mk, mv, batch, seq)
        h, merge_side = _merge(h, gates, o_a, o_b, o_c, w_br.reshape(N_BRANCH, BRANCH_WIDTH, d), w_o,
                               ffn_casts(w_ffn1_in, w_ffn1_out, l + 1) if more else [])
        xt, _ = _ffn(h, g_ffn2[l], wab2, wo2, g_final, not more)
        if more:
            wab1, wo1 = merge_side
    return xt.reshape(batch, seq, d)
```

```python
import functools
import math
from typing import Callable, NamedTuple

import jax
import jax.numpy as jnp
from jax import lax
from jax.experimental import pallas as pl
from jax.experimental.pallas import tpu as pltpu

D_MODEL = 2048
MEM_LEN = 256
CHUNK = 128
WINDOW = 128
G_GROUPS = 4
G_WIDTH = 1024
G_GROUP_DIM = G_WIDTH // G_GROUPS
SWA_HEADS = 16
SWA_KV_HEADS = 4
SWA_HEAD_DIM = 64
SWA_REP = SWA_HEADS // SWA_KV_HEADS
SWA_WIDTH = SWA_HEADS * SWA_HEAD_DIM
SWA_KV_WIDTH = SWA_KV_HEADS * SWA_HEAD_DIM
MEM_HEADS = 4
MEM_HEAD_DIM = 256
MEM_WIDTH = MEM_HEADS * MEM_HEAD_DIM
N_BRANCH = 3
BRANCH_WIDTH = 1024
D_FF = 5504
EPS = 1e-6
NEG = -1e30

F32 = jnp.float32
BF16 = jnp.bfloat16

LANES = 128
VMEM_LIMIT_BYTES = 60 * 1024 * 1024

FFN_TM = 1024
FFN_TF = 1024
FFN_TF_HOST = 768
FFN_TN = 512
UV_TM = 1024
PROJ_TM = 1024
GATES_TM = 1024
MERGE_TM = 256
SWA_TQ = 1024
MEM_TQ = 2048
W_COLS = 512
CAST_ROWS = 128
MIXER_CAST_ROWS = 16
ALONE_IN_ROWS = 256
ALONE_OUT_ROWS = D_FF // 8

SINGLE = pl.Buffered(1)


def _params(*semantics):
    return pltpu.CompilerParams(dimension_semantics=semantics, vmem_limit_bytes=VMEM_LIMIT_BYTES)


def _rmsnorm_f32(xf, g):
    return xf * lax.rsqrt(jnp.mean(xf * xf, axis=-1, keepdims=True) + EPS) * g


def _dot(a, b):
    return jnp.dot(a, b, preferred_element_type=F32)


def _dot_nt(a, b):
    return lax.dot_general(a, b, (((1,), (1,)), ((), ())), preferred_element_type=F32)


class _SideCast(NamedTuple):
    srcs: tuple
    in_specs: tuple
    out_spec: pl.BlockSpec
    out_shape: jax.ShapeDtypeStruct
    write: Callable
    n_blocks: int


def _side_rows(w, l, rows_blk, step_of):
    _, rows, cols = w.shape
    n_blocks = rows // rows_blk
    assert rows % rows_blk == 0

    def blk(*g):
        return jnp.minimum(step_of(*g), n_blocks - 1)

    def write(step, in_refs, out_ref):
        @pl.when(step < n_blocks)
        def _():
            out_ref[...] = in_refs[0][...].astype(BF16)

    return _SideCast(
        (w,),
        (pl.BlockSpec((None, rows_blk, cols), lambda *g: (l, blk(*g), 0)),),
        pl.BlockSpec((rows_blk, cols), lambda *g: (blk(*g), 0)),
        jax.ShapeDtypeStruct((rows, cols), BF16), write, n_blocks)


def _side_ffn_in(w, l, rows_blk, step_of):
    _, d, _ = w.shape
    assert d % rows_blk == 0 and D_FF % LANES == 0
    n_blocks = d // rows_blk

    def blk(*g):
        return jnp.minimum(step_of(*g), n_blocks - 1)

    def write(step, in_refs, out_ref):
        @pl.when(step < n_blocks)
        def _():
            for half, in_ref in enumerate(in_refs):
                for k in range(D_FF // LANES):
                    dst = (2 * k + half) * LANES
                    out_ref[:, dst:dst + LANES] = in_ref[:, k * LANES:(k + 1) * LANES].astype(BF16)

    half_spec = lambda half: pl.BlockSpec((None, rows_blk, D_FF), lambda *g: (l, blk(*g), half))
    return _SideCast(
        (w, w),
        (half_spec(0), half_spec(1)),
        pl.BlockSpec((rows_blk, 2 * D_FF), lambda *g: (blk(*g), 0)),
        jax.ShapeDtypeStruct((d, 2 * D_FF), BF16), write, n_blocks)


def _hosted_call(body, *, grid, step_of, in_specs, out_specs, out_shape, args, sides, semantics, name,
                 scratch_shapes=()):
    n_in, n_out, k = len(in_specs), len(out_specs), len(sides)
    side_in_counts = [len(s.in_specs) for s in sides]
    n_side_in = sum(side_in_counts)
    n_steps = math.prod(grid)
    assert all(s.n_blocks <= n_steps for s in sides)

    def hosted(*refs):
        ins, side_ins = refs[:n_in], refs[n_in:n_in + n_side_in]
        outs = refs[n_in + n_side_in:n_in + n_side_in + n_out]
        side_outs = refs[n_in + n_side_in + n_out:n_in + n_side_in + n_out + k]
        scratch = refs[n_in + n_side_in + n_out + k:]
        if k:
            step = step_of(*[pl.program_id(a) for a in range(len(grid))])
            first = 0
            for side, count, o_ref in zip(sides, side_in_counts, side_outs):
                side.write(step, side_ins[first:first + count], o_ref)
                first += count
        body(*ins, *outs, *scratch)

    res = pl.pallas_call(
        hosted,
        grid=grid,
        in_specs=list(in_specs) + [spec for s in sides for spec in s.in_specs],
        out_specs=list(out_specs) + [s.out_spec for s in sides],
        out_shape=list(out_shape) + [s.out_shape for s in sides],
        scratch_shapes=list(scratch_shapes),
        compiler_params=_params(*semantics),
        name=name,
    )(*args, *[src for s in sides for src in s.srcs])
    return res[:n_out], res[n_out:]


def _cast_alone(make_side):
    side = make_side(lambda i: i)
    _, (out,) = _hosted_call(lambda: None, grid=(side.n_blocks,), step_of=lambda i: i, in_specs=[],
                             out_specs=[], out_shape=[], args=[], sides=[side],
                             semantics=("arbitrary",), name="cast")
    return out


def _ffn_step(n, w_ref, wo_ref, o_ref, units, acc_ref=None):
    acc_ref = o_ref if acc_ref is None else acc_ref
    pairs = units // LANES
    even = pairs - pairs % 2
    z = [_dot(n, w_ref[:, :2 * even * LANES])] if even else []
    if pairs % 2:
        half_rows = n.shape[0] // 2
        w_last = w_ref[:, 2 * even * LANES:2 * pairs * LANES]
        z.append(jnp.concatenate([_dot(n[:half_rows], w_last), _dot(n[half_rows:], w_last)], axis=0))
    z = jnp.concatenate(z, axis=1)
    hid = []
    for k in range(pairs):
        a = z[:, (2 * k) * LANES:(2 * k + 1) * LANES]
        b = z[:, (2 * k + 1) * LANES:(2 * k + 2) * LANES]
        half_a = 0.5 * a
        hid.append((0.5 * ((half_a * jnp.tanh(half_a) + half_a) * b)).astype(BF16))
    hid = jnp.concatenate(hid, axis=1)
    for c in range(o_ref.shape[1] // FFN_TN):
        cs = slice(c * FFN_TN, (c + 1) * FFN_TN)
        o_ref[:, cs] = acc_ref[:, cs] + _dot(hid, wo_ref[:units, cs])


def _ffn_body(x_hbm, g_ref, w_ref, wo_ref, gf_ref, o_ref, n_ref, x_ref, x_sem, *, apply_final_norm, tf):
    i, j = pl.program_id(0), pl.program_id(1)
    steps = pl.cdiv(D_FF, tf)
    last_tf = D_FF - (steps - 1) * tf
    assert steps > 2

    def x_copy(tile):
        return pltpu.make_async_copy(x_hbm.at[pl.ds(tile * FFN_TM, FFN_TM), :], x_ref, x_sem)

    @pl.when((i == 0) & (j == 0))
    def _():
        x_copy(0).start()

    @pl.when(j == 0)
    def _():
        x_copy(i).wait()
        n = _rmsnorm_f32(x_ref[...], g_ref[...]).astype(BF16)
        n_ref[...] = n
        _ffn_step(n, w_ref, wo_ref, o_ref, tf, acc_ref=x_ref)

    @pl.when((j == 1) & (i + 1 < pl.num_programs(0)))
    def _():
        x_copy(i + 1).start()

    @pl.when((j > 0) & (j < steps - 1))
    def _():
        _ffn_step(n_ref[...], w_ref, wo_ref, o_ref, tf)

    @pl.when(j == steps - 1)
    def _():
        _ffn_step(n_ref[...], w_ref, wo_ref, o_ref, last_tf)
        if apply_final_norm:
            o_ref[...] = _rmsnorm_f32(o_ref[...], gf_ref[...])


def _ffn(x, g, w_in_b, w_out_b, g_final, apply_final_norm, make_sides=()):
    t, d = x.shape
    tf = FFN_TF_HOST if make_sides else FFN_TF
    steps = pl.cdiv(D_FF, tf)
    step_of = lambda i, j: i * steps + j
    (out,), side_out = _hosted_call(
        functools.partial(_ffn_body, apply_final_norm=apply_final_norm, tf=tf),
        grid=(t // FFN_TM, steps),
        step_of=step_of,
        in_specs=[
            pl.BlockSpec(memory_space=pl.ANY),
            pl.BlockSpec((1, d), lambda i, j: (0, 0)),
            pl.BlockSpec((d, 2 * tf), lambda i, j: (0, j)),
            pl.BlockSpec((tf, d), lambda i, j: (j, 0)),
            pl.BlockSpec((1, d), lambda i, j: (0, 0)),
        ],
        out_specs=[pl.BlockSpec((FFN_TM, d), lambda i, j: (i, 0))],
        out_shape=[jax.ShapeDtypeStruct((t, d), F32)],
        args=(x, g.reshape(1, d), w_in_b, w_out_b, g_final.reshape(1, d)),
        sides=[m(step_of) for m in make_sides],
        semantics=("arbitrary", "arbitrary"),
        scratch_shapes=[pltpu.VMEM((FFN_TM, d), BF16), pltpu.VMEM((FFN_TM, d), F32),
                        pltpu.SemaphoreType.DMA(())],
        name="ffn",
    )
    return out, side_out


UV_END = 2 * G_WIDTH
QKVM_WIDTH = SWA_WIDTH + 2 * SWA_KV_WIDTH + MEM_WIDTH
QKVM_END = UV_END + QKVM_WIDTH


def _gelu(x):
    return 0.5 * x * (1.0 + lax.erf(x * math.sqrt(0.5)))


def _uv_body(h_ref, g_ref, wu_ref, wv_ref, lng_ref, lnb_ref, ws_ref, bst_ref, n_ref, oa_ref):
    row = lax.broadcasted_iota(jnp.int32, (CHUNK, CHUNK), 0)
    col = lax.broadcasted_iota(jnp.int32, (CHUNK, CHUNK), 1)
    causal = row >= col

    n = _rmsnorm_f32(h_ref[...], g_ref[...]).astype(BF16)
    n_ref[...] = n
    v = _gelu(_dot(n, wv_ref[...]))
    u = _gelu(_dot(n, wu_ref[...]))
    mu = jnp.mean(v, axis=-1, keepdims=True)
    vc = v - mu
    var = jnp.mean(vc * vc, axis=-1, keepdims=True)
    vn = (vc * lax.rsqrt(var + EPS) * lng_ref[...] + lnb_ref[...]).astype(BF16)
    for g in range(G_GROUPS):
        ws = jnp.where(causal, ws_ref[g], 0.0).astype(BF16)
        bias = bst_ref[:, g:g + 1]
        cs = slice(g * G_GROUP_DIM, (g + 1) * G_GROUP_DIM)
        for c in range(UV_TM // CHUNK):
            rs = slice(c * CHUNK, (c + 1) * CHUNK)
            mixed = _dot(ws, vn[rs, cs]) + bias
            oa_ref[rs, cs] = (u[rs, cs] * mixed).astype(BF16)


def _uv(h, g, w_in_b, ln_g, ln_b, w_s, b_s):
    t, d = h.shape
    const = lambda i: (0, 0)
    (n, o_a), _ = _hosted_call(
        _uv_body,
        grid=(t // UV_TM,),
        step_of=lambda i: i,
        in_specs=[
            pl.BlockSpec((UV_TM, d), lambda i: (i, 0)),
            pl.BlockSpec((1, d), const),
            pl.BlockSpec((d, G_WIDTH), lambda i: (0, 0), pipeline_mode=SINGLE),
            pl.BlockSpec((d, G_WIDTH), lambda i: (0, 1), pipeline_mode=SINGLE),
            pl.BlockSpec((1, G_WIDTH), const),
            pl.BlockSpec((1, G_WIDTH), const),
            pl.BlockSpec((G_GROUPS, CHUNK, CHUNK), lambda i: (0, 0, 0)),
            pl.BlockSpec((CHUNK, G_GROUPS), const),
        ],
        out_specs=[
            pl.BlockSpec((UV_TM, d), lambda i: (i, 0)),
            pl.BlockSpec((UV_TM, G_WIDTH), lambda i: (i, 0)),
        ],
        out_shape=[
            jax.ShapeDtypeStruct((t, d), BF16),
            jax.ShapeDtypeStruct((t, G_WIDTH), BF16),
        ],
        args=(h, g.reshape(1, d), w_in_b, w_in_b, ln_g.reshape(1, G_WIDTH), ln_b.reshape(1, G_WIDTH), w_s,
              jnp.transpose(b_s)),
        sides=[],
        semantics=("parallel",),
        name="uv_gmlp",
    )
    return n, o_a


SWA_SCALE = 1.0 / math.sqrt(SWA_HEAD_DIM)
HALF_LANES = LANES // 2
KV_PLACED_WIDTH = SWA_KV_HEADS * LANES
QKVM_BLOCKS = QKVM_WIDTH // W_COLS


def _place_heads(z):
    rows = z.shape[0]
    low_half = lax.broadcasted_iota(jnp.int32, (rows, LANES), 1) < HALF_LANES
    zero = jnp.zeros((rows, LANES), F32)
    low, high = [], []
    for pair in range(SWA_KV_HEADS // 2):
        zg = z[:, pair * LANES:(pair + 1) * LANES]
        swapped = pltpu.roll(zg, HALF_LANES, axis=1)
        low += [jnp.where(low_half, zg, zero), jnp.where(low_half, swapped, zero)]
        high += [jnp.where(low_half, zero, swapped), jnp.where(low_half, zero, zg)]
    return (jnp.concatenate(low, axis=1).astype(BF16), jnp.concatenate(high, axis=1).astype(BF16))


def _qkvm_body(n_ref, wq0_ref, wq1_ref, wkv_ref, wm0_ref, wm1_ref,
               q_ref, klo_ref, khi_ref, vlo_ref, vhi_ref, mq_ref):
    n = n_ref[...]
    q_ref[:, :W_COLS] = (_dot(n, wq0_ref[...]) * SWA_SCALE).astype(BF16)
    q_ref[:, W_COLS:] = (_dot(n, wq1_ref[...]) * SWA_SCALE).astype(BF16)
    kv = _dot(n, wkv_ref[...])
    klo_ref[...], khi_ref[...] = _place_heads(kv[:, :SWA_KV_WIDTH])
    vlo_ref[...], vhi_ref[...] = _place_heads(kv[:, SWA_KV_WIDTH:])
    mq_ref[:, :W_COLS] = _dot(n, wm0_ref[...]).astype(BF16)
    mq_ref[:, W_COLS:] = _dot(n, wm1_ref[...]).astype(BF16)


def _qkvm(n, w_in_b):
    t, d = n.shape
    assert SWA_WIDTH == 2 * W_COLS and 2 * SWA_KV_WIDTH == W_COLS and MEM_WIDTH == 2 * W_COLS
    first = UV_END // W_COLS
    widths = (SWA_WIDTH,) + (KV_PLACED_WIDTH,) * 4 + (MEM_WIDTH,)
    w_specs = [pl.BlockSpec((d, W_COLS), functools.partial(lambda c, i: (0, c), first + c),
                            pipeline_mode=SINGLE) for c in range(QKVM_BLOCKS)]
    outs, _ = _hosted_call(
        _qkvm_body,
        grid=(t // PROJ_TM,),
        step_of=lambda i: i,
        in_specs=[pl.BlockSpec((PROJ_TM, d), lambda i: (i, 0))] + w_specs,
        out_specs=[pl.BlockSpec((PROJ_TM, w_), lambda i: (i, 0)) for w_ in widths],
        out_shape=[jax.ShapeDtypeStruct((t, w_), BF16) for w_ in widths],
        args=(n,) + (w_in_b,) * QKVM_BLOCKS,
        sides=[],
        semantics=("parallel",),
        name="qkvm",
    )
    return outs


GATE_BLOCKS = D_MODEL // W_COLS


def _gates_body(n_ref, *refs):
    w_refs, o_ref = refs[:GATE_BLOCKS], refs[GATE_BLOCKS]
    n = n_ref[...]
    for c, w_ref in enumerate(w_refs):
        z = _dot(n, w_ref[...])
        o_ref[:, c * W_COLS:(c + 1) * W_COLS] = (0.5 * jnp.tanh(0.5 * z) + 0.5).astype(BF16)


def _gates(n, w_in_b, make_sides):
    t, d = n.shape
    tiles = t // GATES_TM
    first = QKVM_END // W_COLS
    step_of = lambda b, i: b * tiles + i
    w_specs = [pl.BlockSpec((d, W_COLS), functools.partial(lambda c, b, i: (0, first + b * GATE_BLOCKS + c), c))
               for c in range(GATE_BLOCKS)]
    (gates,), side_out = _hosted_call(
        _gates_body,
        grid=(N_BRANCH, tiles),
        step_of=step_of,
        in_specs=[pl.BlockSpec((GATES_TM, d), lambda b, i: (i, 0))] + w_specs,
        out_specs=[pl.BlockSpec((None, GATES_TM, d), lambda b, i: (b, i, 0))],
        out_shape=[jax.ShapeDtypeStruct((N_BRANCH, t, d), BF16)],
        args=(n,) + (w_in_b,) * GATE_BLOCKS,
        sides=[m(step_of) for m in make_sides],
        semantics=("arbitrary", "arbitrary"),
        name="gates",
    )
    return gates, side_out


def _mkv_body(m_ref, g_ref, w_ref, mk_ref, mv_ref):
    n = _rmsnorm_f32(m_ref[...], g_ref[...]).astype(BF16)
    z = _dot(n, w_ref[...]).astype(BF16)
    mk_ref[...] = z[:, :MEM_WIDTH]
    mv_ref[...] = z[:, MEM_WIDTH:]


def _mkv(mem, g, w):
    t, d = mem.shape
    return pl.pallas_call(
        _mkv_body,
        grid=(t // MEM_LEN,),
        in_specs=[
            pl.BlockSpec((MEM_LEN, d), lambda i: (i, 0)),
            pl.BlockSpec((1, d), lambda i: (0, 0)),
            pl.BlockSpec((d, 2 * MEM_WIDTH), lambda i: (0, 0), pipeline_mode=SINGLE),
        ],
        out_specs=[pl.BlockSpec((MEM_LEN, MEM_WIDTH), lambda i: (i, 0))] * 2,
        out_shape=[jax.ShapeDtypeStruct((t, MEM_WIDTH), BF16)] * 2,
        compiler_params=_params("parallel"),
        name="mem_kv",
    )(mem, g.reshape(1, d), w)


SWA_SLOPES = tuple(2.0 ** (-8.0 * (h + 1) / SWA_HEADS) for h in range(SWA_HEADS))


def _swa_body(sink_ref, q_ref, klo_ref, khi_ref, vlo_ref, vhi_ref,
              klo_p_ref, khi_p_ref, vlo_p_ref, vhi_p_ref, o_ref):
    w = WINDOW
    t = pl.program_id(1)
    i2 = lax.broadcasted_iota(jnp.int32, (2 * w, w), 0) & (w - 1)
    j2 = lax.broadcasted_iota(jnp.int32, (2 * w, w), 1)
    from_prev = j2 > i2
    dist = ((i2 - j2) & (w - 1)).astype(F32)
    no_prev = j2 > jnp.maximum(i2, jnp.where(t == 0, -1, w))
    low_half = j2 < HALF_LANES
    zero = jnp.zeros((2 * w, w), F32)
    ones_cols = (jnp.where(low_half, 1.0, 0.0).astype(BF16), jnp.where(low_half, 0.0, 1.0).astype(BF16))

    def rows2(top, bottom):
        return jnp.concatenate([jnp.full((w, w), top, F32), jnp.full((w, w), bottom, F32)], axis=0)

    for blk in range(SWA_TQ // w):
        rs = slice(blk * w, (blk + 1) * w)

        def band(cur_ref, prev_ref, cols):
            prev = prev_ref[:, cols] if blk == 0 else cur_ref[(blk - 1) * w:blk * w, cols]
            return jnp.concatenate([prev, cur_ref[rs, cols]], axis=0)

        for kvh in range(SWA_KV_HEADS):
            cols = slice(kvh * LANES, (kvh + 1) * LANES)
            q2 = jnp.concatenate([q_ref[rs, (2 * kvh) * LANES:(2 * kvh + 1) * LANES],
                                  q_ref[rs, (2 * kvh + 1) * LANES:(2 * kvh + 2) * LANES]], axis=0)
            keys = jnp.concatenate([band(klo_ref, klo_p_ref, cols), band(khi_ref, khi_p_ref, cols)], axis=0)
            s = _dot_nt(q2, keys)
            acc = None
            sink_terms = []
            for par, (v_ref, v_p_ref) in enumerate(((vlo_ref, vlo_p_ref), (vhi_ref, vhi_p_ref))):
                heads = (kvh * SWA_REP + par, kvh * SWA_REP + 2 + par)
                base = par * 2 * w
                sc = jnp.where(from_prev, s[:, base:base + w], s[:, base + w:base + 2 * w])
                sc = sc - rows2(SWA_SLOPES[heads[0]], SWA_SLOPES[heads[1]]) * dist
                if blk == 0:
                    sc = jnp.where(no_prev, NEG, sc)
                sink = rows2(sink_ref[heads[0]], sink_ref[heads[1]])
                m = jnp.maximum(jnp.broadcast_to(jnp.max(sc, axis=-1, keepdims=True), (2 * w, w)), sink)
                p = jnp.exp(sc - m)
                pcat = jnp.concatenate([jnp.where(from_prev, p, zero).astype(BF16),
                                        jnp.where(from_prev, zero, p).astype(BF16)], axis=1)
                vext = jnp.concatenate([band(v_ref, v_p_ref, cols), ones_cols[par]], axis=1)
                part = _dot(pcat, vext)
                acc = part if acc is None else acc + part
                sink_terms.append(jnp.exp(sink - m))
            denom = acc[:, w:] + jnp.where(low_half, sink_terms[0], sink_terms[1])
            out = (acc[:, :w] / denom).astype(BF16)
            o_ref[rs, (2 * kvh) * LANES:(2 * kvh + 1) * LANES] = out[:w]
            o_ref[rs, (2 * kvh + 1) * LANES:(2 * kvh + 2) * LANES] = out[w:]


def _swa(q, k_lo, k_hi, v_lo, v_hi, sinks, batch, seq):
    t = q.shape[0]
    steps = seq // SWA_TQ
    blocks_per_step = SWA_TQ // WINDOW
    blocks_per_seq = seq // WINDOW

    def cur(b, s):
        return (b * steps + s, 0)

    def prev(b, s):
        return (b * blocks_per_seq + jnp.maximum(s * blocks_per_step - 1, 0), 0)

    kv_cur = pl.BlockSpec((SWA_TQ, KV_PLACED_WIDTH), cur)
    kv_prev = pl.BlockSpec((WINDOW, KV_PLACED_WIDTH), prev)
    return pl.pallas_call(
        _swa_body,
        grid=(batch, steps),
        in_specs=[pl.BlockSpec(memory_space=pltpu.SMEM), pl.BlockSpec((SWA_TQ, SWA_WIDTH), cur)]
        + [kv_cur] * 4 + [kv_prev] * 4,
        out_specs=pl.BlockSpec((SWA_TQ, SWA_WIDTH), cur),
        out_shape=jax.ShapeDtypeStruct((t, SWA_WIDTH), BF16),
        compiler_params=_params("parallel", "parallel"),
        name="swa",
    )(sinks, q, k_lo, k_hi, v_lo, v_hi, k_lo, k_hi, v_lo, v_hi)


MEM_SCALE = 1.0 / math.sqrt(MEM_HEAD_DIM)


def _memattn_body(q_ref, mk_ref, mv_ref, o_ref):
    for h in range(MEM_HEADS):
        hs = slice(h * MEM_HEAD_DIM, (h + 1) * MEM_HEAD_DIM)
        s = _dot_nt(q_ref[:, hs], mk_ref[:, hs]) * MEM_SCALE
        m = jnp.max(s, axis=-1, keepdims=True)
        p = jnp.exp(s - m)
        probs = (p / jnp.sum(p, axis=-1, keepdims=True)).astype(BF16)
        o_ref[:, hs] = _dot(probs, mv_ref[:, hs]).astype(BF16)


def _memattn(mq, mk, mv, batch, seq):
    t = mq.shape[0]
    steps = seq // MEM_TQ
    return pl.pallas_call(
        _memattn_body,
        grid=(batch, steps),
        in_specs=[
            pl.BlockSpec((MEM_TQ, MEM_WIDTH), lambda b, s: (b * steps + s, 0)),
            pl.BlockSpec((MEM_LEN, MEM_WIDTH), lambda b, s: (b, 0)),
            pl.BlockSpec((MEM_LEN, MEM_WIDTH), lambda b, s: (b, 0)),
        ],
        out_specs=pl.BlockSpec((MEM_TQ, MEM_WIDTH), lambda b, s: (b * steps + s, 0)),
        out_shape=jax.ShapeDtypeStruct((t, MEM_WIDTH), BF16),
        compiler_params=_params("parallel", "parallel"),
        name="mem_attn",
    )(mq, mk, mv)


def _merge_body(h_ref, gt_ref, oa_ref, ob_ref, oc_ref, wbr_ref, wo_ref, o_ref):
    y = gt_ref[0].astype(F32) * _dot(oa_ref[...], wbr_ref[0])
    y += gt_ref[1].astype(F32) * _dot(ob_ref[...], wbr_ref[1])
    y += gt_ref[2].astype(F32) * _dot(oc_ref[...], wbr_ref[2])
    o_ref[...] = h_ref[...] + _dot(y.astype(BF16), wo_ref[...])


def _merge(h, gates, o_a, o_b, o_c, w_branch, w_out, make_sides):
    t, d = h.shape
    row = lambda i: (i, 0)
    (out,), side_out = _hosted_call(
        _merge_body,
        grid=(t // MERGE_TM,),
        step_of=lambda i: i,
        in_specs=[
            pl.BlockSpec((MERGE_TM, d), row),
            pl.BlockSpec((N_BRANCH, MERGE_TM, d), lambda i: (0, i, 0)),
            pl.BlockSpec((MERGE_TM, BRANCH_WIDTH), row),
            pl.BlockSpec((MERGE_TM, BRANCH_WIDTH), row),
            pl.BlockSpec((MERGE_TM, BRANCH_WIDTH), row),
            pl.BlockSpec((N_BRANCH, BRANCH_WIDTH, d), lambda i: (0, 0, 0), pipeline_mode=SINGLE),
            pl.BlockSpec((d, d), lambda i: (0, 0), pipeline_mode=SINGLE),
        ],
        out_specs=[pl.BlockSpec((MERGE_TM, d), row)],
        out_shape=[jax.ShapeDtypeStruct((t, d), F32)],
        args=(h, gates, o_a, o_b, o_c, w_branch, w_out),
        sides=[m(lambda i: i) for m in make_sides],
        semantics=("arbitrary",),
        name="merge",
    )
    return out, side_out


def kernel(x, mem, g_ffn1, w_ffn1_in, w_ffn1_out, g_mix, w_in, gmlp_ln_g, gmlp_ln_b, w_s, b_s, swa_sinks, g_mem, w_mem_kv, w_branch, w_out, g_ffn2, w_ffn2_in, w_ffn2_out, g_final):
    batch, seq, d = x.shape
    depth = w_in.shape[0]
    xt = x.reshape(batch * seq, d)
    memt = mem.reshape(batch * MEM_LEN, d)
    w_branch_rows = w_branch.reshape(depth, N_BRANCH * BRANCH_WIDTH, d)

    def mixer_casts(l):
        return [functools.partial(_side_rows, w_in, l, MIXER_CAST_ROWS),
                functools.partial(_side_rows, w_mem_kv, l, MIXER_CAST_ROWS),
                functools.partial(_side_rows, w_out, l, MIXER_CAST_ROWS),
                functools.partial(_side_rows, w_branch_rows, l, 2 * MIXER_CAST_ROWS)]

    def ffn_casts(w_in_ffn, w_out_ffn, l, rows_in=CAST_ROWS // 2, rows_out=CAST_ROWS):
        return [functools.partial(_side_ffn_in, w_in_ffn, l, rows_in),
                functools.partial(_side_rows, w_out_ffn, l, rows_out)]

    wab1, wo1 = [_cast_alone(m) for m in ffn_casts(w_ffn1_in, w_ffn1_out, 0, ALONE_IN_ROWS, ALONE_OUT_ROWS)]

    for l in range(depth):
        more = l + 1 < depth
        h, (w_in_b, w_mkv, w_o, w_br) = _ffn(xt, g_ffn1[l], wab1, wo1, g_final, False, mixer_casts(l))
        n, o_a = _uv(h, g_mix[l], w_in_b, gmlp_ln_g[l], gmlp_ln_b[l], w_s[l], b_s[l])
        q, k_lo, k_hi, v_lo, v_hi, mq = _qkvm(n, w_in_b)
        gates, (wab2, wo2) = _gates(n, w_in_b, ffn_casts(w_ffn2_in, w_ffn2_out, l))
        mk, mv = _mkv(memt, g_mem[l], w_mkv)
        o_b = _swa(q, k_lo, k_hi, v_lo, v_hi, swa_sinks[l], batch, seq)
        o_c = _memattn(mq, mk, mv, batch, seq)
        h, merge_side = _merge(h, gates, o_a, o_b, o_c, w_br.reshape(N_BRANCH, BRANCH_WIDTH, d), w_o,
                               ffn_casts(w_ffn1_in, w_ffn1_out, l + 1) if more else [])
        xt, _ = _ffn(h, g_ffn2[l], wab2, wo2, g_final, not more)
        if more:
            wab1, wo1 = merge_side
    return xt.reshape(batch, seq, d)
```

```python
import functools
import math
from typing import Callable, NamedTuple

import jax
import jax.numpy as jnp
from jax import lax
from jax.experimental import pallas as pl
from jax.experimental.pallas import tpu as pltpu

D_MODEL = 2048
MEM_LEN = 256
CHUNK = 128
WINDOW = 128
G_GROUPS = 4
G_WIDTH = 1024
G_GROUP_DIM = G_WIDTH // G_GROUPS
SWA_HEADS = 16
SWA_KV_HEADS = 4
SWA_HEAD_DIM = 64
SWA_REP = SWA_HEADS // SWA_KV_HEADS
SWA_WIDTH = SWA_HEADS * SWA_HEAD_DIM
SWA_KV_WIDTH = SWA_KV_HEADS * SWA_HEAD_DIM
MEM_HEADS = 4
MEM_HEAD_DIM = 256
MEM_WIDTH = MEM_HEADS * MEM_HEAD_DIM
N_BRANCH = 3
BRANCH_WIDTH = 1024
D_FF = 5504
EPS = 1e-6
NEG = -1e30

F32 = jnp.float32
BF16 = jnp.bfloat16

LANES = 128
VMEM_LIMIT_BYTES = 60 * 1024 * 1024

FFN_TM = 1024
FFN_TF = 1024
FFN_TF_HOST = 512
FFN_TN = 512
UV_TM = 1024
PROJ_TM = 1024
GATES_TM = 1024
MERGE_TM = 256
SWA_TQ = 1024
MEM_TQ = 2048
W_COLS = 512
CAST_ROWS = 128
MIXER_CAST_ROWS = 16
ALONE_IN_ROWS = 256
ALONE_OUT_ROWS = D_FF // 8

SINGLE = pl.Buffered(1)


def _params(*semantics):
    return pltpu.CompilerParams(dimension_semantics=semantics, vmem_limit_bytes=VMEM_LIMIT_BYTES)


def _rmsnorm_f32(xf, g):
    return xf * lax.rsqrt(jnp.mean(xf * xf, axis=-1, keepdims=True) + EPS) * g


def _dot(a, b):
    return jnp.dot(a, b, preferred_element_type=F32)


def _dot_nt(a, b):
    return lax.dot_general(a, b, (((1,), (1,)), ((), ())), preferred_element_type=F32)


class _SideCast(NamedTuple):
    srcs: tuple
    in_specs: tuple
    out_spec: pl.BlockSpec
    out_shape: jax.ShapeDtypeStruct
    write: Callable
    n_blocks: int


def _side_rows(w, l, rows_blk, step_of):
    _, rows, cols = w.shape
    n_blocks = rows // rows_blk
    assert rows % rows_blk == 0

    def blk(*g):
        return jnp.minimum(step_of(*g), n_blocks - 1)

    def write(step, in_refs, out_ref):
        @pl.when(step < n_blocks)
        def _():
            out_ref[...] = in_refs[0][...].astype(BF16)

    return _SideCast(
        (w,),
        (pl.BlockSpec((None, rows_blk, cols), lambda *g: (l, blk(*g), 0)),),
        pl.BlockSpec((rows_blk, cols), lambda *g: (blk(*g), 0)),
        jax.ShapeDtypeStruct((rows, cols), BF16), write, n_blocks)


def _side_ffn_in(w, l, rows_blk, step_of):
    _, d, _ = w.shape
    assert d % rows_blk == 0 and D_FF % LANES == 0
    n_blocks = d // rows_blk

    def blk(*g):
        return jnp.minimum(step_of(*g), n_blocks - 1)

    def write(step, in_refs, out_ref):
        @pl.when(step < n_blocks)
        def _():
            for half, in_ref in enumerate(in_refs):
                for k in range(D_FF // LANES):
                    dst = (2 * k + half) * LANES
                    out_ref[:, dst:dst + LANES] = in_ref[:, k * LANES:(k + 1) * LANES].astype(BF16)

    half_spec = lambda half: pl.BlockSpec((None, rows_blk, D_FF), lambda *g: (l, blk(*g), half))
    return _SideCast(
        (w, w),
        (half_spec(0), half_spec(1)),
        pl.BlockSpec((rows_blk, 2 * D_FF), lambda *g: (blk(*g), 0)),
        jax.ShapeDtypeStruct((d, 2 * D_FF), BF16), write, n_blocks)


def _hosted_call(body, *, grid, step_of, in_specs, out_specs, out_shape, args, sides, semantics, name,
                 scratch_shapes=()):
    n_in, n_out, k = len(in_specs), len(out_specs), len(sides)
    side_in_counts = [len(s.in_specs) for s in sides]
    n_side_in = sum(side_in_counts)
    n_steps = math.prod(grid)
    assert all(s.n_blocks <= n_steps for s in sides)

    def hosted(*refs):
        ins, side_ins = refs[:n_in], refs[n_in:n_in + n_side_in]
        outs = refs[n_in + n_side_in:n_in + n_side_in + n_out]
        side_outs = refs[n_in + n_side_in + n_out:n_in + n_side_in + n_out + k]
        scratch = refs[n_in + n_side_in + n_out + k:]
        if k:
            step = step_of(*[pl.program_id(a) for a in range(len(grid))])
            first = 0
            for side, count, o_ref in zip(sides, side_in_counts, side_outs):
                side.write(step, side_ins[first:first + count], o_ref)
                first += count
        body(*ins, *outs, *scratch)

    res = pl.pallas_call(
        hosted,
        grid=grid,
        in_specs=list(in_specs) + [spec for s in sides for spec in s.in_specs],
        out_specs=list(out_specs) + [s.out_spec for s in sides],
        out_shape=list(out_shape) + [s.out_shape for s in sides],
        scratch_shapes=list(scratch_shapes),
        compiler_params=_params(*semantics),
        name=name,
    )(*args, *[src for s in sides for src in s.srcs])
    return res[:n_out], res[n_out:]


def _cast_alone(make_side):
    side = make_side(lambda i: i)
    _, (out,) = _hosted_call(lambda: None, grid=(side.n_blocks,), step_of=lambda i: i, in_specs=[],
                             out_specs=[], out_shape=[], args=[], sides=[side],
                             semantics=("arbitrary",), name="cast")
    return out


def _ffn_step(n, w_ref, wo_ref, o_ref, units, acc_ref=None):
    acc_ref = o_ref if acc_ref is None else acc_ref
    pairs = units // LANES
    even = pairs - pairs % 2
    z = [_dot(n, w_ref[:, :2 * even * LANES])] if even else []
    if pairs % 2:
        half_rows = n.shape[0] // 2
        w_last = w_ref[:, 2 * even * LANES:2 * pairs * LANES]
        z.append(jnp.concatenate([_dot(n[:half_rows], w_last), _dot(n[half_rows:], w_last)], axis=0))
    z = jnp.concatenate(z, axis=1)
    hid = []
    for k in range(pairs):
        a = z[:, (2 * k) * LANES:(2 * k + 1) * LANES]
        b = z[:, (2 * k + 1) * LANES:(2 * k + 2) * LANES]
        half_a = 0.5 * a
        hid.append((0.5 * ((half_a * jnp.tanh(half_a) + half_a) * b)).astype(BF16))
    hid = jnp.concatenate(hid, axis=1)
    for c in range(o_ref.shape[1] // FFN_TN):
        cs = slice(c * FFN_TN, (c + 1) * FFN_TN)
        o_ref[:, cs] = acc_ref[:, cs] + _dot(hid, wo_ref[:units, cs])


def _ffn_body(x_hbm, g_ref, w_ref, wo_ref, gf_ref, o_ref, n_ref, x_ref, x_sem, *, apply_final_norm, tf):
    i, j = pl.program_id(0), pl.program_id(1)
    steps = pl.cdiv(D_FF, tf)
    last_tf = D_FF - (steps - 1) * tf
    assert steps > 2

    def x_copy(tile):
        return pltpu.make_async_copy(x_hbm.at[pl.ds(tile * FFN_TM, FFN_TM), :], x_ref, x_sem)

    @pl.when((i == 0) & (j == 0))
    def _():
        x_copy(0).start()

    @pl.when(j == 0)
    def _():
        x_copy(i).wait()
        n = _rmsnorm_f32(x_ref[...], g_ref[...]).astype(BF16)
        n_ref[...] = n
        _ffn_step(n, w_ref, wo_ref, o_ref, tf, acc_ref=x_ref)

    @pl.when((j == 1) & (i + 1 < pl.num_programs(0)))
    def _():
        x_copy(i + 1).start()

    @pl.when((j > 0) & (j < steps - 1))
    def _():
        _ffn_step(n_ref[...], w_ref, wo_ref, o_ref, tf)

    @pl.when(j == steps - 1)
    def _():
        _ffn_step(n_ref[...], w_ref, wo_ref, o_ref, last_tf)
        if apply_final_norm:
            o_ref[...] = _rmsnorm_f32(o_ref[...], gf_ref[...])


def _ffn(x, g, w_in_b, w_out_b, g_final, apply_final_norm, make_sides=()):
    t, d = x.shape
    tf = FFN_TF_HOST if make_sides else FFN_TF
    steps = pl.cdiv(D_FF, tf)
    step_of = lambda i, j: i * steps + j
    (out,), side_out = _hosted_call(
        functools.partial(_ffn_body, apply_final_norm=apply_final_norm, tf=tf),
        grid=(t // FFN_TM, steps),
        step_of=step_of,
        in_specs=[
            pl.BlockSpec(memory_space=pl.ANY),
            pl.BlockSpec((1, d), lambda i, j: (0, 0)),
            pl.BlockSpec((d, 2 * tf), lambda i, j: (0, j)),
            pl.BlockSpec((tf, d), lambda i, j: (j, 0)),
            pl.BlockSpec((1, d), lambda i, j: (0, 0)),
        ],
        out_specs=[pl.BlockSpec((FFN_TM, d), lambda i, j: (i, 0))],
        out_shape=[jax.ShapeDtypeStruct((t, d), F32)],
        args=(x, g.reshape(1, d), w_in_b, w_out_b, g_final.reshape(1, d)),
        sides=[m(step_of) for m in make_sides],
        semantics=("arbitrary", "arbitrary"),
        scratch_shapes=[pltpu.VMEM((FFN_TM, d), BF16), pltpu.VMEM((FFN_TM, d), F32),
                        pltpu.SemaphoreType.DMA(())],
        name="ffn",
    )
    return out, side_out


UV_END = 2 * G_WIDTH
QKVM_WIDTH = SWA_WIDTH + 2 * SWA_KV_WIDTH + MEM_WIDTH
QKVM_END = UV_END + QKVM_WIDTH


def _gelu(x):
    return 0.5 * x * (1.0 + lax.erf(x * math.sqrt(0.5)))


def _uv_body(h_ref, g_ref, wu_ref, wv_ref, lng_ref, lnb_ref, ws_ref, bst_ref, n_ref, oa_ref):
    row = lax.broadcasted_iota(jnp.int32, (CHUNK, CHUNK), 0)
    col = lax.broadcasted_iota(jnp.int32, (CHUNK, CHUNK), 1)
    causal = row >= col

    n = _rmsnorm_f32(h_ref[...], g_ref[...]).astype(BF16)
    n_ref[...] = n
    v = _gelu(_dot(n, wv_ref[...]))
    u = _gelu(_dot(n, wu_ref[...]))
    mu = jnp.mean(v, axis=-1, keepdims=True)
    vc = v - mu
    var = jnp.mean(vc * vc, axis=-1, keepdims=True)
    vn = (vc * lax.rsqrt(var + EPS) * lng_ref[...] + lnb_ref[...]).astype(BF16)
    for g in range(G_GROUPS):
        ws = jnp.where(causal, ws_ref[g], 0.0).astype(BF16)
        bias = bst_ref[:, g:g + 1]
        cs = slice(g * G_GROUP_DIM, (g + 1) * G_GROUP_DIM)
        for c in range(UV_TM // CHUNK):
            rs = slice(c * CHUNK, (c + 1) * CHUNK)
            mixed = _dot(ws, vn[rs, cs]) + bias
            oa_ref[rs, cs] = (u[rs, cs] * mixed).astype(BF16)


def _uv(h, g, w_in_b, ln_g, ln_b, w_s, b_s):
    t, d = h.shape
    const = lambda i: (0, 0)
    (n, o_a), _ = _hosted_call(
        _uv_body,
        grid=(t // UV_TM,),
        step_of=lambda i: i,
        in_specs=[
            pl.BlockSpec((UV_TM, d), lambda i: (i, 0)),
            pl.BlockSpec((1, d), const),
            pl.BlockSpec((d, G_WIDTH), lambda i: (0, 0), pipeline_mode=SINGLE),
            pl.BlockSpec((d, G_WIDTH), lambda i: (0, 1), pipeline_mode=SINGLE),
            pl.BlockSpec((1, G_WIDTH), const),
            pl.BlockSpec((1, G_WIDTH), const),
            pl.BlockSpec((G_GROUPS, CHUNK, CHUNK), lambda i: (0, 0, 0)),
            pl.BlockSpec((CHUNK, G_GROUPS), const),
        ],
        out_specs=[
            pl.BlockSpec((UV_TM, d), lambda i: (i, 0)),
            pl.BlockSpec((UV_TM, G_WIDTH), lambda i: (i, 0)),
        ],
        out_shape=[
            jax.ShapeDtypeStruct((t, d), BF16),
            jax.ShapeDtypeStruct((t, G_WIDTH), BF16),
        ],
        args=(h, g.reshape(1, d), w_in_b, w_in_b, ln_g.reshape(1, G_WIDTH), ln_b.reshape(1, G_WIDTH), w_s,
              jnp.transpose(b_s)),
        sides=[],
        semantics=("parallel",),
        name="uv_gmlp",
    )
    return n, o_a


SWA_SCALE = 1.0 / math.sqrt(SWA_HEAD_DIM)
HALF_LANES = LANES // 2
KV_PLACED_WIDTH = SWA_KV_HEADS * LANES
QKVM_BLOCKS = QKVM_WIDTH // W_COLS


def _place_heads(z):
    rows = z.shape[0]
    low_half = lax.broadcasted_iota(jnp.int32, (rows, LANES), 1) < HALF_LANES
    zero = jnp.zeros((rows, LANES), F32)
    low, high = [], []
    for pair in range(SWA_KV_HEADS // 2):
        zg = z[:, pair * LANES:(pair + 1) * LANES]
        swapped = pltpu.roll(zg, HALF_LANES, axis=1)
        low += [jnp.where(low_half, zg, zero), jnp.where(low_half, swapped, zero)]
        high += [jnp.where(low_half, zero, swapped), jnp.where(low_half, zero, zg)]
    return (jnp.concatenate(low, axis=1).astype(BF16), jnp.concatenate(high, axis=1).astype(BF16))


def _qkvm_body(n_ref, wq0_ref, wq1_ref, wkv_ref, wm0_ref, wm1_ref,
               q_ref, klo_ref, khi_ref, vlo_ref, vhi_ref, mq_ref):
    n = n_ref[...]
    q_ref[:, :W_COLS] = (_dot(n, wq0_ref[...]) * SWA_SCALE).astype(BF16)
    q_ref[:, W_COLS:] = (_dot(n, wq1_ref[...]) * SWA_SCALE).astype(BF16)
    kv = _dot(n, wkv_ref[...])
    klo_ref[...], khi_ref[...] = _place_heads(kv[:, :SWA_KV_WIDTH])
    vlo_ref[...], vhi_ref[...] = _place_heads(kv[:, SWA_KV_WIDTH:])
    mq_ref[:, :W_COLS] = _dot(n, wm0_ref[...]).astype(BF16)
    mq_ref[:, W_COLS:] = _dot(n, wm1_ref[...]).astype(BF16)


def _qkvm(n, w_in_b):
    t, d = n.shape
    assert SWA_WIDTH == 2 * W_COLS and 2 * SWA_KV_WIDTH == W_COLS and MEM_WIDTH == 2 * W_COLS
    first = UV_END // W_COLS
    widths = (SWA_WIDTH,) + (KV_PLACED_WIDTH,) * 4 + (MEM_WIDTH,)
    w_specs = [pl.BlockSpec((d, W_COLS), functools.partial(lambda c, i: (0, c), first + c),
                            pipeline_mode=SINGLE) for c in range(QKVM_BLOCKS)]
    outs, _ = _hosted_call(
        _qkvm_body,
        grid=(t // PROJ_TM,),
        step_of=lambda i: i,
        in_specs=[pl.BlockSpec((PROJ_TM, d), lambda i: (i, 0))] + w_specs,
        out_specs=[pl.BlockSpec((PROJ_TM, w_), lambda i: (i, 0)) for w_ in widths],
        out_shape=[jax.ShapeDtypeStruct((t, w_), BF16) for w_ in widths],
        args=(n,) + (w_in_b,) * QKVM_BLOCKS,
        sides=[],
        semantics=("parallel",),
        name="qkvm",
    )
    return outs


GATE_BLOCKS = D_MODEL // W_COLS


def _gates_body(n_ref, *refs):
    w_refs, o_ref = refs[:GATE_BLOCKS], refs[GATE_BLOCKS]
    n = n_ref[...]
    for c, w_ref in enumerate(w_refs):
        z = _dot(n, w_ref[...])
        o_ref[:, c * W_COLS:(c + 1) * W_COLS] = (0.5 * jnp.tanh(0.5 * z) + 0.5).astype(BF16)


def _gates(n, w_in_b, make_sides):
    t, d = n.shape
    tiles = t // GATES_TM
    first = QKVM_END // W_COLS
    step_of = lambda b, i: b * tiles + i
    w_specs = [pl.BlockSpec((d, W_COLS), functools.partial(lambda c, b, i: (0, first + b * GATE_BLOCKS + c), c))
               for c in range(GATE_BLOCKS)]
    (gates,), side_out = _hosted_call(
        _gates_body,
        grid=(N_BRANCH, tiles),
        step_of=step_of,
        in_specs=[pl.BlockSpec((GATES_TM, d), lambda b, i: (i, 0))] + w_specs,
        out_specs=[pl.BlockSpec((None, GATES_TM, d), lambda b, i: (b, i, 0))],
        out_shape=[jax.ShapeDtypeStruct((N_BRANCH, t, d), BF16)],
        args=(n,) + (w_in_b,) * GATE_BLOCKS,
        sides=[m(step_of) for m in make_sides],
        semantics=("arbitrary", "arbitrary"),
        name="gates",
    )
    return gates, side_out


def _mkv_body(m_ref, g_ref, w_ref, mk_ref, mv_ref):
    n = _rmsnorm_f32(m_ref[...], g_ref[...]).astype(BF16)
    z = _dot(n, w_ref[...]).astype(BF16)
    mk_ref[...] = z[:, :MEM_WIDTH]
    mv_ref[...] = z[:, MEM_WIDTH:]


def _mkv(mem, g, w):
    t, d = mem.shape
    return pl.pallas_call(
        _mkv_body,
        grid=(t // MEM_LEN,),
        in_specs=[
            pl.BlockSpec((MEM_LEN, d), lambda i: (i, 0)),
            pl.BlockSpec((1, d), lambda i: (0, 0)),
            pl.BlockSpec((d, 2 * MEM_WIDTH), lambda i: (0, 0), pipeline_mode=SINGLE),
        ],
        out_specs=[pl.BlockSpec((MEM_LEN, MEM_WIDTH), lambda i: (i, 0))] * 2,
        out_shape=[jax.ShapeDtypeStruct((t, MEM_WIDTH), BF16)] * 2,
        compiler_params=_params("parallel"),
        name="mem_kv",
    )(mem, g.reshape(1, d), w)


SWA_SLOPES = tuple(2.0 ** (-8.0 * (h + 1) / SWA_HEADS) for h in range(SWA_HEADS))


def _swa_body(sink_ref, q_ref, klo_ref, khi_ref, vlo_ref, vhi_ref,
              klo_p_ref, khi_p_ref, vlo_p_ref, vhi_p_ref, o_ref):
    w = WINDOW
    t = pl.program_id(1)
    i2 = lax.broadcasted_iota(jnp.int32, (2 * w, w), 0) & (w - 1)
    j2 = lax.broadcasted_iota(jnp.int32, (2 * w, w), 1)
    from_prev = j2 > i2
    dist = ((i2 - j2) & (w - 1)).astype(F32)
    no_prev = j2 > jnp.maximum(i2, jnp.where(t == 0, -1, w))
    low_half = j2 < HALF_LANES
    zero = jnp.zeros((2 * w, w), F32)
    ones_cols = (jnp.where(low_half, 1.0, 0.0).astype(BF16), jnp.where(low_half, 0.0, 1.0).astype(BF16))

    def rows2(top, bottom):
        return jnp.concatenate([jnp.full((w, w), top, F32), jnp.full((w, w), bottom, F32)], axis=0)

    for blk in range(SWA_TQ // w):
        rs = slice(blk * w, (blk + 1) * w)

        def band(cur_ref, prev_ref, cols):
            prev = prev_ref[:, cols] if blk == 0 else cur_ref[(blk - 1) * w:blk * w, cols]
            return jnp.concatenate([prev, cur_ref[rs, cols]], axis=0)

        for kvh in range(SWA_KV_HEADS):
            cols = slice(kvh * LANES, (kvh + 1) * LANES)
            q2 = jnp.concatenate([q_ref[rs, (2 * kvh) * LANES:(2 * kvh + 1) * LANES],
                                  q_ref[rs, (2 * kvh + 1) * LANES:(2 * kvh + 2) * LANES]], axis=0)
            keys = jnp.concatenate([band(klo_ref, klo_p_ref, cols), band(khi_ref, khi_p_ref, cols)], axis=0)
            s = _dot_nt(q2, keys)
            acc = None
            sink_terms = []
            for par, (v_ref, v_p_ref) in enumerate(((vlo_ref, vlo_p_ref), (vhi_ref, vhi_p_ref))):
                heads = (kvh * SWA_REP + par, kvh * SWA_REP + 2 + par)
                base = par * 2 * w
                sc = jnp.where(from_prev, s[:, base:base + w], s[:, base + w:base + 2 * w])
                sc = sc - rows2(SWA_SLOPES[heads[0]], SWA_SLOPES[heads[1]]) * dist
                if blk == 0:
                    sc = jnp.where(no_prev, NEG, sc)
                sink = rows2(sink_ref[heads[0]], sink_ref[heads[1]])
                m = jnp.maximum(jnp.broadcast_to(jnp.max(sc, axis=-1, keepdims=True), (2 * w, w)), sink)
                p = jnp.exp(sc - m)
                pcat = jnp.concatenate([jnp.where(from_prev, p, zero).astype(BF16),
                                        jnp.where(from_prev, zero, p).astype(BF16)], axis=1)
                vext = jnp.concatenate([band(v_ref, v_p_ref, cols), ones_cols[par]], axis=1)
                part = _dot(pcat, vext)
                acc = part if acc is None else acc + part
                sink_terms.append(jnp.exp(sink - m))
            denom = acc[:, w:] + jnp.where(low_half, sink_terms[0], sink_terms[1])
            out = (acc[:, :w] / denom).astype(BF16)
            o_ref[rs, (2 * kvh) * LANES:(2 * kvh + 1) * LANES] = out[:w]
            o_ref[rs, (2 * kvh + 1) * LANES:(2 * kvh + 2) * LANES] = out[w:]


def _swa(q, k_lo, k_hi, v_lo, v_hi, sinks, batch, seq):
    t = q.shape[0]
    steps = seq // SWA_TQ
    blocks_per_step = SWA_TQ // WINDOW
    blocks_per_seq = seq // WINDOW

    def cur(b, s):
        return (b * steps + s, 0)

    def prev(b, s):
        return (b * blocks_per_seq + jnp.maximum(s * blocks_per_step - 1, 0), 0)

    kv_cur = pl.BlockSpec((SWA_TQ, KV_PLACED_WIDTH), cur)
    kv_prev = pl.BlockSpec((WINDOW, KV_PLACED_WIDTH), prev)
    return pl.pallas_call(
        _swa_body,
        grid=(batch, steps),
        in_specs=[pl.BlockSpec(memory_space=pltpu.SMEM), pl.BlockSpec((SWA_TQ, SWA_WIDTH), cur)]
        + [kv_cur] * 4 + [kv_prev] * 4,
        out_specs=pl.BlockSpec((SWA_TQ, SWA_WIDTH), cur),
        out_shape=jax.ShapeDtypeStruct((t, SWA_WIDTH), BF16),
        compiler_params=_params("parallel", "parallel"),
        name="swa",
    )(sinks, q, k_lo, k_hi, v_lo, v_hi, k_lo, k_hi, v_lo, v_hi)


MEM_SCALE = 1.0 / math.sqrt(MEM_HEAD_DIM)


def _memattn_body(q_ref, mk_ref, mv_ref, o_ref):
    for h in range(MEM_HEADS):
        hs = slice(h * MEM_HEAD_DIM, (h + 1) * MEM_HEAD_DIM)
        s = _dot_nt(q_ref[:, hs], mk_ref[:, hs]) * MEM_SCALE
        m = jnp.max(s, axis=-1, keepdims=True)
        p = jnp.exp(s - m)
        probs = (p / jnp.sum(p, axis=-1, keepdims=True)).astype(BF16)
        o_ref[:, hs] = _dot(probs, mv_ref[:, hs]).astype(BF16)


def _memattn(mq, mk, mv, batch, seq):
    t = mq.shape[0]
    steps = seq // MEM_TQ
    return pl.pallas_call(
        _memattn_body,
        grid=(batch, steps),
        in_specs=[
            pl.BlockSpec((MEM_TQ, MEM_WIDTH), lambda b, s: (b * steps + s, 0)),
            pl.BlockSpec((MEM_LEN, MEM_WIDTH), lambda b, s: (b, 0)),
            pl.BlockSpec((MEM_LEN, MEM_WIDTH), lambda b, s: (b, 0)),
        ],
        out_specs=pl.BlockSpec((MEM_TQ, MEM_WIDTH), lambda b, s: (b * steps + s, 0)),
        out_shape=jax.ShapeDtypeStruct((t, MEM_WIDTH), BF16),
        compiler_params=_params("parallel", "parallel"),
        name="mem_attn",
    )(mq, mk, mv)


def _merge_body(h_ref, gt_ref, oa_ref, ob_ref, oc_ref, wbr_ref, wo_ref, o_ref):
    y = gt_ref[0].astype(F32) * _dot(oa_ref[...], wbr_ref[0])
    y += gt_ref[1].astype(F32) * _dot(ob_ref[...], wbr_ref[1])
    y += gt_ref[2].astype(F32) * _dot(oc_ref[...], wbr_ref[2])
    o_ref[...] = h_ref[...] + _dot(y.astype(BF16), wo_ref[...])


def _merge(h, gates, o_a, o_b, o_c, w_branch, w_out, make_sides):
    t, d = h.shape
    row = lambda i: (i, 0)
    (out,), side_out = _hosted_call(
        _merge_body,
        grid=(t // MERGE_TM,),
        step_of=lambda i: i,
        in_specs=[
            pl.BlockSpec((MERGE_TM, d), row),
            pl.BlockSpec((N_BRANCH, MERGE_TM, d), lambda i: (0, i, 0)),
            pl.BlockSpec((MERGE_TM, BRANCH_WIDTH), row),
            pl.BlockSpec((MERGE_TM, BRANCH_WIDTH), row),
            pl.BlockSpec((MERGE_TM, BRANCH_WIDTH), row),
            pl.BlockSpec((N_BRANCH, BRANCH_WIDTH, d), lambda i: (0, 0, 0), pipeline_mode=SINGLE),
            pl.BlockSpec((d, d), lambda i: (0, 0), pipeline_mode=SINGLE),
        ],
        out_specs=[pl.BlockSpec((MERGE_TM, d), row)],
        out_shape=[jax.ShapeDtypeStruct((t, d), F32)],
        args=(h, gates, o_a, o_b, o_c, w_branch, w_out),
        sides=[m(lambda i: i) for m in make_sides],
        semantics=("arbitrary",),
        name="merge",
    )
    return out, side_out


def kernel(x, mem, g_ffn1, w_ffn1_in, w_ffn1_out, g_mix, w_in, gmlp_ln_g, gmlp_ln_b, w_s, b_s, swa_sinks, g_mem, w_mem_kv, w_branch, w_out, g_ffn2, w_ffn2_in, w_ffn2_out, g_final):
    batch, seq, d = x.shape
    depth = w_in.shape[0]
    xt = x.reshape(batch * seq, d)
    memt = mem.reshape(batch * MEM_LEN, d)
    w_branch_rows = w_branch.reshape(depth, N_BRANCH * BRANCH_WIDTH, d)

    def mixer_casts(l):
        return [functools.partial(_side_rows, w_in, l, MIXER_CAST_ROWS),
                functools.partial(_side_rows, w_mem_kv, l, MIXER_CAST_ROWS),
                functools.partial(_side_rows, w_out, l, MIXER_CAST_ROWS),
                functools.partial(_side_rows, w_branch_rows, l, 2 * MIXER_CAST_ROWS)]

    def ffn_casts(w_in_ffn, w_out_ffn, l, rows_in=CAST_ROWS // 2, rows_out=CAST_ROWS):
        return [functools.partial(_side_ffn_in, w_in_ffn, l, rows_in),
                functools.partial(_side_rows, w_out_ffn, l, rows_out)]

    wab1, wo1 = [_cast_alone(m) for m in ffn_casts(w_ffn1_in, w_ffn1_out, 0, ALONE_IN_ROWS, ALONE_OUT_ROWS)]

    for l in range(depth):
        more = l + 1 < depth
        h, (w_in_b, w_mkv, w_o, w_br) = _ffn(xt, g_ffn1[l], wab1, wo1, g_final, False, mixer_casts(l))
        n, o_a = _uv(h, g_mix[l], w_in_b, gmlp_ln_g[l], gmlp_ln_b[l], w_s[l], b_s[l])
        q, k_lo, k_hi, v_lo, v_hi, mq = _qkvm(n, w_in_b)
        gates, (wab2, wo2) = _gates(n, w_in_b, ffn_casts(w_ffn2_in, w_ffn2_out, l))
        mk, mv = _mkv(memt, g_mem[l], w_mkv)
        o_b = _swa(q, k_lo, k_hi, v_lo, v_hi, swa_sinks[l], batch, seq)
        o_c = _memattn(mq, mk, mv, batch, seq)
        h, merge_side = _merge(h, gates, o_a, o_b, o_c, w_br.reshape(N_BRANCH, BRANCH_WIDTH, d), w_o,
                               ffn_casts(w_ffn1_in, w_ffn1_out, l + 1) if more else [])
        xt, _ = _ffn(h, g_ffn2[l], wab2, wo2, g_final, not more)
        if more:
            wab1, wo1 = merge_side
    return xt.reshape(batch, seq, d)
```

```python
import functools
import math
from typing import Callable, NamedTuple

import jax
import jax.numpy as jnp
from jax import lax
from jax.experimental import pallas as pl
from jax.experimental.pallas import tpu as pltpu

D_MODEL = 2048
MEM_LEN = 256
CHUNK = 128
WINDOW = 128
G_GROUPS = 4
G_WIDTH = 1024
G_GROUP_DIM = G_WIDTH // G_GROUPS
SWA_HEADS = 16
SWA_KV_HEADS = 4
SWA_HEAD_DIM = 64
SWA_REP = SWA_HEADS // SWA_KV_HEADS
SWA_WIDTH = SWA_HEADS * SWA_HEAD_DIM
SWA_KV_WIDTH = SWA_KV_HEADS * SWA_HEAD_DIM
MEM_HEADS = 4
MEM_HEAD_DIM = 256
MEM_WIDTH = MEM_HEADS * MEM_HEAD_DIM
N_BRANCH = 3
BRANCH_WIDTH = 1024
D_FF = 5504
EPS = 1e-6
NEG = -1e30

F32 = jnp.float32
BF16 = jnp.bfloat16

LANES = 128
VMEM_LIMIT_BYTES = 60 * 1024 * 1024

FFN_TM = 1024
FFN_TF = 1024
FFN_TF_HOST = 512
FFN_TN = 512
UV_TM = 1024
PROJ_TM = 1024
PROJ_TM_HOST = 512
GATES_TM = 1024
MERGE_TM = 256
SWA_TQ = 1024
MEM_TQ = 2048
W_COLS = 512
CAST_ROWS = 128
MIXER_CAST_ROWS = 16
ALONE_IN_ROWS = 256
ALONE_OUT_ROWS = D_FF // 8

SINGLE = pl.Buffered(1)


def _params(*semantics):
    return pltpu.CompilerParams(dimension_semantics=semantics, vmem_limit_bytes=VMEM_LIMIT_BYTES)


def _rmsnorm_f32(xf, g):
    return xf * lax.rsqrt(jnp.mean(xf * xf, axis=-1, keepdims=True) + EPS) * g


def _dot(a, b):
    return jnp.dot(a, b, preferred_element_type=F32)


def _dot_nt(a, b):
    return lax.dot_general(a, b, (((1,), (1,)), ((), ())), preferred_element_type=F32)


class _SideCast(NamedTuple):
    srcs: tuple
    in_specs: tuple
    out_spec: pl.BlockSpec
    out_shape: jax.ShapeDtypeStruct
    write: Callable
    n_blocks: int


def _side_rows(w, l, rows_blk, step_of):
    _, rows, cols = w.shape
    n_blocks = rows // rows_blk
    assert rows % rows_blk == 0

    def blk(*g):
        return jnp.minimum(step_of(*g), n_blocks - 1)

    def write(step, in_refs, out_ref):
        @pl.when(step < n_blocks)
        def _():
            out_ref[...] = in_refs[0][...].astype(BF16)

    return _SideCast(
        (w,),
        (pl.BlockSpec((None, rows_blk, cols), lambda *g: (l, blk(*g), 0)),),
        pl.BlockSpec((rows_blk, cols), lambda *g: (blk(*g), 0)),
        jax.ShapeDtypeStruct((rows, cols), BF16), write, n_blocks)


def _side_ffn_in(w, l, rows_blk, step_of):
    _, d, _ = w.shape
    assert d % rows_blk == 0 and D_FF % LANES == 0
    n_blocks = d // rows_blk

    def blk(*g):
        return jnp.minimum(step_of(*g), n_blocks - 1)

    def write(step, in_refs, out_ref):
        @pl.when(step < n_blocks)
        def _():
            for half, in_ref in enumerate(in_refs):
                for k in range(D_FF // LANES):
                    dst = (2 * k + half) * LANES
                    out_ref[:, dst:dst + LANES] = in_ref[:, k * LANES:(k + 1) * LANES].astype(BF16)

    half_spec = lambda half: pl.BlockSpec((None, rows_blk, D_FF), lambda *g: (l, blk(*g), half))
    return _SideCast(
        (w, w),
        (half_spec(0), half_spec(1)),
        pl.BlockSpec((rows_blk, 2 * D_FF), lambda *g: (blk(*g), 0)),
        jax.ShapeDtypeStruct((d, 2 * D_FF), BF16), write, n_blocks)


def _hosted_call(body, *, grid, step_of, in_specs, out_specs, out_shape, args, sides, semantics, name,
                 scratch_shapes=()):
    n_in, n_out, k = len(in_specs), len(out_specs), len(sides)
    side_in_counts = [len(s.in_specs) for s in sides]
    n_side_in = sum(side_in_counts)
    n_steps = math.prod(grid)
    assert all(s.n_blocks <= n_steps for s in sides)

    def hosted(*refs):
        ins, side_ins = refs[:n_in], refs[n_in:n_in + n_side_in]
        outs = refs[n_in + n_side_in:n_in + n_side_in + n_out]
        side_outs = refs[n_in + n_side_in + n_out:n_in + n_side_in + n_out + k]
        scratch = refs[n_in + n_side_in + n_out + k:]
        if k:
            step = step_of(*[pl.program_id(a) for a in range(len(grid))])
            first = 0
            for side, count, o_ref in zip(sides, side_in_counts, side_outs):
                side.write(step, side_ins[first:first + count], o_ref)
                first += count
        body(*ins, *outs, *scratch)

    res = pl.pallas_call(
        hosted,
        grid=grid,
        in_specs=list(in_specs) + [spec for s in sides for spec in s.in_specs],
        out_specs=list(out_specs) + [s.out_spec for s in sides],
        out_shape=list(out_shape) + [s.out_shape for s in sides],
        scratch_shapes=list(scratch_shapes),
        compiler_params=_params(*semantics),
        name=name,
    )(*args, *[src for s in sides for src in s.srcs])
    return res[:n_out], res[n_out:]


def _cast_alone(make_side):
    side = make_side(lambda i: i)
    _, (out,) = _hosted_call(lambda: None, grid=(side.n_blocks,), step_of=lambda i: i, in_specs=[],
                             out_specs=[], out_shape=[], args=[], sides=[side],
                             semantics=("arbitrary",), name="cast")
    return out


def _ffn_step(n, w_ref, wo_ref, o_ref, units, acc_ref=None):
    acc_ref = o_ref if acc_ref is None else acc_ref
    pairs = units // LANES
    even = pairs - pairs % 2
    z = [_dot(n, w_ref[:, :2 * even * LANES])] if even else []
    if pairs % 2:
        half_rows = n.shape[0] // 2
        w_last = w_ref[:, 2 * even * LANES:2 * pairs * LANES]
        z.append(jnp.concatenate([_dot(n[:half_rows], w_last), _dot(n[half_rows:], w_last)], axis=0))
    z = jnp.concatenate(z, axis=1)
    hid = []
    for k in range(pairs):
        a = z[:, (2 * k) * LANES:(2 * k + 1) * LANES]
        b = z[:, (2 * k + 1) * LANES:(2 * k + 2) * LANES]
        half_a = 0.5 * a
        hid.append((0.5 * ((half_a * jnp.tanh(half_a) + half_a) * b)).astype(BF16))
    hid = jnp.concatenate(hid, axis=1)
    for c in range(o_ref.shape[1] // FFN_TN):
        cs = slice(c * FFN_TN, (c + 1) * FFN_TN)
        o_ref[:, cs] = acc_ref[:, cs] + _dot(hid, wo_ref[:units, cs])


def _ffn_body(x_hbm, g_ref, w_ref, wo_ref, gf_ref, o_ref, n_ref, x_ref, x_sem, *, apply_final_norm, tf):
    i, j = pl.program_id(0), pl.program_id(1)
    steps = pl.cdiv(D_FF, tf)
    last_tf = D_FF - (steps - 1) * tf
    assert steps > 2

    def x_copy(tile):
        return pltpu.make_async_copy(x_hbm.at[pl.ds(tile * FFN_TM, FFN_TM), :], x_ref, x_sem)

    @pl.when((i == 0) & (j == 0))
    def _():
        x_copy(0).start()

    @pl.when(j == 0)
    def _():
        x_copy(i).wait()
        n = _rmsnorm_f32(x_ref[...], g_ref[...]).astype(BF16)
        n_ref[...] = n
        _ffn_step(n, w_ref, wo_ref, o_ref, tf, acc_ref=x_ref)

    @pl.when((j == 1) & (i + 1 < pl.num_programs(0)))
    def _():
        x_copy(i + 1).start()

    @pl.when((j > 0) & (j < steps - 1))
    def _():
        _ffn_step(n_ref[...], w_ref, wo_ref, o_ref, tf)

    @pl.when(j == steps - 1)
    def _():
        _ffn_step(n_ref[...], w_ref, wo_ref, o_ref, last_tf)
        if apply_final_norm:
            o_ref[...] = _rmsnorm_f32(o_ref[...], gf_ref[...])


def _ffn(x, g, w_in_b, w_out_b, g_final, apply_final_norm, make_sides=()):
    t, d = x.shape
    tf = FFN_TF_HOST if make_sides else FFN_TF
    steps = pl.cdiv(D_FF, tf)
    step_of = lambda i, j: i * steps + j
    (out,), side_out = _hosted_call(
        functools.partial(_ffn_body, apply_final_norm=apply_final_norm, tf=tf),
        grid=(t // FFN_TM, steps),
        step_of=step_of,
        in_specs=[
            pl.BlockSpec(memory_space=pl.ANY),
            pl.BlockSpec((1, d), lambda i, j: (0, 0)),
            pl.BlockSpec((d, 2 * tf), lambda i, j: (0, j)),
            pl.BlockSpec((tf, d), lambda i, j: (j, 0)),
            pl.BlockSpec((1, d), lambda i, j: (0, 0)),
        ],
        out_specs=[pl.BlockSpec((FFN_TM, d), lambda i, j: (i, 0))],
        out_shape=[jax.ShapeDtypeStruct((t, d), F32)],
        args=(x, g.reshape(1, d), w_in_b, w_out_b, g_final.reshape(1, d)),
        sides=[m(step_of) for m in make_sides],
        semantics=("arbitrary", "arbitrary"),
        scratch_shapes=[pltpu.VMEM((FFN_TM, d), BF16), pltpu.VMEM((FFN_TM, d), F32),
                        pltpu.SemaphoreType.DMA(())],
        name="ffn",
    )
    return out, side_out


UV_END = 2 * G_WIDTH
QKVM_WIDTH = SWA_WIDTH + 2 * SWA_KV_WIDTH + MEM_WIDTH
QKVM_END = UV_END + QKVM_WIDTH


def _gelu(x):
    return 0.5 * x * (1.0 + lax.erf(x * math.sqrt(0.5)))


def _uv_body(h_ref, g_ref, wu_ref, wv_ref, lng_ref, lnb_ref, ws_ref, bst_ref, n_ref, oa_ref):
    row = lax.broadcasted_iota(jnp.int32, (CHUNK, CHUNK), 0)
    col = lax.broadcasted_iota(jnp.int32, (CHUNK, CHUNK), 1)
    causal = row >= col

    n = _rmsnorm_f32(h_ref[...], g_ref[...]).astype(BF16)
    n_ref[...] = n
    v = _gelu(_dot(n, wv_ref[...]))
    u = _gelu(_dot(n, wu_ref[...]))
    mu = jnp.mean(v, axis=-1, keepdims=True)
    vc = v - mu
    var = jnp.mean(vc * vc, axis=-1, keepdims=True)
    vn = (vc * lax.rsqrt(var + EPS) * lng_ref[...] + lnb_ref[...]).astype(BF16)
    for g in range(G_GROUPS):
        ws = jnp.where(causal, ws_ref[g], 0.0).astype(BF16)
        bias = bst_ref[:, g:g + 1]
        cs = slice(g * G_GROUP_DIM, (g + 1) * G_GROUP_DIM)
        for c in range(UV_TM // CHUNK):
            rs = slice(c * CHUNK, (c + 1) * CHUNK)
            mixed = _dot(ws, vn[rs, cs]) + bias
            oa_ref[rs, cs] = (u[rs, cs] * mixed).astype(BF16)


def _uv(h, g, w_in_b, ln_g, ln_b, w_s, b_s):
    t, d = h.shape
    const = lambda i: (0, 0)
    (n, o_a), _ = _hosted_call(
        _uv_body,
        grid=(t // UV_TM,),
        step_of=lambda i: i,
        in_specs=[
            pl.BlockSpec((UV_TM, d), lambda i: (i, 0)),
            pl.BlockSpec((1, d), const),
            pl.BlockSpec((d, G_WIDTH), lambda i: (0, 0), pipeline_mode=SINGLE),
            pl.BlockSpec((d, G_WIDTH), lambda i: (0, 1), pipeline_mode=SINGLE),
            pl.BlockSpec((1, G_WIDTH), const),
            pl.BlockSpec((1, G_WIDTH), const),
            pl.BlockSpec((G_GROUPS, CHUNK, CHUNK), lambda i: (0, 0, 0)),
            pl.BlockSpec((CHUNK, G_GROUPS), const),
        ],
        out_specs=[
            pl.BlockSpec((UV_TM, d), lambda i: (i, 0)),
            pl.BlockSpec((UV_TM, G_WIDTH), lambda i: (i, 0)),
        ],
        out_shape=[
            jax.ShapeDtypeStruct((t, d), BF16),
            jax.ShapeDtypeStruct((t, G_WIDTH), BF16),
        ],
        args=(h, g.reshape(1, d), w_in_b, w_in_b, ln_g.reshape(1, G_WIDTH), ln_b.reshape(1, G_WIDTH), w_s,
              jnp.transpose(b_s)),
        sides=[],
        semantics=("parallel",),
        name="uv_gmlp",
    )
    return n, o_a


SWA_SCALE = 1.0 / math.sqrt(SWA_HEAD_DIM)
HALF_LANES = LANES // 2
KV_PLACED_WIDTH = SWA_KV_HEADS * LANES
QKVM_BLOCKS = QKVM_WIDTH // W_COLS


def _place_heads(z):
    rows = z.shape[0]
    low_half = lax.broadcasted_iota(jnp.int32, (rows, LANES), 1) < HALF_LANES
    zero = jnp.zeros((rows, LANES), F32)
    low, high = [], []
    for pair in range(SWA_KV_HEADS // 2):
        zg = z[:, pair * LANES:(pair + 1) * LANES]
        swapped = pltpu.roll(zg, HALF_LANES, axis=1)
        low += [jnp.where(low_half, zg, zero), jnp.where(low_half, swapped, zero)]
        high += [jnp.where(low_half, zero, swapped), jnp.where(low_half, zero, zg)]
    return (jnp.concatenate(low, axis=1).astype(BF16), jnp.concatenate(high, axis=1).astype(BF16))


def _qkvm_body(n_ref, wq0_ref, wq1_ref, wkv_ref, wm0_ref, wm1_ref,
               q_ref, klo_ref, khi_ref, vlo_ref, vhi_ref, mq_ref):
    n = n_ref[...]
    q_ref[:, :W_COLS] = (_dot(n, wq0_ref[...]) * SWA_SCALE).astype(BF16)
    q_ref[:, W_COLS:] = (_dot(n, wq1_ref[...]) * SWA_SCALE).astype(BF16)
    kv = _dot(n, wkv_ref[...])
    klo_ref[...], khi_ref[...] = _place_heads(kv[:, :SWA_KV_WIDTH])
    vlo_ref[...], vhi_ref[...] = _place_heads(kv[:, SWA_KV_WIDTH:])
    mq_ref[:, :W_COLS] = _dot(n, wm0_ref[...]).astype(BF16)
    mq_ref[:, W_COLS:] = _dot(n, wm1_ref[...]).astype(BF16)


def _qkvm(n, w_in_b, make_sides=()):
    t, d = n.shape
    assert SWA_WIDTH == 2 * W_COLS and 2 * SWA_KV_WIDTH == W_COLS and MEM_WIDTH == 2 * W_COLS
    first = UV_END // W_COLS
    tm = PROJ_TM_HOST if make_sides else PROJ_TM
    widths = (SWA_WIDTH,) + (KV_PLACED_WIDTH,) * 4 + (MEM_WIDTH,)
    w_specs = [pl.BlockSpec((d, W_COLS), functools.partial(lambda c, i: (0, c), first + c),
                            pipeline_mode=SINGLE) for c in range(QKVM_BLOCKS)]
    outs, side_out = _hosted_call(
        _qkvm_body,
        grid=(t // tm,),
        step_of=lambda i: i,
        in_specs=[pl.BlockSpec((tm, d), lambda i: (i, 0))] + w_specs,
        out_specs=[pl.BlockSpec((tm, w_), lambda i: (i, 0)) for w_ in widths],
        out_shape=[jax.ShapeDtypeStruct((t, w_), BF16) for w_ in widths],
        args=(n,) + (w_in_b,) * QKVM_BLOCKS,
        sides=[m(lambda i: i) for m in make_sides],
        semantics=("arbitrary",),
        name="qkvm",
    )
    return outs, side_out


GATE_BLOCKS = D_MODEL // W_COLS


def _gates_body(n_ref, *refs):
    w_refs, o_ref = refs[:GATE_BLOCKS], refs[GATE_BLOCKS]
    n = n_ref[...]
    for c, w_ref in enumerate(w_refs):
        z = _dot(n, w_ref[...])
        o_ref[:, c * W_COLS:(c + 1) * W_COLS] = (0.5 * jnp.tanh(0.5 * z) + 0.5).astype(BF16)


def _gates(n, w_in_b, make_sides):
    t, d = n.shape
    tiles = t // GATES_TM
    first = QKVM_END // W_COLS
    step_of = lambda b, i: b * tiles + i
    w_specs = [pl.BlockSpec((d, W_COLS), functools.partial(lambda c, b, i: (0, first + b * GATE_BLOCKS + c), c))
               for c in range(GATE_BLOCKS)]
    (gates,), side_out = _hosted_call(
        _gates_body,
        grid=(N_BRANCH, tiles),
        step_of=step_of,
        in_specs=[pl.BlockSpec((GATES_TM, d), lambda b, i: (i, 0))] + w_specs,
        out_specs=[pl.BlockSpec((None, GATES_TM, d), lambda b, i: (b, i, 0))],
        out_shape=[jax.ShapeDtypeStruct((N_BRANCH, t, d), BF16)],
        args=(n,) + (w_in_b,) * GATE_BLOCKS,
        sides=[m(step_of) for m in make_sides],
        semantics=("arbitrary", "arbitrary"),
        name="gates",
    )
    return gates, side_out


def _mkv_body(m_ref, g_ref, w_ref, mk_ref, mv_ref):
    n = _rmsnorm_f32(m_ref[...], g_ref[...]).astype(BF16)
    z = _dot(n, w_ref[...]).astype(BF16)
    mk_ref[...] = z[:, :MEM_WIDTH]
    mv_ref[...] = z[:, MEM_WIDTH:]


def _mkv(mem, g, w):
    t, d = mem.shape
    return pl.pallas_call(
        _mkv_body,
        grid=(t // MEM_LEN,),
        in_specs=[
            pl.BlockSpec((MEM_LEN, d), lambda i: (i, 0)),
            pl.BlockSpec((1, d), lambda i: (0, 0)),
            pl.BlockSpec((d, 2 * MEM_WIDTH), lambda i: (0, 0), pipeline_mode=SINGLE),
        ],
        out_specs=[pl.BlockSpec((MEM_LEN, MEM_WIDTH), lambda i: (i, 0))] * 2,
        out_shape=[jax.ShapeDtypeStruct((t, MEM_WIDTH), BF16)] * 2,
        compiler_params=_params("parallel"),
        name="mem_kv",
    )(mem, g.reshape(1, d), w)


SWA_SLOPES = tuple(2.0 ** (-8.0 * (h + 1) / SWA_HEADS) for h in range(SWA_HEADS))


def _swa_body(sink_ref, q_ref, klo_ref, khi_ref, vlo_ref, vhi_ref,
              klo_p_ref, khi_p_ref, vlo_p_ref, vhi_p_ref, o_ref):
    w = WINDOW
    t = pl.program_id(1)
    i2 = lax.broadcasted_iota(jnp.int32, (2 * w, w), 0) & (w - 1)
    j2 = lax.broadcasted_iota(jnp.int32, (2 * w, w), 1)
    from_prev = j2 > i2
    dist = ((i2 - j2) & (w - 1)).astype(F32)
    no_prev = j2 > jnp.maximum(i2, jnp.where(t == 0, -1, w))
    low_half = j2 < HALF_LANES
    zero = jnp.zeros((2 * w, w), F32)
    ones_cols = (jnp.where(low_half, 1.0, 0.0).astype(BF16), jnp.where(low_half, 0.0, 1.0).astype(BF16))

    def rows2(top, bottom):
        return jnp.concatenate([jnp.full((w, w), top, F32), jnp.full((w, w), bottom, F32)], axis=0)

    for blk in range(SWA_TQ // w):
        rs = slice(blk * w, (blk + 1) * w)

        def band(cur_ref, prev_ref, cols):
            prev = prev_ref[:, cols] if blk == 0 else cur_ref[(blk - 1) * w:blk * w, cols]
            return jnp.concatenate([prev, cur_ref[rs, cols]], axis=0)

        for kvh in range(SWA_KV_HEADS):
            cols = slice(kvh * LANES, (kvh + 1) * LANES)
            q2 = jnp.concatenate([q_ref[rs, (2 * kvh) * LANES:(2 * kvh + 1) * LANES],
                                  q_ref[rs, (2 * kvh + 1) * LANES:(2 * kvh + 2) * LANES]], axis=0)
            keys = jnp.concatenate([band(klo_ref, klo_p_ref, cols), band(khi_ref, khi_p_ref, cols)], axis=0)
            s = _dot_nt(q2, keys)
            acc = None
            sink_terms = []
            for par, (v_ref, v_p_ref) in enumerate(((vlo_ref, vlo_p_ref), (vhi_ref, vhi_p_ref))):
                heads = (kvh * SWA_REP + par, kvh * SWA_REP + 2 + par)
                base = par * 2 * w
                sc = jnp.where(from_prev, s[:, base:base + w], s[:, base + w:base + 2 * w])
                sc = sc - rows2(SWA_SLOPES[heads[0]], SWA_SLOPES[heads[1]]) * dist
                if blk == 0:
                    sc = jnp.where(no_prev, NEG, sc)
                sink = rows2(sink_ref[heads[0]], sink_ref[heads[1]])
                m = jnp.maximum(jnp.broadcast_to(jnp.max(sc, axis=-1, keepdims=True), (2 * w, w)), sink)
                p = jnp.exp(sc - m)
                pcat = jnp.concatenate([jnp.where(from_prev, p, zero).astype(BF16),
                                        jnp.where(from_prev, zero, p).astype(BF16)], axis=1)
                vext = jnp.concatenate([band(v_ref, v_p_ref, cols), ones_cols[par]], axis=1)
                part = _dot(pcat, vext)
                acc = part if acc is None else acc + part
                sink_terms.append(jnp.exp(sink - m))
            denom = acc[:, w:] + jnp.where(low_half, sink_terms[0], sink_terms[1])
            out = (acc[:, :w] / denom).astype(BF16)
            o_ref[rs, (2 * kvh) * LANES:(2 * kvh + 1) * LANES] = out[:w]
            o_ref[rs, (2 * kvh + 1) * LANES:(2 * kvh + 2) * LANES] = out[w:]


def _swa(q, k_lo, k_hi, v_lo, v_hi, sinks, batch, seq):
    t = q.shape[0]
    steps = seq // SWA_TQ
    blocks_per_step = SWA_TQ // WINDOW
    blocks_per_seq = seq // WINDOW

    def cur(b, s):
        return (b * steps + s, 0)

    def prev(b, s):
        return (b * blocks_per_seq + jnp.maximum(s * blocks_per_step - 1, 0), 0)

    kv_cur = pl.BlockSpec((SWA_TQ, KV_PLACED_WIDTH), cur)
    kv_prev = pl.BlockSpec((WINDOW, KV_PLACED_WIDTH), prev)
    return pl.pallas_call(
        _swa_body,
        grid=(batch, steps),
        in_specs=[pl.BlockSpec(memory_space=pltpu.SMEM), pl.BlockSpec((SWA_TQ, SWA_WIDTH), cur)]
        + [kv_cur] * 4 + [kv_prev] * 4,
        out_specs=pl.BlockSpec((SWA_TQ, SWA_WIDTH), cur),
        out_shape=jax.ShapeDtypeStruct((t, SWA_WIDTH), BF16),
        compiler_params=_params("parallel", "parallel"),
        name="swa",
    )(sinks, q, k_lo, k_hi, v_lo, v_hi, k_lo, k_hi, v_lo, v_hi)


MEM_SCALE = 1.0 / math.sqrt(MEM_HEAD_DIM)


def _memattn_body(q_ref, mk_ref, mv_ref, o_ref):
    for h in range(MEM_HEADS):
        hs = slice(h * MEM_HEAD_DIM, (h + 1) * MEM_HEAD_DIM)
        s = _dot_nt(q_ref[:, hs], mk_ref[:, hs]) * MEM_SCALE
        m = jnp.max(s, axis=-1, keepdims=True)
        p = jnp.exp(s - m)
        probs = (p / jnp.sum(p, axis=-1, keepdims=True)).astype(BF16)
        o_ref[:, hs] = _dot(probs, mv_ref[:, hs]).astype(BF16)


def _memattn(mq, mk, mv, batch, seq):
    t = mq.shape[0]
    steps = seq // MEM_TQ
    return pl.pallas_call(
        _memattn_body,
        grid=(batch, steps),
        in_specs=[
            pl.BlockSpec((MEM_TQ, MEM_WIDTH), lambda b, s: (b * steps + s, 0)),
            pl.BlockSpec((MEM_LEN, MEM_WIDTH), lambda b, s: (b, 0)),
            pl.BlockSpec((MEM_LEN, MEM_WIDTH), lambda b, s: (b, 0)),
        ],
        out_specs=pl.BlockSpec((MEM_TQ, MEM_WIDTH), lambda b, s: (b * steps + s, 0)),
        out_shape=jax.ShapeDtypeStruct((t, MEM_WIDTH), BF16),
        compiler_params=_params("parallel", "parallel"),
        name="mem_attn",
    )(mq, mk, mv)


def _merge_body(h_ref, gt_ref, oa_ref, ob_ref, oc_ref, wbr_ref, wo_ref, o_ref):
    y = gt_ref[0].astype(F32) * _dot(oa_ref[...], wbr_ref[0])
    y += gt_ref[1].astype(F32) * _dot(ob_ref[...], wbr_ref[1])
    y += gt_ref[2].astype(F32) * _dot(oc_ref[...], wbr_ref[2])
    o_ref[...] = h_ref[...] + _dot(y.astype(BF16), wo_ref[...])


def _merge(h, gates, o_a, o_b, o_c, w_branch, w_out, make_sides):
    t, d = h.shape
    row = lambda i: (i, 0)
    (out,), side_out = _hosted_call(
        _merge_body,
        grid=(t // MERGE_TM,),
        step_of=lambda i: i,
        in_specs=[
            pl.BlockSpec((MERGE_TM, d), row),
            pl.BlockSpec((N_BRANCH, MERGE_TM, d), lambda i: (0, i, 0)),
            pl.BlockSpec((MERGE_TM, BRANCH_WIDTH), row),
            pl.BlockSpec((MERGE_TM, BRANCH_WIDTH), row),
            pl.BlockSpec((MERGE_TM, BRANCH_WIDTH), row),
            pl.BlockSpec((N_BRANCH, BRANCH_WIDTH, d), lambda i: (0, 0, 0), pipeline_mode=SINGLE),
            pl.BlockSpec((d, d), lambda i: (0, 0), pipeline_mode=SINGLE),
        ],
        out_specs=[pl.BlockSpec((MERGE_TM, d), row)],
        out_shape=[jax.ShapeDtypeStruct((t, d), F32)],
        args=(h, gates, o_a, o_b, o_c, w_branch, w_out),
        sides=[m(lambda i: i) for m in make_sides],
        semantics=("arbitrary",),
        name="merge",
    )
    return out, side_out


def kernel(x, mem, g_ffn1, w_ffn1_in, w_ffn1_out, g_mix, w_in, gmlp_ln_g, gmlp_ln_b, w_s, b_s, swa_sinks, g_mem, w_mem_kv, w_branch, w_out, g_ffn2, w_ffn2_in, w_ffn2_out, g_final):
    batch, seq, d = x.shape
    depth = w_in.shape[0]
    xt = x.reshape(batch * seq, d)
    memt = mem.reshape(batch * MEM_LEN, d)
    w_branch_rows = w_branch.reshape(depth, N_BRANCH * BRANCH_WIDTH, d)

    def mixer_casts(l, rows):
        return [functools.partial(_side_rows, w_in, l, rows),
                functools.partial(_side_rows, w_mem_kv, l, rows),
                functools.partial(_side_rows, w_out, l, rows),
                functools.partial(_side_rows, w_branch_rows, l, 2 * rows)]

    def ffn_casts(w_in_ffn, w_out_ffn, l, rows_in=CAST_ROWS // 2, rows_out=CAST_ROWS):
        return [functools.partial(_side_ffn_in, w_in_ffn, l, rows_in),
                functools.partial(_side_rows, w_out_ffn, l, rows_out)]

    wab1, wo1 = [_cast_alone(m) for m in ffn_casts(w_ffn1_in, w_ffn1_out, 0, ALONE_IN_ROWS, ALONE_OUT_ROWS)]

    mixer = None
    for l in range(depth):
        more = l + 1 < depth
        h, ffn_side = _ffn(xt, g_ffn1[l], wab1, wo1, g_final, False,
                           mixer_casts(l, MIXER_CAST_ROWS) if mixer is None else ())
        w_in_b, w_mkv, w_o, w_br = ffn_side if mixer is None else mixer
        n, o_a = _uv(h, g_mix[l], w_in_b, gmlp_ln_g[l], gmlp_ln_b[l], w_s[l], b_s[l])
        (q, k_lo, k_hi, v_lo, v_hi, mq), mixer = _qkvm(n, w_in_b, mixer_casts(l + 1, CAST_ROWS // 2) if more else ())
        gates, (wab2, wo2) = _gates(n, w_in_b, ffn_casts(w_ffn2_in, w_ffn2_out, l))
        mk, mv = _mkv(memt, g_mem[l], w_mkv)
        o_b = _swa(q, k_lo, k_hi, v_lo, v_hi, swa_sinks[l], batch, seq)
        o_c = _memattn(mq, mk, mv, batch, seq)
        h, merge_side = _merge(h, gates, o_a, o_b, o_c, w_br.reshape(N_BRANCH, BRANCH_WIDTH, d), w_o,
                               ffn_casts(w_ffn1_in, w_ffn1_out, l + 1) if more else [])
        xt, _ = _ffn(h, g_ffn2[l], wab2, wo2, g_final, not more)
        if more:
            wab1, wo1 = merge_side
    return xt.reshape(batch, seq, d)
```

```python
import functools
import math
from typing import Callable, NamedTuple

import jax
import jax.numpy as jnp
from jax import lax
from jax.experimental import pallas as pl
from jax.experimental.pallas import tpu as pltpu

D_MODEL = 2048
MEM_LEN = 256
CHUNK = 128
WINDOW = 128
G_GROUPS = 4
G_WIDTH = 1024
G_GROUP_DIM = G_WIDTH // G_GROUPS
SWA_HEADS = 16
SWA_KV_HEADS = 4
SWA_HEAD_DIM = 64
SWA_REP = SWA_HEADS // SWA_KV_HEADS
SWA_WIDTH = SWA_HEADS * SWA_HEAD_DIM
SWA_KV_WIDTH = SWA_KV_HEADS * SWA_HEAD_DIM
MEM_HEADS = 4
MEM_HEAD_DIM = 256
MEM_WIDTH = MEM_HEADS * MEM_HEAD_DIM
N_BRANCH = 3
BRANCH_WIDTH = 1024
D_FF = 5504
EPS = 1e-6
NEG = -1e30

F32 = jnp.float32
BF16 = jnp.bfloat16

LANES = 128
VMEM_LIMIT_BYTES = 60 * 1024 * 1024

FFN_TM = 1024
FFN_TF = 1024
FFN_TF_HOST = 512
FFN_TN = 512
UV_TM = 1024
PROJ_TM = 1024
GATES_TM = 1024
MERGE_TM = 256
SWA_TQ = 1024
MEM_TQ = 2048
W_COLS = 512
CAST_ROWS = 128
MIXER_CAST_ROWS = 16
ALONE_IN_ROWS = 256
ALONE_OUT_ROWS = D_FF // 8

SINGLE = pl.Buffered(1)


def _params(*semantics):
    return pltpu.CompilerParams(dimension_semantics=semantics, vmem_limit_bytes=VMEM_LIMIT_BYTES)


def _rmsnorm_f32(xf, g):
    return xf * lax.rsqrt(jnp.mean(xf * xf, axis=-1, keepdims=True) + EPS) * g


def _dot(a, b):
    return jnp.dot(a, b, preferred_element_type=F32)


def _dot_nt(a, b):
    return lax.dot_general(a, b, (((1,), (1,)), ((), ())), preferred_element_type=F32)


class _SideCast(NamedTuple):
    srcs: tuple
    in_specs: tuple
    out_spec: pl.BlockSpec
    out_shape: jax.ShapeDtypeStruct
    write: Callable
    n_blocks: int


def _side_rows(w, l, rows_blk, step_of):
    _, rows, cols = w.shape
    n_blocks = rows // rows_blk
    assert rows % rows_blk == 0

    def blk(*g):
        return jnp.minimum(step_of(*g), n_blocks - 1)

    def write(step, in_refs, out_ref):
        @pl.when(step < n_blocks)
        def _():
            out_ref[...] = in_refs[0][...].astype(BF16)

    return _SideCast(
        (w,),
        (pl.BlockSpec((None, rows_blk, cols), lambda *g: (l, blk(*g), 0)),),
        pl.BlockSpec((rows_blk, cols), lambda *g: (blk(*g), 0)),
        jax.ShapeDtypeStruct((rows, cols), BF16), write, n_blocks)


def _side_ffn_in(w, l, rows_blk, step_of):
    _, d, _ = w.shape
    assert d % rows_blk == 0 and D_FF % LANES == 0
    n_blocks = d // rows_blk

    def blk(*g):
        return jnp.minimum(step_of(*g), n_blocks - 1)

    def write(step, in_refs, out_ref):
        @pl.when(step < n_blocks)
        def _():
            for half, in_ref in enumerate(in_refs):
                for k in range(D_FF // LANES):
                    dst = (2 * k + half) * LANES
                    out_ref[:, dst:dst + LANES] = in_ref[:, k * LANES:(k + 1) * LANES].astype(BF16)

    half_spec = lambda half: pl.BlockSpec((None, rows_blk, D_FF), lambda *g: (l, blk(*g), half))
    return _SideCast(
        (w, w),
        (half_spec(0), half_spec(1)),
        pl.BlockSpec((rows_blk, 2 * D_FF), lambda *g: (blk(*g), 0)),
        jax.ShapeDtypeStruct((d, 2 * D_FF), BF16), write, n_blocks)


def _hosted_call(body, *, grid, step_of, in_specs, out_specs, out_shape, args, sides, semantics, name,
                 scratch_shapes=()):
    n_in, n_out, k = len(in_specs), len(out_specs), len(sides)
    side_in_counts = [len(s.in_specs) for s in sides]
    n_side_in = sum(side_in_counts)
    n_steps = math.prod(grid)
    assert all(s.n_blocks <= n_steps for s in sides)

    def hosted(*refs):
        ins, side_ins = refs[:n_in], refs[n_in:n_in + n_side_in]
        outs = refs[n_in + n_side_in:n_in + n_side_in + n_out]
        side_outs = refs[n_in + n_side_in + n_out:n_in + n_side_in + n_out + k]
        scratch = refs[n_in + n_side_in + n_out + k:]
        if k:
            step = step_of(*[pl.program_id(a) for a in range(len(grid))])
            first = 0
            for side, count, o_ref in zip(sides, side_in_counts, side_outs):
                side.write(step, side_ins[first:first + count], o_ref)
                first += count
        body(*ins, *outs, *scratch)

    res = pl.pallas_call(
        hosted,
        grid=grid,
        in_specs=list(in_specs) + [spec for s in sides for spec in s.in_specs],
        out_specs=list(out_specs) + [s.out_spec for s in sides],
        out_shape=list(out_shape) + [s.out_shape for s in sides],
        scratch_shapes=list(scratch_shapes),
        compiler_params=_params(*semantics),
        name=name,
    )(*args, *[src for s in sides for src in s.srcs])
    return res[:n_out], res[n_out:]


def _cast_alone(make_side):
    side = make_side(lambda i: i)
    _, (out,) = _hosted_call(lambda: None, grid=(side.n_blocks,), step_of=lambda i: i, in_specs=[],
                             out_specs=[], out_shape=[], args=[], sides=[side],
                             semantics=("arbitrary",), name="cast")
    return out


def _ffn_step(n, w_ref, wo_ref, o_ref, units, acc_ref=None):
    acc_ref = o_ref if acc_ref is None else acc_ref
    pairs = units // LANES
    even = pairs - pairs % 2
    z = [_dot(n, w_ref[:, :2 * even * LANES])] if even else []
    if pairs % 2:
        half_rows = n.shape[0] // 2
        w_last = w_ref[:, 2 * even * LANES:2 * pairs * LANES]
        z.append(jnp.concatenate([_dot(n[:half_rows], w_last), _dot(n[half_rows:], w_last)], axis=0))
    z = jnp.concatenate(z, axis=1)
    hid = []
    for k in range(pairs):
        a = z[:, (2 * k) * LANES:(2 * k + 1) * LANES]
        b = z[:, (2 * k + 1) * LANES:(2 * k + 2) * LANES]
        half_a = 0.5 * a
        hid.append((0.5 * ((half_a * jnp.tanh(half_a) + half_a) * b)).astype(BF16))
    hid = jnp.concatenate(hid, axis=1)
    for c in range(o_ref.shape[1] // FFN_TN):
        cs = slice(c * FFN_TN, (c + 1) * FFN_TN)
        o_ref[:, cs] = acc_ref[:, cs] + _dot(hid, wo_ref[:units, cs])


def _ffn_body(x_hbm, g_ref, w_ref, wo_ref, gf_ref, o_ref, n_ref, x_ref, x_sem, *, apply_final_norm, tf):
    i, j = pl.program_id(0), pl.program_id(1)
    steps = pl.cdiv(D_FF, tf)
    last_tf = D_FF - (steps - 1) * tf
    assert steps > 2

    def x_copy(tile):
        return pltpu.make_async_copy(x_hbm.at[pl.ds(tile * FFN_TM, FFN_TM), :], x_ref, x_sem)

    @pl.when((i == 0) & (j == 0))
    def _():
        x_copy(0).start()

    @pl.when(j == 0)
    def _():
        x_copy(i).wait()
        n = _rmsnorm_f32(x_ref[...], g_ref[...]).astype(BF16)
        n_ref[...] = n
        _ffn_step(n, w_ref, wo_ref, o_ref, tf, acc_ref=x_ref)

    @pl.when((j == 1) & (i + 1 < pl.num_programs(0)))
    def _():
        x_copy(i + 1).start()

    @pl.when((j > 0) & (j < steps - 1))
    def _():
        _ffn_step(n_ref[...], w_ref, wo_ref, o_ref, tf)

    @pl.when(j == steps - 1)
    def _():
        _ffn_step(n_ref[...], w_ref, wo_ref, o_ref, last_tf)
        if apply_final_norm:
            o_ref[...] = _rmsnorm_f32(o_ref[...], gf_ref[...])


def _ffn(x, g, w_in_b, w_out_b, g_final, apply_final_norm, make_sides=()):
    t, d = x.shape
    tf = FFN_TF_HOST if make_sides else FFN_TF
    steps = pl.cdiv(D_FF, tf)
    step_of = lambda i, j: i * steps + j
    (out,), side_out = _hosted_call(
        functools.partial(_ffn_body, apply_final_norm=apply_final_norm, tf=tf),
        grid=(t // FFN_TM, steps),
        step_of=step_of,
        in_specs=[
            pl.BlockSpec(memory_space=pl.ANY),
            pl.BlockSpec((1, d), lambda i, j: (0, 0)),
            pl.BlockSpec((d, 2 * tf), lambda i, j: (0, j)),
            pl.BlockSpec((tf, d), lambda i, j: (j, 0)),
            pl.BlockSpec((1, d), lambda i, j: (0, 0)),
        ],
        out_specs=[pl.BlockSpec((FFN_TM, d), lambda i, j: (i, 0))],
        out_shape=[jax.ShapeDtypeStruct((t, d), F32)],
        args=(x, g.reshape(1, d), w_in_b, w_out_b, g_final.reshape(1, d)),
        sides=[m(step_of) for m in make_sides],
        semantics=("arbitrary", "arbitrary"),
        scratch_shapes=[pltpu.VMEM((FFN_TM, d), BF16), pltpu.VMEM((FFN_TM, d), F32),
                        pltpu.SemaphoreType.DMA(())],
        name="ffn",
    )
    return out, side_out


UV_END = 2 * G_WIDTH
QKVM_WIDTH = SWA_WIDTH + 2 * SWA_KV_WIDTH + MEM_WIDTH
QKVM_END = UV_END + QKVM_WIDTH


def _gelu(x):
    return 0.5 * x * (1.0 + lax.erf(x * math.sqrt(0.5)))


def _uv_body(h_ref, g_ref, wu_ref, wv_ref, lng_ref, lnb_ref, ws_ref, bst_ref, n_ref, oa_ref):
    row = lax.broadcasted_iota(jnp.int32, (CHUNK, CHUNK), 0)
    col = lax.broadcasted_iota(jnp.int32, (CHUNK, CHUNK), 1)
    causal = row >= col

    n = _rmsnorm_f32(h_ref[...], g_ref[...]).astype(BF16)
    n_ref[...] = n
    v = _gelu(_dot(n, wv_ref[...]))
    u = _gelu(_dot(n, wu_ref[...]))
    mu = jnp.mean(v, axis=-1, keepdims=True)
    vc = v - mu
    var = jnp.mean(vc * vc, axis=-1, keepdims=True)
    vn = (vc * lax.rsqrt(var + EPS) * lng_ref[...] + lnb_ref[...]).astype(BF16)
    for g in range(G_GROUPS):
        ws = jnp.where(causal, ws_ref[g], 0.0).astype(BF16)
        bias = bst_ref[:, g:g + 1]
        cs = slice(g * G_GROUP_DIM, (g + 1) * G_GROUP_DIM)
        for c in range(UV_TM // CHUNK):
            rs = slice(c * CHUNK, (c + 1) * CHUNK)
            mixed = _dot(ws, vn[rs, cs]) + bias
            oa_ref[rs, cs] = (u[rs, cs] * mixed).astype(BF16)


def _uv(h, g, w_in_b, ln_g, ln_b, w_s, b_s):
    t, d = h.shape
    const = lambda i: (0, 0)
    (n, o_a), _ = _hosted_call(
        _uv_body,
        grid=(t // UV_TM,),
        step_of=lambda i: i,
        in_specs=[
            pl.BlockSpec((UV_TM, d), lambda i: (i, 0)),
            pl.BlockSpec((1, d), const),
            pl.BlockSpec((d, G_WIDTH), lambda i: (0, 0), pipeline_mode=SINGLE),
            pl.BlockSpec((d, G_WIDTH), lambda i: (0, 1), pipeline_mode=SINGLE),
            pl.BlockSpec((1, G_WIDTH), const),
            pl.BlockSpec((1, G_WIDTH), const),
            pl.BlockSpec((G_GROUPS, CHUNK, CHUNK), lambda i: (0, 0, 0)),
            pl.BlockSpec((CHUNK, G_GROUPS), const),
        ],
        out_specs=[
            pl.BlockSpec((UV_TM, d), lambda i: (i, 0)),
            pl.BlockSpec((UV_TM, G_WIDTH), lambda i: (i, 0)),
        ],
        out_shape=[
            jax.ShapeDtypeStruct((t, d), BF16),
            jax.ShapeDtypeStruct((t, G_WIDTH), BF16),
        ],
        args=(h, g.reshape(1, d), w_in_b, w_in_b, ln_g.reshape(1, G_WIDTH), ln_b.reshape(1, G_WIDTH), w_s,
              jnp.transpose(b_s)),
        sides=[],
        semantics=("parallel",),
        name="uv_gmlp",
    )
    return n, o_a


SWA_SCALE = 1.0 / math.sqrt(SWA_HEAD_DIM)
HALF_LANES = LANES // 2
KV_PLACED_WIDTH = SWA_KV_HEADS * LANES
QKVM_BLOCKS = QKVM_WIDTH // W_COLS


def _place_heads(z):
    rows = z.shape[0]
    low_half = lax.broadcasted_iota(jnp.int32, (rows, LANES), 1) < HALF_LANES
    zero = jnp.zeros((rows, LANES), F32)
    low, high = [], []
    for pair in range(SWA_KV_HEADS // 2):
        zg = z[:, pair * LANES:(pair + 1) * LANES]
        swapped = pltpu.roll(zg, HALF_LANES, axis=1)
        low += [jnp.where(low_half, zg, zero), jnp.where(low_half, swapped, zero)]
        high += [jnp.where(low_half, zero, swapped), jnp.where(low_half, zero, zg)]
    return (jnp.concatenate(low, axis=1).astype(BF16), jnp.concatenate(high, axis=1).astype(BF16))


def _qkvm_body(n_ref, wq0_ref, wq1_ref, wkv_ref, wm0_ref, wm1_ref,
               q_ref, klo_ref, khi_ref, vlo_ref, vhi_ref, mq_ref):
    n = n_ref[...]
    q_ref[:, :W_COLS] = (_dot(n, wq0_ref[...]) * SWA_SCALE).astype(BF16)
    q_ref[:, W_COLS:] = (_dot(n, wq1_ref[...]) * SWA_SCALE).astype(BF16)
    kv = _dot(n, wkv_ref[...])
    klo_ref[...], khi_ref[...] = _place_heads(kv[:, :SWA_KV_WIDTH])
    vlo_ref[...], vhi_ref[...] = _place_heads(kv[:, SWA_KV_WIDTH:])
    mq_ref[:, :W_COLS] = _dot(n, wm0_ref[...]).astype(BF16)
    mq_ref[:, W_COLS:] = _dot(n, wm1_ref[...]).astype(BF16)


def _qkvm(n, w_in_b):
    t, d = n.shape
    assert SWA_WIDTH == 2 * W_COLS and 2 * SWA_KV_WIDTH == W_COLS and MEM_WIDTH == 2 * W_COLS
    first = UV_END // W_COLS
    widths = (SWA_WIDTH,) + (KV_PLACED_WIDTH,) * 4 + (MEM_WIDTH,)
    w_specs = [pl.BlockSpec((d, W_COLS), functools.partial(lambda c, i: (0, c), first + c),
                            pipeline_mode=SINGLE) for c in range(QKVM_BLOCKS)]
    outs, _ = _hosted_call(
        _qkvm_body,
        grid=(t // PROJ_TM,),
        step_of=lambda i: i,
        in_specs=[pl.BlockSpec((PROJ_TM, d), lambda i: (i, 0))] + w_specs,
        out_specs=[pl.BlockSpec((PROJ_TM, w_), lambda i: (i, 0)) for w_ in widths],
        out_shape=[jax.ShapeDtypeStruct((t, w_), BF16) for w_ in widths],
        args=(n,) + (w_in_b,) * QKVM_BLOCKS,
        sides=[],
        semantics=("parallel",),
        name="qkvm",
    )
    return outs


GATE_BLOCKS = D_MODEL // W_COLS


def _gates_body(n_ref, *refs):
    w_refs, o_ref = refs[:GATE_BLOCKS], refs[GATE_BLOCKS]
    n = n_ref[...]
    for c, w_ref in enumerate(w_refs):
        z = _dot(n, w_ref[...])
        o_ref[:, c * W_COLS:(c + 1) * W_COLS] = (0.5 * jnp.tanh(0.5 * z) + 0.5).astype(BF16)


def _gates(n, w_in_b, make_sides):
    t, d = n.shape
    tiles = t // GATES_TM
    first = QKVM_END // W_COLS
    step_of = lambda b, i: b * tiles + i
    w_specs = [pl.BlockSpec((d, W_COLS), functools.partial(lambda c, b, i: (0, first + b * GATE_BLOCKS + c), c))
               for c in range(GATE_BLOCKS)]
    (gates,), side_out = _hosted_call(
        _gates_body,
        grid=(N_BRANCH, tiles),
        step_of=step_of,
        in_specs=[pl.BlockSpec((GATES_TM, d), lambda b, i: (i, 0))] + w_specs,
        out_specs=[pl.BlockSpec((None, GATES_TM, d), lambda b, i: (b, i, 0))],
        out_shape=[jax.ShapeDtypeStruct((N_BRANCH, t, d), BF16)],
        args=(n,) + (w_in_b,) * GATE_BLOCKS,
        sides=[m(step_of) for m in make_sides],
        semantics=("arbitrary", "arbitrary"),
        name="gates",
    )
    return gates, side_out


def _mkv_body(m_ref, g_ref, w_ref, mk_ref, mv_ref):
    n = _rmsnorm_f32(m_ref[...], g_ref[...]).astype(BF16)
    z = _dot(n, w_ref[...]).astype(BF16)
    mk_ref[...] = z[:, :MEM_WIDTH]
    mv_ref[...] = z[:, MEM_WIDTH:]


def _mkv(mem, g, w):
    t, d = mem.shape
    return pl.pallas_call(
        _mkv_body,
        grid=(t // MEM_LEN,),
        in_specs=[
            pl.BlockSpec((MEM_LEN, d), lambda i: (i, 0)),
            pl.BlockSpec((1, d), lambda i: (0, 0)),
            pl.BlockSpec((d, 2 * MEM_WIDTH), lambda i: (0, 0), pipeline_mode=SINGLE),
        ],
        out_specs=[pl.BlockSpec((MEM_LEN, MEM_WIDTH), lambda i: (i, 0))] * 2,
        out_shape=[jax.ShapeDtypeStruct((t, MEM_WIDTH), BF16)] * 2,
        compiler_params=_params("parallel"),
        name="mem_kv",
    )(mem, g.reshape(1, d), w)


SWA_SLOPES = tuple(2.0 ** (-8.0 * (h + 1) / SWA_HEADS) for h in range(SWA_HEADS))


def _swa_body(sink_ref, q_ref, klo_ref, khi_ref, vlo_ref, vhi_ref,
              klo_p_ref, khi_p_ref, vlo_p_ref, vhi_p_ref, o_ref):
    w = WINDOW
    t = pl.program_id(1)
    i2 = lax.broadcasted_iota(jnp.int32, (2 * w, w), 0) & (w - 1)
    j2 = lax.broadcasted_iota(jnp.int32, (2 * w, w), 1)
    from_prev = j2 > i2
    dist = ((i2 - j2) & (w - 1)).astype(F32)
    no_prev = j2 > jnp.maximum(i2, jnp.where(t == 0, -1, w))
    low_half = j2 < HALF_LANES
    zero = jnp.zeros((2 * w, w), F32)
    ones_cols = (jnp.where(low_half, 1.0, 0.0).astype(BF16), jnp.where(low_half, 0.0, 1.0).astype(BF16))

    def rows2(top, bottom):
        return jnp.concatenate([jnp.full((w, w), top, F32), jnp.full((w, w), bottom, F32)], axis=0)

    for blk in range(SWA_TQ // w):
        rs = slice(blk * w, (blk + 1) * w)

        def band(cur_ref, prev_ref, cols):
            prev = prev_ref[:, cols] if blk == 0 else cur_ref[(blk - 1) * w:blk * w, cols]
            return jnp.concatenate([prev, cur_ref[rs, cols]], axis=0)

        for kvh in range(SWA_KV_HEADS):
            cols = slice(kvh * LANES, (kvh + 1) * LANES)
            q2 = jnp.concatenate([q_ref[rs, (2 * kvh) * LANES:(2 * kvh + 1) * LANES],
                                  q_ref[rs, (2 * kvh + 1) * LANES:(2 * kvh + 2) * LANES]], axis=0)
            keys = jnp.concatenate([band(klo_ref, klo_p_ref, cols), band(khi_ref, khi_p_ref, cols)], axis=0)
            s = _dot_nt(q2, keys)
            acc = None
            sink_terms = []
            for par, (v_ref, v_p_ref) in enumerate(((vlo_ref, vlo_p_ref), (vhi_ref, vhi_p_ref))):
                heads = (kvh * SWA_REP + par, kvh * SWA_REP + 2 + par)
                base = par * 2 * w
                sc = jnp.where(from_prev, s[:, base:base + w], s[:, base + w:base + 2 * w])
                sc = sc - rows2(SWA_SLOPES[heads[0]], SWA_SLOPES[heads[1]]) * dist
                if blk == 0:
                    sc = jnp.where(no_prev, NEG, sc)
                sink = rows2(sink_ref[heads[0]], sink_ref[heads[1]])
                m = jnp.maximum(jnp.broadcast_to(jnp.max(sc, axis=-1, keepdims=True), (2 * w, w)), sink)
                p = jnp.exp(sc - m)
                pcat = jnp.concatenate([jnp.where(from_prev, p, zero).astype(BF16),
                                        jnp.where(from_prev, zero, p).astype(BF16)], axis=1)
                vext = jnp.concatenate([band(v_ref, v_p_ref, cols), ones_cols[par]], axis=1)
                part = _dot(pcat, vext)
                acc = part if acc is None else acc + part
                sink_terms.append(jnp.exp(sink - m))
            denom = acc[:, w:] + jnp.where(low_half, sink_terms[0], sink_terms[1])
            out = (acc[:, :w] / denom).astype(BF16)
            o_ref[rs, (2 * kvh) * LANES:(2 * kvh + 1) * LANES] = out[:w]
            o_ref[rs, (2 * kvh + 1) * LANES:(2 * kvh + 2) * LANES] = out[w:]


def _swa(q, k_lo, k_hi, v_lo, v_hi, sinks, batch, seq):
    t = q.shape[0]
    steps = seq // SWA_TQ
    blocks_per_step = SWA_TQ // WINDOW
    blocks_per_seq = seq // WINDOW

    def cur(b, s):
        return (b * steps + s, 0)

    def prev(b, s):
        return (b * blocks_per_seq + jnp.maximum(s * blocks_per_step - 1, 0), 0)

    kv_cur = pl.BlockSpec((SWA_TQ, KV_PLACED_WIDTH), cur)
    kv_prev = pl.BlockSpec((WINDOW, KV_PLACED_WIDTH), prev)
    return pl.pallas_call(
        _swa_body,
        grid=(batch, steps),
        in_specs=[pl.BlockSpec(memory_space=pltpu.SMEM), pl.BlockSpec((SWA_TQ, SWA_WIDTH), cur)]
        + [kv_cur] * 4 + [kv_prev] * 4,
        out_specs=pl.BlockSpec((SWA_TQ, SWA_WIDTH), cur),
        out_shape=jax.ShapeDtypeStruct((t, SWA_WIDTH), BF16),
        compiler_params=_params("parallel", "parallel"),
        name="swa",
    )(sinks, q, k_lo, k_hi, v_lo, v_hi, k_lo, k_hi, v_lo, v_hi)


MEM_SCALE = 1.0 / math.sqrt(MEM_HEAD_DIM)


def _memattn_body(q_ref, mk_ref, mv_ref, o_ref):
    for h in range(MEM_HEADS):
        hs = slice(h * MEM_HEAD_DIM, (h + 1) * MEM_HEAD_DIM)
        s = _dot_nt(q_ref[:, hs], mk_ref[:, hs]) * MEM_SCALE
        m = jnp.max(s, axis=-1, keepdims=True)
        p = jnp.exp(s - m)
        probs = (p / jnp.sum(p, axis=-1, keepdims=True)).astype(BF16)
        o_ref[:, hs] = _dot(probs, mv_ref[:, hs]).astype(BF16)


def _memattn(mq, mk, mv, batch, seq):
    t = mq.shape[0]
    steps = seq // MEM_TQ
    return pl.pallas_call(
        _memattn_body,
        grid=(batch, steps),
        in_specs=[
            pl.BlockSpec((MEM_TQ, MEM_WIDTH), lambda b, s: (b * steps + s, 0)),
            pl.BlockSpec((MEM_LEN, MEM_WIDTH), lambda b, s: (b, 0)),
            pl.BlockSpec((MEM_LEN, MEM_WIDTH), lambda b, s: (b, 0)),
        ],
        out_specs=pl.BlockSpec((MEM_TQ, MEM_WIDTH), lambda b, s: (b * steps + s, 0)),
        out_shape=jax.ShapeDtypeStruct((t, MEM_WIDTH), BF16),
        compiler_params=_params("parallel", "parallel"),
        name="mem_attn",
    )(mq, mk, mv)


def _attn_body(*refs):
    swa_refs, (mq_ref, mk_ref, mv_ref), (ob_ref, oc_ref) = refs[:10], refs[10:13], refs[13:]
    _swa_body(*swa_refs, ob_ref)
    _memattn_body(mq_ref, mk_ref, mv_ref, oc_ref)


def _attn(q, k_lo, k_hi, v_lo, v_hi, sinks, mq, mk, mv, batch, seq):
    t = q.shape[0]
    steps = seq // SWA_TQ
    blocks_per_step = SWA_TQ // WINDOW
    blocks_per_seq = seq // WINDOW

    def cur(b, s):
        return (b * steps + s, 0)

    def prev(b, s):
        return (b * blocks_per_seq + jnp.maximum(s * blocks_per_step - 1, 0), 0)

    kv_cur = pl.BlockSpec((SWA_TQ, KV_PLACED_WIDTH), cur)
    kv_prev = pl.BlockSpec((WINDOW, KV_PLACED_WIDTH), prev)
    mem_spec = pl.BlockSpec((MEM_LEN, MEM_WIDTH), lambda b, s: (b, 0))
    return pl.pallas_call(
        _attn_body,
        grid=(batch, steps),
        in_specs=[pl.BlockSpec(memory_space=pltpu.SMEM), pl.BlockSpec((SWA_TQ, SWA_WIDTH), cur)]
        + [kv_cur] * 4 + [kv_prev] * 4 + [pl.BlockSpec((SWA_TQ, MEM_WIDTH), cur), mem_spec, mem_spec],
        out_specs=[pl.BlockSpec((SWA_TQ, SWA_WIDTH), cur), pl.BlockSpec((SWA_TQ, MEM_WIDTH), cur)],
        out_shape=[jax.ShapeDtypeStruct((t, SWA_WIDTH), BF16), jax.ShapeDtypeStruct((t, MEM_WIDTH), BF16)],
        compiler_params=_params("parallel", "parallel"),
        name="attn",
    )(sinks, q, k_lo, k_hi, v_lo, v_hi, k_lo, k_hi, v_lo, v_hi, mq, mk, mv)


def _merge_body(h_ref, gt_ref, oa_ref, ob_ref, oc_ref, wbr_ref, wo_ref, o_ref):
    y = gt_ref[0].astype(F32) * _dot(oa_ref[...], wbr_ref[0])
    y += gt_ref[1].astype(F32) * _dot(ob_ref[...], wbr_ref[1])
    y += gt_ref[2].astype(F32) * _dot(oc_ref[...], wbr_ref[2])
    o_ref[...] = h_ref[...] + _dot(y.astype(BF16), wo_ref[...])


def _merge(h, gates, o_a, o_b, o_c, w_branch, w_out, make_sides):
    t, d = h.shape
    row = lambda i: (i, 0)
    (out,), side_out = _hosted_call(
        _merge_body,
        grid=(t // MERGE_TM,),
        step_of=lambda i: i,
        in_specs=[
            pl.BlockSpec((MERGE_TM, d), row),
            pl.BlockSpec((N_BRANCH, MERGE_TM, d), lambda i: (0, i, 0)),
            pl.BlockSpec((MERGE_TM, BRANCH_WIDTH), row),
            pl.BlockSpec((MERGE_TM, BRANCH_WIDTH), row),
            pl.BlockSpec((MERGE_TM, BRANCH_WIDTH), row),
            pl.BlockSpec((N_BRANCH, BRANCH_WIDTH, d), lambda i: (0, 0, 0), pipeline_mode=SINGLE),
            pl.BlockSpec((d, d), lambda i: (0, 0), pipeline_mode=SINGLE),
        ],
        out_specs=[pl.BlockSpec((MERGE_TM, d), row)],
        out_shape=[jax.ShapeDtypeStruct((t, d), F32)],
        args=(h, gates, o_a, o_b, o_c, w_branch, w_out),
        sides=[m(lambda i: i) for m in make_sides],
        semantics=("arbitrary",),
        name="merge",
    )
    return out, side_out


def kernel(x, mem, g_ffn1, w_ffn1_in, w_ffn1_out, g_mix, w_in, gmlp_ln_g, gmlp_ln_b, w_s, b_s, swa_sinks, g_mem, w_mem_kv, w_branch, w_out, g_ffn2, w_ffn2_in, w_ffn2_out, g_final):
    batch, seq, d = x.shape
    depth = w_in.shape[0]
    xt = x.reshape(batch * seq, d)
    memt = mem.reshape(batch * MEM_LEN, d)
    w_branch_rows = w_branch.reshape(depth, N_BRANCH * BRANCH_WIDTH, d)

    def mixer_casts(l):
        return [functools.partial(_side_rows, w_in, l, MIXER_CAST_ROWS),
                functools.partial(_side_rows, w_mem_kv, l, MIXER_CAST_ROWS),
                functools.partial(_side_rows, w_out, l, MIXER_CAST_ROWS),
                functools.partial(_side_rows, w_branch_rows, l, 2 * MIXER_CAST_ROWS)]

    def ffn_casts(w_in_ffn, w_out_ffn, l, rows_in=CAST_ROWS // 2, rows_out=CAST_ROWS):
        return [functools.partial(_side_ffn_in, w_in_ffn, l, rows_in),
                functools.partial(_side_rows, w_out_ffn, l, rows_out)]

    wab1, wo1 = [_cast_alone(m) for m in ffn_casts(w_ffn1_in, w_ffn1_out, 0, ALONE_IN_ROWS, ALONE_OUT_ROWS)]

    for l in range(depth):
        more = l + 1 < depth
        h, (w_in_b, w_mkv, w_o, w_br) = _ffn(xt, g_ffn1[l], wab1, wo1, g_final, False, mixer_casts(l))
        n, o_a = _uv(h, g_mix[l], w_in_b, gmlp_ln_g[l], gmlp_ln_b[l], w_s[l], b_s[l])
        q, k_lo, k_hi, v_lo, v_hi, mq = _qkvm(n, w_in_b)
        gates, (wab2, wo2) = _gates(n, w_in_b, ffn_casts(w_ffn2_in, w_ffn2_out, l))
        mk, mv = _mkv(memt, g_mem[l], w_mkv)
        o_b, o_c = _attn(q, k_lo, k_hi, v_lo, v_hi, swa_sinks[l], mq, mk, mv, batch, seq)
        h, merge_side = _merge(h, gates, o_a, o_b, o_c, w_br.reshape(N_BRANCH, BRANCH_WIDTH, d), w_o,
                               ffn_casts(w_ffn1_in, w_ffn1_out, l + 1) if more else [])
        xt, _ = _ffn(h, g_ffn2[l], wab2, wo2, g_final, not more)
        if more:
            wab1, wo1 = merge_side
    return xt.reshape(batch, seq, d)
```
